```python
import math
import jax, jax.numpy as jnp
from jax import lax
import numpy as np

D_MODEL = 1024
BATCH = 16
SEQ = 2048
DEPTH = 2

CHUNK = 64
Q_BLOCK = 128
N_MEM = 256
N_A_LAYERS = DEPTH // 2
N_B_LAYERS = DEPTH - N_A_LAYERS

MLSTM_HEADS = 4
MLSTM_QK_DIM = D_MODEL // 16
MLSTM_V_DIM = D_MODEL // 8
MLSTM_WIDTH = MLSTM_HEADS * MLSTM_V_DIM

MEM_HEADS = 4
MEM_HEAD_DIM = D_MODEL // 8
MEM_WIDTH = MEM_HEADS * MEM_HEAD_DIM

MLA_HEADS = 8
MLA_NOPE = 64
MLA_ROPE = 32
MLA_V = 64
Q_LORA = D_MODEL // 4
KV_LORA = D_MODEL // 4
MLA_WIDTH = MLA_HEADS * MLA_V

D_FF = 4 * D_MODEL
ROPE_THETA = 10000.0
LN_EPS = 1e-5
RMS_EPS = 1e-6
ALPHA = (2 * DEPTH) ** 0.25
BETA = (8 * DEPTH) ** -0.25

A_SPLITS = [MLSTM_HEADS * MLSTM_QK_DIM, MLSTM_HEADS * MLSTM_QK_DIM, MLSTM_WIDTH,
            MLSTM_WIDTH, MLSTM_HEADS, MLSTM_HEADS, MEM_WIDTH]
A_IN_WIDTH = sum(A_SPLITS)
B_IN_WIDTH = Q_LORA + MEM_WIDTH

kernel_name = "yoco_mlstm_mla_memory_deepnorm"


def _offsets(sizes):
    out, acc = [], 0
    for s in sizes[:-1]:
        acc += s
        out.append(acc)
    return out


def layer_norm(x, g, b):
    xf = x.astype(jnp.float32)
    mu = jnp.mean(xf, axis=-1, keepdims=True)
    var = jnp.mean(jnp.square(xf - mu), axis=-1, keepdims=True)
    return ((xf - mu) * lax.rsqrt(var + LN_EPS) * g + b).astype(x.dtype)


def rms_norm(x, g):
    xf = x.astype(jnp.float32)
    return (xf * lax.rsqrt(jnp.mean(jnp.square(xf), axis=-1, keepdims=True) + RMS_EPS) * g).astype(x.dtype)


def apply_rope(x, cos, sin):
    half = x.shape[-1] // 2
    x1, x2 = x[..., :half], x[..., half:]
    return jnp.concatenate([x1 * cos - x2 * sin, x2 * cos + x1 * sin], axis=-1).astype(x.dtype)


def mlstm_chunkwise(q, k, v, i_pre, f_pre):
    bsz, seq, heads, dk = q.shape
    dv = v.shape[-1]
    nc = seq // CHUNK

    def chunks(t):
        return t.reshape(bsz, nc, CHUNK, heads, t.shape[-1]).transpose(0, 3, 1, 2, 4)

    def chunks_g(t):
        return t.reshape(bsz, nc, CHUNK, heads).transpose(0, 3, 1, 2)

    qc, kc, vc = chunks(q), chunks(k), chunks(v)
    a = chunks_g(i_pre.astype(jnp.float32))
    b = jnp.cumsum(chunks_g(jax.nn.log_sigmoid(f_pre.astype(jnp.float32))), axis=-1)
    g = b[..., -1]
    logw = g[..., None] - b + a

    def step(carry, inp):
        c_st, n_st, m_st = carry
        k_c, v_c, lw, g_c = inp
        m_new = jnp.maximum(g_c + m_st, jnp.max(lw, axis=-1))
        decay = jnp.exp(g_c + m_st - m_new)
        w = jnp.exp(lw - m_new[..., None])
        c_new = decay[..., None, None] * c_st + jnp.einsum("bhl,bhlk,bhlv->bhkv", w, k_c, v_c)
        n_new = decay[..., None] * n_st + jnp.einsum("bhl,bhlk->bhk", w, k_c)
        return (c_new, n_new, m_new), (c_st, n_st, m_st)

    init = (jnp.zeros((bsz, heads, dk, dv), jnp.float32),
            jnp.zeros((bsz, heads, dk), jnp.float32),
            jnp.zeros((bsz, heads), jnp.float32))
    xs = (kc.transpose(2, 0, 1, 3, 4), vc.transpose(2, 0, 1, 3, 4),
          logw.transpose(2, 0, 1, 3), g.transpose(2, 0, 1))
    _, (c_prev, n_prev, m_prev) = lax.scan(step, init, xs)
    c_prev = c_prev.transpose(1, 2, 0, 3, 4)
    n_prev = n_prev.transpose(1, 2, 0, 3)
    m_prev = m_prev.transpose(1, 2, 0)

    causal = jnp.tril(jnp.ones((CHUNK, CHUNK), dtype=bool))
    d_log = jnp.where(causal, b[..., :, None] - b[..., None, :] + a[..., None, :], -jnp.inf)
    m_inter = b + m_prev[..., None]
    m_t = jnp.maximum(m_inter, jnp.max(d_log, axis=-1))
    p = jnp.einsum("bhclk,bhcsk->bhcls", qc, kc) * jnp.exp(d_log - m_t[..., None])
    inter = jnp.exp(m_inter - m_t)
    num = (jnp.einsum("bhcls,bhcsv->bhclv", p, vc)
           + inter[..., None] * jnp.einsum("bhclk,bhckv->bhclv", qc, c_prev))
    nq = jnp.sum(p, axis=-1) + inter * jnp.einsum("bhclk,bhck->bhcl", qc, n_prev)
    h = num / jnp.maximum(jnp.abs(nq), jnp.exp(-m_t))[..., None]
    return h.transpose(0, 2, 3, 1, 4).reshape(bsz, seq, heads, dv)


def memory_attention(q_mem, mem, w_mem_kv):
    bsz, seq, _ = q_mem.shape
    kv = (mem @ w_mem_kv).reshape(bsz, mem.shape[1], 2, MEM_HEADS, MEM_HEAD_DIM)
    q = q_mem.reshape(bsz, seq, MEM_HEADS, MEM_HEAD_DIM)
    s = jnp.einsum("bshd,bmhd->bhsm", q, kv[:, :, 0]).astype(jnp.float32) * (MEM_HEAD_DIM ** -0.5)
    p = jax.nn.softmax(s, axis=-1).astype(q.dtype)
    return jnp.einsum("bhsm,bmhd->bshd", p, kv[:, :, 1]).reshape(bsz, seq, MEM_WIDTH)


def mla_chunk_causal_attention(q_nope, q_rope, k_nope, k_rope, v):
    seq = q_nope.shape[1]
    scale = (MLA_NOPE + MLA_ROPE) ** -0.5
    outs = []
    for blk in range(seq // Q_BLOCK):
        q0, q1 = blk * Q_BLOCK, (blk + 1) * Q_BLOCK
        kend = q1
        s = (jnp.einsum("bqhd,bkhd->bhqk", q_nope[:, q0:q1], k_nope[:, :kend])
             + jnp.einsum("bqhr,bkr->bhqk", q_rope[:, q0:q1], k_rope[:, :kend]))
        s = s.astype(jnp.float32) * scale
        allowed = (jnp.arange(kend) // CHUNK)[None, :] <= (jnp.arange(q0, q1) // CHUNK)[:, None]
        p = jax.nn.softmax(jnp.where(allowed, s, -jnp.inf), axis=-1).astype(v.dtype)
        outs.append(jnp.einsum("bhqk,bkhd->bqhd", p, v[:, :kend]))
    return jnp.concatenate(outs, axis=1)


def mlstm_mem_sublayer(x, mem, w_in, b_igate, b_fgate, w_mem_kv, w_out):
    bsz, seq, _ = x.shape
    q, k, v, o_pre, i_pre, f_pre, q_mem = jnp.split(x @ w_in, _offsets(A_SPLITS), axis=-1)
    q = q.reshape(bsz, seq, MLSTM_HEADS, MLSTM_QK_DIM)
    k = k.reshape(bsz, seq, MLSTM_HEADS, MLSTM_QK_DIM) * (MLSTM_QK_DIM ** -0.5)
    v = v.reshape(bsz, seq, MLSTM_HEADS, MLSTM_V_DIM)
    h = mlstm_chunkwise(q, k, v, i_pre + b_igate, f_pre + b_fgate)
    h = h.reshape(bsz, seq, MLSTM_WIDTH) * jax.nn.sigmoid(o_pre.astype(jnp.float32))
    h_mem = memory_attention(q_mem, mem, w_mem_kv)
    return (jnp.concatenate([h.astype(x.dtype), h_mem.astype(x.dtype)], axis=-1) @ w_out).astype(x.dtype)


def shared_latent_kv(h, cos, sin, w_down, norm_g, w_uk, w_uv):
    bsz, seq, _ = h.shape
    c_kv, k_rope = jnp.split(h @ w_down, [KV_LORA], axis=-1)
    c_kv = rms_norm(c_kv, norm_g)
    k_nope = (c_kv @ w_uk).reshape(bsz, seq, MLA_HEADS, MLA_NOPE)
    v = (c_kv @ w_uv).reshape(bsz, seq, MLA_HEADS, MLA_V)
    k_rope = apply_rope(k_rope, cos, sin)
    return k_nope, k_rope, v


def mla_mem_sublayer(x, mem, cos, sin, k_nope, k_rope, v, w_in, q_norm_g, w_uq, w_mem_kv, w_out):
    bsz, seq, _ = x.shape
    c_q, q_mem = jnp.split(x @ w_in, [Q_LORA], axis=-1)
    q = (rms_norm(c_q, q_norm_g) @ w_uq).reshape(bsz, seq, MLA_HEADS, MLA_NOPE + MLA_ROPE)
    q_nope, q_rope = q[..., :MLA_NOPE], q[..., MLA_NOPE:]
    q_rope = apply_rope(q_rope, cos[:, :, None, :], sin[:, :, None, :])
    o = mla_chunk_causal_attention(q_nope, q_rope, k_nope, k_rope, v).reshape(bsz, seq, MLA_WIDTH)
    h_mem = memory_attention(q_mem, mem, w_mem_kv)
    return (jnp.concatenate([o.astype(x.dtype), h_mem.astype(x.dtype)], axis=-1) @ w_out).astype(x.dtype)


def sqrelu_ffn(x, w_up, w_down):
    return (jnp.square(jax.nn.relu(x @ w_up)) @ w_down).astype(x.dtype)


def setup_inputs(seed: int = 0) -> dict:
    key = jax.random.key(seed)
    ks = jax.random.split(key, 24)
    f32 = jnp.float32

    def w(k, shape, fan_in, scale=1.0):
        return jax.random.normal(k, shape, f32) * (scale * fan_in ** -0.5)

    def gain(k, shape):
        return 1.0 + 0.02 * jax.random.normal(k, shape, f32)

    def bias(k, shape):
        return 0.02 * jax.random.normal(k, shape, f32)

    offsets = jax.random.randint(ks[2], (BATCH, 1), 0, 4096, dtype=jnp.int32)
    positions = offsets + jnp.arange(SEQ, dtype=jnp.int32)[None, :]
    return {
        "x": jax.random.normal(ks[0], (BATCH, SEQ, D_MODEL), f32),
        "mem": jax.random.normal(ks[1], (BATCH, N_MEM, D_MODEL), f32),
        "positions": positions,
        "a_w_in": w(ks[3], (N_A_LAYERS, D_MODEL, A_IN_WIDTH), D_MODEL),
        "a_b_igate": 0.1 * jax.random.normal(ks[4], (N_A_LAYERS, MLSTM_HEADS), f32),
        "a_b_fgate": 3.0 + 0.5 * jax.random.normal(ks[5], (N_A_LAYERS, MLSTM_HEADS), f32),
        "a_w_mem_kv": w(ks[6], (N_A_LAYERS, D_MODEL, 2 * MEM_WIDTH), D_MODEL),
        "a_w_out": w(ks[7], (N_A_LAYERS, MLSTM_WIDTH + MEM_WIDTH, D_MODEL), MLSTM_WIDTH + MEM_WIDTH, BETA),
        "kv_w_down": w(ks[8], (D_MODEL, KV_LORA + MLA_ROPE), D_MODEL),
        "kv_norm_g": gain(ks[9], (KV_LORA,)),
        "kv_w_uk": w(ks[10], (KV_LORA, MLA_HEADS * MLA_NOPE), KV_LORA),
        "kv_w_uv": w(ks[11], (KV_LORA, MLA_HEADS * MLA_V), KV_LORA),
        "b_w_in": w(ks[12], (N_B_LAYERS, D_MODEL, B_IN_WIDTH), D_MODEL),
        "b_q_norm_g": gain(ks[13], (N_B_LAYERS, Q_LORA)),
        "b_w_uq": w(ks[14], (N_B_LAYERS, Q_LORA, MLA_HEADS * (MLA_NOPE + MLA_ROPE)), Q_LORA),
        "b_w_mem_kv": w(ks[15], (N_B_LAYERS, D_MODEL, 2 * MEM_WIDTH), D_MODEL),
        "b_w_out": w(ks[16], (N_B_LAYERS, MLA_WIDTH + MEM_WIDTH, D_MODEL), MLA_WIDTH + MEM_WIDTH, BETA),
        "ln1_g": gain(ks[17], (DEPTH, D_MODEL)),
        "ln1_b": bias(ks[18], (DEPTH, D_MODEL)),
        "ffn_w_up": w(ks[19], (DEPTH, D_MODEL, D_FF), D_MODEL),
        "ffn_w_down": w(ks[20], (DEPTH, D_FF, D_MODEL), D_FF, BETA),
        "ln2_g": gain(ks[21], (DEPTH, D_MODEL)),
        "ln2_b": bias(ks[22], (DEPTH, D_MODEL)),
    }


def reference(x, mem, positions, a_w_in, a_b_igate, a_b_fgate, a_w_mem_kv, a_w_out,
              kv_w_down, kv_norm_g, kv_w_uk, kv_w_uv,
              b_w_in, b_q_norm_g, b_w_uq, b_w_mem_kv, b_w_out,
              ln1_g, ln1_b, ffn_w_up, ffn_w_down, ln2_g, ln2_b):
    inv_freq = ROPE_THETA ** (-jnp.arange(0, MLA_ROPE, 2, dtype=jnp.float32) / MLA_ROPE)
    ang = positions.astype(jnp.float32)[..., None] * inv_freq
    cos, sin = jnp.cos(ang), jnp.sin(ang)

    for layer in range(DEPTH):
        if layer < N_A_LAYERS:
            mix = mlstm_mem_sublayer(x, mem, a_w_in[layer], a_b_igate[layer], a_b_fgate[layer],
                                     a_w_mem_kv[layer], a_w_out[layer])
        else:
            if layer == N_A_LAYERS:
                k_nope, k_rope, v_sh = shared_latent_kv(x, cos, sin, kv_w_down, kv_norm_g,
                                                        kv_w_uk, kv_w_uv)
            j = layer - N_A_LAYERS
            mix = mla_mem_sublayer(x, mem, cos, sin, k_nope, k_rope, v_sh, b_w_in[j], b_q_norm_g[j],
                                   b_w_uq[j], b_w_mem_kv[j], b_w_out[j])
        x = layer_norm(ALPHA * x + mix, ln1_g[layer], ln1_b[layer])
        x = layer_norm(ALPHA * x + sqrelu_ffn(x, ffn_w_up[layer], ffn_w_down[layer]),
                       ln2_g[layer], ln2_b[layer])
    return x
```

```python
import functools

import jax
import jax.numpy as jnp
from jax import lax
from jax.experimental import pallas as pl
from jax.experimental.pallas import tpu as pltpu

F32 = jnp.float32
BF16 = jnp.bfloat16

D_MODEL = 1024
DEPTH = 2
N_MEM = 256
MLSTM_HEADS = 4
MLSTM_QK = 64
MLSTM_V = 128
MEM_HEADS = 4
MEM_DIM = 128
MLA_HEADS = 8
MLA_NOPE = 64
MLA_ROPE = 32
MLA_V = 64
Q_LORA = 256
KV_LORA = 256
D_FF = 4 * D_MODEL
ROPE_THETA = 10000.0
LN_EPS = 1e-5
RMS_EPS = 1e-6
ALPHA = (2 * DEPTH) ** 0.25

LANES = 128
HEAD_PAD = 128
ROPE_LO = MLA_NOPE
ROPE_HALF = MLA_ROPE // 2

TS_A = 256
TQ_B = 256
TK_B = 256
TM_FFN = 512
TM_KV = 512
TM_ROPE = 2048
TM_MEM = 512
FF_CHUNK = 1024
VMEM_LIMIT = 56 * 1024 * 1024

NT_DIMS = (((1,), (1,)), ((), ()))
TN_DIMS = (((0,), (0,)), ((), ()))


def _dot(a, b):
    return jnp.dot(a, b, preferred_element_type=F32)


def _dot_nt(a, b):
    return lax.dot_general(a, b, NT_DIMS, preferred_element_type=F32)


def _layer_norm(y, g, b):
    mu = jnp.mean(y, axis=-1, keepdims=True)
    yc = y - mu
    var = jnp.mean(yc * yc, axis=-1, keepdims=True)
    return yc * lax.rsqrt(var + LN_EPS) * g + b


def _log_sigmoid(z):
    return jnp.minimum(z, 0.0) - jnp.log(1.0 + jnp.exp(-jnp.abs(z)))


def _const_spec(shape):
    nd = len(shape)
    return pl.BlockSpec(shape, lambda *_: (0,) * nd, pipeline_mode=pl.Buffered(1))


def _rope_table_kernel(pos_ref, invf_ref, c_ref, s_ref):
    ang = pos_ref[...] * invf_ref[...]
    lane = lax.broadcasted_iota(jnp.int32, ang.shape, 1)
    is_x1 = (lane >= ROPE_LO) & (lane < ROPE_LO + ROPE_HALF)
    is_x2 = (lane >= ROPE_LO + ROPE_HALF) & (lane < ROPE_LO + MLA_ROPE)
    cos = jnp.cos(ang)
    sin = jnp.sin(ang)
    c_ref[...] = jnp.where(is_x1 | is_x2, cos, 1.0)
    s_ref[...] = jnp.where(is_x1, -sin, jnp.where(is_x2, sin, 0.0))


def _rope_tables(pos_col, invf_row):
    t = pos_col.shape[0]
    return pl.pallas_call(
        _rope_table_kernel,
        out_shape=(jax.ShapeDtypeStruct((t, LANES), F32),) * 2,
        grid=(t // TM_ROPE,),
        in_specs=[pl.BlockSpec((TM_ROPE, 1), lambda i: (i, 0)),
                  _const_spec((1, LANES))],
        out_specs=(pl.BlockSpec((TM_ROPE, LANES), lambda i: (i, 0)),) * 2,
        compiler_params=pltpu.CompilerParams(dimension_semantics=("parallel",)),
        name="rope_tables",
    )(pos_col, invf_row)


def _mem_kv_kernel(mem_ref, w_ref, o_ref):
    o_ref[...] = _dot(mem_ref[...].astype(BF16), w_ref[...]).astype(BF16)


def _mem_kv(mem2d, w):
    r, n = mem2d.shape[0], w.shape[1]
    return pl.pallas_call(
        _mem_kv_kernel,
        out_shape=jax.ShapeDtypeStruct((r, n), BF16),
        grid=(r // TM_MEM,),
        in_specs=[pl.BlockSpec((TM_MEM, D_MODEL), lambda i: (i, 0)),
                  _const_spec(w.shape)],
        out_specs=pl.BlockSpec((TM_MEM, n), lambda i: (i, 0)),
        compiler_params=pltpu.CompilerParams(dimension_semantics=("parallel",),
                                             vmem_limit_bytes=VMEM_LIMIT),
        name="mem_kv",
    )(mem2d, w)


def _memory_attention(q_all, mkv_ref, cat_ref, col0):
    scale = MEM_DIM ** -0.5
    width = MEM_HEADS * MEM_DIM
    for h in range(MEM_HEADS):
        lo = h * MEM_DIM
        qh = q_all[:, lo:lo + MEM_DIM].astype(BF16)
        kh = mkv_ref[:, lo:lo + MEM_DIM]
        vh = mkv_ref[:, width + lo:width + lo + MEM_DIM]
        s = _dot_nt(qh, kh) * scale
        p = jnp.exp(s - jnp.max(s, axis=-1, keepdims=True))
        den = jnp.sum(p, axis=-1, keepdims=True)
        o = _dot(p.astype(BF16), vh) / den
        cat_ref[:, col0 + lo:col0 + lo + MEM_DIM] = o.astype(BF16)


def _mixer_a_kernel(x_ref, wmain_ref, wgc_ref, wgr_ref, bgc_ref, bgr_ref, mkv_ref,
                    wout_ref, g_ref, b_ref, o_ref, c_st, n_st, m_st, cat_ref):
    ts = x_ref.shape[0]
    hq = MLSTM_HEADS * MLSTM_QK
    hv = MLSTM_HEADS * MLSTM_V

    @pl.when(pl.program_id(1) == 0)
    def _():
        c_st[...] = jnp.zeros_like(c_st)
        n_st[...] = jnp.zeros_like(n_st)
        m_st[...] = jnp.zeros_like(m_st)

    x = x_ref[...]
    xb = x.astype(BF16)
    proj = _dot(xb, wmain_ref[...])
    gate_c = _dot(xb, wgc_ref[...]) + bgc_ref[...]
    gate_r = _dot_nt(wgr_ref[...], xb) + bgr_ref[...]
    row = lax.broadcasted_iota(jnp.int32, (ts, ts), 0)
    col = lax.broadcasted_iota(jnp.int32, (ts, ts), 1)
    causal = col <= row
    tri_lower = causal.astype(F32)
    tri_upper = (row <= col).astype(F32)
    b_c = jnp.dot(tri_lower, _log_sigmoid(gate_c), precision=lax.Precision.HIGHEST,
                  preferred_element_type=F32)
    b_r = jnp.dot(_log_sigmoid(gate_r), tri_upper, precision=lax.Precision.HIGHEST,
                  preferred_element_type=F32)

    for h in range(MLSTM_HEADS):
        q = proj[:, h * MLSTM_QK:(h + 1) * MLSTM_QK]
        k = proj[:, hq + h * MLSTM_QK:hq + (h + 1) * MLSTM_QK] * (MLSTM_QK ** -0.5)
        v = proj[:, 2 * hq + h * MLSTM_V:2 * hq + (h + 1) * MLSTM_V].astype(BF16)
        o_pre = proj[:, 2 * hq + hv + h * MLSTM_V:2 * hq + hv + (h + 1) * MLSTM_V]
        qb = q.astype(BF16)
        a_col = gate_c[:, h:h + 1]
        a_row = gate_r[h:h + 1, :]
        bc = b_c[:, MLSTM_HEADS + h:MLSTM_HEADS + h + 1]
        br = b_r[MLSTM_HEADS + h:MLSTM_HEADS + h + 1, :]
        g_tot = bc[ts - 1:ts, :]
        c_prev = c_st[h]
        n_prev = n_st[h]
        m_prev = m_st[h]

        d_log = jnp.where(causal, bc - br + a_row, -jnp.inf)
        m_inter = bc + m_prev
        m_t = jnp.maximum(m_inter, jnp.max(d_log, axis=-1, keepdims=True))
        p = _dot_nt(qb, k.astype(BF16)) * jnp.exp(d_log - m_t)
        inter = jnp.exp(m_inter - m_t)
        num = _dot(p.astype(BF16), v) + inter * _dot(qb, c_prev.astype(BF16))
        nq = (jnp.sum(p, axis=-1, keepdims=True)
              + inter * jnp.sum(q * n_prev, axis=-1, keepdims=True))
        hh = num / jnp.maximum(jnp.abs(nq), jnp.exp(-m_t))
        hh = hh * jax.nn.sigmoid(o_pre)
        cat_ref[:, h * MLSTM_V:(h + 1) * MLSTM_V] = hh.astype(BF16)

        lw_row = g_tot - br + a_row
        lw_col = g_tot - bc + a_col
        m_new = jnp.maximum(g_tot + m_prev, jnp.max(lw_row, axis=-1, keepdims=True))
        decay = jnp.exp(g_tot + m_prev - m_new)
        kw = k * jnp.exp(lw_col - m_new)
        c_st[h] = decay * c_prev + lax.dot_general(kw.astype(BF16), v, TN_DIMS,
                                                   preferred_element_type=F32)
        n_st[h] = decay * n_prev + jnp.sum(kw, axis=0, keepdims=True)
        m_st[h] = m_new

    _memory_attention(proj[:, 2 * hq + 2 * hv:], mkv_ref, cat_ref, hv)
    mix = _dot(cat_ref[...], wout_ref[...])
    o_ref[...] = _layer_norm(ALPHA * x + mix, g_ref[...], b_ref[...])


def _mixer_a(x2d, wmain, wgc, wgr, bgc, bgr, memkv, wout, g, b, batch, seq):
    ns = seq // TS_A
    width = MLSTM_HEADS * MLSTM_V + MEM_HEADS * MEM_DIM
    return pl.pallas_call(
        _mixer_a_kernel,
        out_shape=jax.ShapeDtypeStruct(x2d.shape, F32),
        grid=(batch, ns),
        in_specs=[pl.BlockSpec((TS_A, D_MODEL), lambda bi, si: (bi * ns + si, 0)),
                  _const_spec(wmain.shape), _const_spec(wgc.shape), _const_spec(wgr.shape),
                  _const_spec(bgc.shape), _const_spec(bgr.shape),
                  pl.BlockSpec((N_MEM, 2 * MEM_HEADS * MEM_DIM), lambda bi, si: (bi, 0)),
                  _const_spec(wout.shape), _const_spec(g.shape), _const_spec(b.shape)],
        out_specs=pl.BlockSpec((TS_A, D_MODEL), lambda bi, si: (bi * ns + si, 0)),
        scratch_shapes=[pltpu.VMEM((MLSTM_HEADS, MLSTM_QK, MLSTM_V), F32),
                        pltpu.VMEM((MLSTM_HEADS, 1, MLSTM_QK), F32),
                        pltpu.VMEM((MLSTM_HEADS, 1, 1), F32),
                        pltpu.VMEM((TS_A, width), BF16)],
        compiler_params=pltpu.CompilerParams(dimension_semantics=("parallel", "arbitrary"),
                                             vmem_limit_bytes=VMEM_LIMIT),
        name="mixer_a",
    )(x2d, wmain, wgc, wgr, bgc, bgr, memkv, wout, g, b)


def _ffn_kernel(x_ref, wup_ref, wdn_ref, g_ref, b_ref, o_ref):
    x = x_ref[...]
    xb = x.astype(BF16)
    acc = jnp.zeros(x.shape, F32)
    for j in range(D_FF // FF_CHUNK):
        hid = _dot(xb, wup_ref[:, j * FF_CHUNK:(j + 1) * FF_CHUNK])
        hid = jnp.square(jnp.maximum(hid, 0.0)).astype(BF16)
        acc = acc + _dot(hid, wdn_ref[j * FF_CHUNK:(j + 1) * FF_CHUNK, :])
    o_ref[...] = _layer_norm(ALPHA * x + acc, g_ref[...], b_ref[...])


def _ffn(x2d, wup, wdn, g, b):
    t = x2d.shape[0]
    return pl.pallas_call(
        _ffn_kernel,
        out_shape=jax.ShapeDtypeStruct(x2d.shape, F32),
        grid=(t // TM_FFN,),
        in_specs=[pl.BlockSpec((TM_FFN, D_MODEL), lambda i: (i, 0)),
                  _const_spec(wup.shape), _const_spec(wdn.shape),
                  _const_spec(g.shape), _const_spec(b.shape)],
        out_specs=pl.BlockSpec((TM_FFN, D_MODEL), lambda i: (i, 0)),
        compiler_params=pltpu.CompilerParams(dimension_semantics=("parallel",),
                                             vmem_limit_bytes=VMEM_LIMIT),
        name="ffn",
    )(x2d, wup, wdn, g, b)


def _shared_kv_kernel(x_ref, wd_ref, gk_ref, wuk_ref, wuv_ref, c_ref, s_ref, k_ref, v_ref):
    xb = x_ref[...].astype(BF16)
    d = _dot(xb, wd_ref[...])
    ckv = d[:, :KV_LORA]
    ckv = ckv * lax.rsqrt(jnp.mean(ckv * ckv, axis=-1, keepdims=True) + RMS_EPS) * gk_ref[...]
    ckv = ckv.astype(BF16)
    k_rope = (d[:, KV_LORA:KV_LORA + LANES] * c_ref[...]
              + d[:, KV_LORA + LANES:KV_LORA + 2 * LANES] * s_ref[...])
    k_nope = _dot(ckv, wuk_ref[...])
    for h in range(MLA_HEADS):
        k_ref[:, h * HEAD_PAD:(h + 1) * HEAD_PAD] = (
            k_nope[:, h * HEAD_PAD:(h + 1) * HEAD_PAD] + k_rope).astype(BF16)
    v_ref[...] = _dot(ckv, wuv_ref[...]).astype(BF16)


def _shared_kv(x2d, wd, gk, wuk, wuv, ctab, stab):
    t = x2d.shape[0]
    kw, vw = MLA_HEADS * HEAD_PAD, MLA_HEADS * MLA_V
    return pl.pallas_call(
        _shared_kv_kernel,
        out_shape=(jax.ShapeDtypeStruct((t, kw), BF16), jax.ShapeDtypeStruct((t, vw), BF16)),
        grid=(t // TM_KV,),
        in_specs=[pl.BlockSpec((TM_KV, D_MODEL), lambda i: (i, 0)),
                  _const_spec(wd.shape), _const_spec(gk.shape),
                  _const_spec(wuk.shape), _const_spec(wuv.shape),
                  pl.BlockSpec((TM_KV, LANES), lambda i: (i, 0)),
                  pl.BlockSpec((TM_KV, LANES), lambda i: (i, 0))],
        out_specs=(pl.BlockSpec((TM_KV, kw), lambda i: (i, 0)),
                   pl.BlockSpec((TM_KV, vw), lambda i: (i, 0))),
        compiler_params=pltpu.CompilerParams(dimension_semantics=("parallel",),
                                             vmem_limit_bytes=VMEM_LIMIT),
        name="shared_kv",
    )(x2d, wd, gk, wuk, wuv, ctab, stab)


def _mixer_b_kernel(x_ref, win_ref, gq_ref, wuq_ref, wuqs_ref, c_ref, s_ref, k_ref, v_ref,
                    mkv_ref, wout_ref, g_ref, b_ref, o_ref, q_sc, cat_ref):
    tq = x_ref.shape[0]
    qi = pl.program_id(1)
    scale = (MLA_NOPE + MLA_ROPE) ** -0.5
    x = x_ref[...]
    xb = x.astype(BF16)
    proj = _dot(xb, win_ref[...])
    cq = proj[:, :Q_LORA]
    cq = cq * lax.rsqrt(jnp.mean(cq * cq, axis=-1, keepdims=True) + RMS_EPS) * gq_ref[...]
    cq = cq.astype(BF16)
    q_lin = _dot(cq, wuq_ref[...])
    q_swp = _dot(cq, wuqs_ref[...])
    ctab = c_ref[...]
    stab = s_ref[...]
    for h in range(MLA_HEADS):
        sl = slice(h * HEAD_PAD, (h + 1) * HEAD_PAD)
        q_sc[:, sl] = (q_lin[:, sl] * ctab + q_swp[:, sl] * stab).astype(BF16)

    row_chunk = lax.broadcasted_iota(jnp.int32, (tq, TK_B), 0) // 64
    col_chunk = lax.broadcasted_iota(jnp.int32, (tq, TK_B), 1) // 64
    allowed = col_chunk <= row_chunk
    low_half = lax.broadcasted_iota(jnp.int32, (tq, LANES), 1) < MLA_V

    def attend(h, kv_blk, carry, masked):
        m, l, acc = carry
        k0 = pl.multiple_of(kv_blk * TK_B, TK_B)
        kb = k_ref[pl.ds(k0, TK_B), h * HEAD_PAD:(h + 1) * HEAD_PAD]
        vb = v_ref[pl.ds(k0, TK_B), (h // 2) * LANES:(h // 2 + 1) * LANES]
        s = _dot_nt(q_sc[:, h * HEAD_PAD:(h + 1) * HEAD_PAD], kb) * scale
        if masked:
            s = jnp.where(allowed, s, -jnp.inf)
        m_new = jnp.maximum(m, jnp.max(s, axis=-1, keepdims=True))
        corr = jnp.exp(m - m_new)
        p = jnp.exp(s - m_new)
        l = corr * l + jnp.sum(p, axis=-1, keepdims=True)
        acc = corr * acc + _dot(p.astype(BF16), vb)
        return m_new, l, acc

    outs = []
    for h in range(MLA_HEADS):
        init = (jnp.full((tq, 1), -jnp.inf, F32), jnp.zeros((tq, 1), F32),
                jnp.zeros((tq, LANES), F32))
        carry = lax.fori_loop(0, qi, lambda j, c, h=h: attend(h, j, c, False), init)
        _, l, acc = attend(h, qi, carry, True)
        outs.append(acc / l)
        if h % 2 == 1:
            pair = jnp.where(low_half, outs[h - 1], outs[h])
            cat_ref[:, (h // 2) * LANES:(h // 2 + 1) * LANES] = pair.astype(BF16)

    _memory_attention(proj[:, Q_LORA:], mkv_ref, cat_ref, MLA_HEADS * MLA_V)
    mix = _dot(cat_ref[...], wout_ref[...])
    o_ref[...] = _layer_norm(ALPHA * x + mix, g_ref[...], b_ref[...])


def _mixer_b(x2d, win, gq, wuq, wuqs, ctab, stab, k_all, v_all, memkv, wout, g, b, batch, seq):
    nq = seq // TQ_B
    width = MLA_HEADS * MLA_V + MEM_HEADS * MEM_DIM
    tile = lambda bi, qi: (bi * nq + qi, 0)
    return pl.pallas_call(
        _mixer_b_kernel,
        out_shape=jax.ShapeDtypeStruct(x2d.shape, F32),
        grid=(batch, nq),
        in_specs=[pl.BlockSpec((TQ_B, D_MODEL), tile),
                  _const_spec(win.shape), _const_spec(gq.shape),
                  _const_spec(wuq.shape), _const_spec(wuqs.shape),
                  pl.BlockSpec((TQ_B, LANES), tile), pl.BlockSpec((TQ_B, LANES), tile),
                  pl.BlockSpec((seq, k_all.shape[1]), lambda bi, qi: (bi, 0)),
                  pl.BlockSpec((seq, v_all.shape[1]), lambda bi, qi: (bi, 0)),
                  pl.BlockSpec((N_MEM, 2 * MEM_HEADS * MEM_DIM), lambda bi, qi: (bi, 1)),
                  _const_spec(wout.shape), _const_spec(g.shape), _const_spec(b.shape)],
        out_specs=pl.BlockSpec((TQ_B, D_MODEL), tile),
        scratch_shapes=[pltpu.VMEM((TQ_B, MLA_HEADS * HEAD_PAD), BF16),
                        pltpu.VMEM((TQ_B, width), BF16)],
        compiler_params=pltpu.CompilerParams(dimension_semantics=("parallel", "arbitrary"),
                                             vmem_limit_bytes=VMEM_LIMIT),
        name="mixer_b",
    )(x2d, win, gq, wuq, wuqs, ctab, stab, k_all, v_all, memkv, wout, g, b)


def _pad_heads(w, heads, dim):
    r = w.shape[0]
    w = w.reshape(r, heads, dim)
    w = jnp.pad(w, ((0, 0), (0, 0), (0, HEAD_PAD - dim)))
    return w.reshape(r, heads * HEAD_PAD)


def _swap_rope_halves(w, heads):
    r = w.shape[0]
    w = w.reshape(r, heads, MLA_NOPE + MLA_ROPE)
    x1 = w[..., MLA_NOPE:MLA_NOPE + ROPE_HALF]
    x2 = w[..., MLA_NOPE + ROPE_HALF:]
    return jnp.concatenate([jnp.zeros_like(w[..., :MLA_NOPE]), x2, x1], axis=-1).reshape(r, -1)


def kernel(x, mem, positions, a_w_in, a_b_igate, a_b_fgate, a_w_mem_kv, a_w_out, kv_w_down, kv_norm_g, kv_w_uk, kv_w_uv, b_w_in, b_q_norm_g, b_w_uq, b_w_mem_kv, b_w_out, ln1_g, ln1_b, ffn_w_up, ffn_w_down, ln2_g, ln2_b):
    batch, seq, _ = x.shape
    t = batch * seq
    x2d = x.reshape(t, D_MODEL)
    row = lambda v: v.reshape(1, -1).astype(F32)

    inv_freq = ROPE_THETA ** (-jnp.arange(0, MLA_ROPE, 2, dtype=F32) / MLA_ROPE)
    lane = jnp.arange(LANES)
    invf_row = jnp.where((lane >= ROPE_LO) & (lane < ROPE_LO + MLA_ROPE),
                         inv_freq[(lane - ROPE_LO) % ROPE_HALF], 0.0).reshape(1, LANES)
    ctab, stab = _rope_tables(positions.reshape(t, 1).astype(F32), invf_row)

    memkv = _mem_kv(mem.reshape(batch * N_MEM, D_MODEL),
                    jnp.concatenate([a_w_mem_kv[0], b_w_mem_kv[0]], axis=1).astype(BF16))

    hq = MLSTM_HEADS * MLSTM_QK
    hv = MLSTM_HEADS * MLSTM_V
    g0 = 2 * hq + 2 * hv
    w_in = a_w_in[0]
    wmain = jnp.concatenate([w_in[:, :g0], w_in[:, g0 + 2 * MLSTM_HEADS:]], axis=1).astype(BF16)
    w_gate = w_in[:, g0:g0 + 2 * MLSTM_HEADS]
    wgc = jnp.pad(w_gate, ((0, 0), (0, LANES - 2 * MLSTM_HEADS))).astype(BF16)
    wgr = w_gate.T.astype(BF16)
    b_gate = jnp.concatenate([a_b_igate[0], a_b_fgate[0]]).astype(F32)
    bgc = jnp.pad(b_gate, (0, LANES - 2 * MLSTM_HEADS)).reshape(1, LANES)
    bgr = b_gate.reshape(2 * MLSTM_HEADS, 1)
    x2d = _mixer_a(x2d, wmain, wgc, wgr, bgc, bgr, memkv, a_w_out[0].astype(BF16),
                   row(ln1_g[0]), row(ln1_b[0]), batch, seq)
    x2d = _ffn(x2d, ffn_w_up[0].astype(BF16), ffn_w_down[0].astype(BF16),
               row(ln2_g[0]), row(ln2_b[0]))

    wd = jnp.zeros((D_MODEL, KV_LORA + 2 * LANES), F32)
    wd = wd.at[:, :KV_LORA].set(kv_w_down[:, :KV_LORA])
    r0 = KV_LORA + ROPE_LO
    wd = wd.at[:, r0:r0 + MLA_ROPE].set(kv_w_down[:, KV_LORA:])
    r1 = KV_LORA + LANES + ROPE_LO
    wd = wd.at[:, r1:r1 + ROPE_HALF].set(kv_w_down[:, KV_LORA + ROPE_HALF:])
    wd = wd.at[:, r1 + ROPE_HALF:r1 + MLA_ROPE].set(kv_w_down[:, KV_LORA:KV_LORA + ROPE_HALF])
    k_all, v_all = _shared_kv(x2d, wd.astype(BF16), row(kv_norm_g),
                              _pad_heads(kv_w_uk, MLA_HEADS, MLA_NOPE).astype(BF16),
                              kv_w_uv.astype(BF16), ctab, stab)

    wuq = _pad_heads(b_w_uq[0], MLA_HEADS, MLA_NOPE + MLA_ROPE).astype(BF16)
    wuqs = _pad_heads(_swap_rope_halves(b_w_uq[0], MLA_HEADS), MLA_HEADS,
                      MLA_NOPE + MLA_ROPE).astype(BF16)
    x2d = _mixer_b(x2d, b_w_in[0].astype(BF16), row(b_q_norm_g[0]), wuq, wuqs, ctab, stab,
                   k_all, v_all, memkv, b_w_out[0].astype(BF16),
                   row(ln1_g[1]), row(ln1_b[1]), batch, seq)
    x2d = _ffn(x2d, ffn_w_up[1].astype(BF16), ffn_w_down[1].astype(BF16),
               row(ln2_g[1]), row(ln2_b[1]))
    return x2d.reshape(batch, seq, D_MODEL)
```

```python
import functools

import jax
import jax.numpy as jnp
from jax import lax
from jax.experimental import pallas as pl
from jax.experimental.pallas import tpu as pltpu

F32 = jnp.float32
BF16 = jnp.bfloat16

D_MODEL = 1024
DEPTH = 2
N_MEM = 256
MLSTM_HEADS = 4
MLSTM_QK = 64
MLSTM_V = 128
MEM_HEADS = 4
MEM_DIM = 128
MLA_HEADS = 8
MLA_NOPE = 64
MLA_ROPE = 32
MLA_V = 64
Q_LORA = 256
KV_LORA = 256
D_FF = 4 * D_MODEL
ROPE_THETA = 10000.0
LN_EPS = 1e-5
RMS_EPS = 1e-6
ALPHA = (2 * DEPTH) ** 0.25
MLA_CHUNK = 64
BF16_SUBLANES = 16
VT_ROWS = MLA_V + BF16_SUBLANES
QK_AHEAD = 8
LOG2_E = 1.4426950408889634

LANES = 128
HEAD_PAD = 128
ROPE_LO = MLA_NOPE
ROPE_HALF = MLA_ROPE // 2

TS_A = 256
TQ_B = 256
TK_B = 256
TM_FFN = 512
TM_KV = 512
TM_ROPE = 2048
TM_MEM = 512
FF_CHUNK = 1024
VMEM_LIMIT = 56 * 1024 * 1024

NT_DIMS = (((1,), (1,)), ((), ()))
TN_DIMS = (((0,), (0,)), ((), ()))


def _dot(a, b):
    return jnp.dot(a, b, preferred_element_type=F32)


def _dot_nt(a, b):
    return lax.dot_general(a, b, NT_DIMS, preferred_element_type=F32)


def _layer_norm(y, g, b):
    mu = jnp.mean(y, axis=-1, keepdims=True)
    yc = y - mu
    var = jnp.mean(yc * yc, axis=-1, keepdims=True)
    return yc * lax.rsqrt(var + LN_EPS) * g + b


def _log_sigmoid(z):
    return jnp.minimum(z, 0.0) - jnp.log(1.0 + jnp.exp(-jnp.abs(z)))


def _const_spec(shape):
    nd = len(shape)
    return pl.BlockSpec(shape, lambda *_: (0,) * nd, pipeline_mode=pl.Buffered(1))


def _rope_table_kernel(pos_ref, invf_ref, c_ref, s_ref):
    ang = invf_ref[...] * pos_ref[...]
    cos = jnp.cos(ang)
    sin = jnp.sin(ang)
    tm = ang.shape[1]
    tail = LANES - ROPE_LO - MLA_ROPE
    ct = jnp.concatenate([jnp.ones((ROPE_LO, tm), F32), cos, cos, jnp.ones((tail, tm), F32)], axis=0)
    st = jnp.concatenate([jnp.zeros((ROPE_LO, tm), F32), -sin, sin, jnp.zeros((tail, tm), F32)], axis=0)
    c_ref[...] = ct.T
    s_ref[...] = st.T


def _rope_tables(pos_row, invf_col):
    t = pos_row.shape[1]
    return pl.pallas_call(
        _rope_table_kernel,
        out_shape=(jax.ShapeDtypeStruct((t, LANES), F32),) * 2,
        grid=(t // TM_ROPE,),
        in_specs=[pl.BlockSpec((1, TM_ROPE), lambda i: (0, i)),
                  _const_spec((ROPE_HALF, 1))],
        out_specs=(pl.BlockSpec((TM_ROPE, LANES), lambda i: (i, 0)),) * 2,
        compiler_params=pltpu.CompilerParams(dimension_semantics=("parallel",)),
        name="rope_tables",
    )(pos_row, invf_col)


def _mem_kv_kernel(mem_ref, w_ref, o_ref):
    o_ref[...] = _dot(mem_ref[...].astype(BF16), w_ref[...]).astype(BF16)


def _mem_kv(mem2d, w):
    r, n = mem2d.shape[0], w.shape[1]
    return pl.pallas_call(
        _mem_kv_kernel,
        out_shape=jax.ShapeDtypeStruct((r, n), BF16),
        grid=(r // TM_MEM,),
        in_specs=[pl.BlockSpec((TM_MEM, D_MODEL), lambda i: (i, 0)),
                  _const_spec(w.shape)],
        out_specs=pl.BlockSpec((TM_MEM, n), lambda i: (i, 0)),
        compiler_params=pltpu.CompilerParams(dimension_semantics=("parallel",),
                                             vmem_limit_bytes=VMEM_LIMIT),
        name="mem_kv",
    )(mem2d, w)


def _memory_scores(q_all, mkv_ref, h):
    q_scale = MEM_DIM ** -0.5 * LOG2_E
    lo = h * MEM_DIM
    qh = (q_all[:, lo:lo + MEM_DIM] * q_scale).astype(BF16)
    return _dot_nt(qh, mkv_ref[:, lo:lo + MEM_DIM])


def _memory_output(s, mkv_ref, cat_ref, col0, h):
    lo = h * MEM_DIM
    v0 = MEM_HEADS * MEM_DIM + lo
    p = jnp.exp2(s - jnp.max(s, axis=-1, keepdims=True))
    den = jnp.sum(p, axis=-1, keepdims=True)
    o = _dot(p.astype(BF16), mkv_ref[:, v0:v0 + MEM_DIM]) / den
    cat_ref[:, col0 + lo:col0 + lo + MEM_DIM] = o.astype(BF16)


def _mixer_a_kernel(x_ref, wmain_ref, wgc_ref, wgr_ref, bgc_ref, bgr_ref, mkv_ref,
                    wout_ref, g_ref, b_ref, o_ref, c_st, n_st, m_st, cat_ref):
    ts = x_ref.shape[0]
    hq = MLSTM_HEADS * MLSTM_QK
    hv = MLSTM_HEADS * MLSTM_V

    @pl.when(pl.program_id(1) == 0)
    def _():
        c_st[...] = jnp.zeros_like(c_st)
        n_st[...] = jnp.zeros_like(n_st)
        m_st[...] = jnp.zeros_like(m_st)

    x = x_ref[...]
    xb = x.astype(BF16)
    proj = _dot(xb, wmain_ref[...])
    gate_c = _dot(xb, wgc_ref[...]) + bgc_ref[...]
    gate_r = _dot_nt(wgr_ref[...], xb) + bgr_ref[...]
    row = lax.broadcasted_iota(jnp.int32, (ts, ts), 0)
    col = lax.broadcasted_iota(jnp.int32, (ts, ts), 1)
    causal = col <= row
    tri_lower = causal.astype(F32)
    tri_upper = (row <= col).astype(F32)
    b_c = jnp.dot(tri_lower, _log_sigmoid(gate_c), precision=lax.Precision.HIGHEST,
                  preferred_element_type=F32)
    b_r = jnp.dot(_log_sigmoid(gate_r), tri_upper, precision=lax.Precision.HIGHEST,
                  preferred_element_type=F32)

    for h in range(MLSTM_HEADS):
        q = proj[:, h * MLSTM_QK:(h + 1) * MLSTM_QK]
        k = proj[:, hq + h * MLSTM_QK:hq + (h + 1) * MLSTM_QK] * (MLSTM_QK ** -0.5)
        v = proj[:, 2 * hq + h * MLSTM_V:2 * hq + (h + 1) * MLSTM_V].astype(BF16)
        qb = q.astype(BF16)
        c_prev = c_st[h]
        o_pre = proj[:, 2 * hq + hv + h * MLSTM_V:2 * hq + hv + (h + 1) * MLSTM_V]
        a_col = gate_c[:, h:h + 1]
        a_row = gate_r[h:h + 1, :]
        bc = b_c[:, MLSTM_HEADS + h:MLSTM_HEADS + h + 1]
        br = b_r[MLSTM_HEADS + h:MLSTM_HEADS + h + 1, :]
        g_tot = bc[ts - 1:ts, :]
        n_prev = n_st[h]
        m_prev = m_st[h]

        d_log = jnp.where(causal, bc - br + a_row, -jnp.inf)
        m_inter = bc + m_prev
        m_t = jnp.maximum(m_inter, jnp.max(d_log, axis=-1, keepdims=True))
        p = _dot_nt(qb, k.astype(BF16)) * jnp.exp(d_log - m_t)
        inter = jnp.exp(m_inter - m_t)
        num = _dot(p.astype(BF16), v) + inter * _dot(qb, c_prev.astype(BF16))
        nq = (jnp.sum(p, axis=-1, keepdims=True)
              + inter * jnp.sum(q * n_prev, axis=-1, keepdims=True))
        hh = num / jnp.maximum(jnp.abs(nq), jnp.exp(-m_t))
        hh = hh * jax.nn.sigmoid(o_pre)
        cat_ref[:, h * MLSTM_V:(h + 1) * MLSTM_V] = hh.astype(BF16)

        lw_row = g_tot - br + a_row
        lw_col = g_tot - bc + a_col
        m_new = jnp.maximum(g_tot + m_prev, jnp.max(lw_row, axis=-1, keepdims=True))
        decay = jnp.exp(g_tot + m_prev - m_new)
        kw = k * jnp.exp(lw_col - m_new)
        c_st[h] = decay * c_prev + lax.dot_general(kw.astype(BF16), v, TN_DIMS,
                                                   preferred_element_type=F32)
        n_st[h] = decay * n_prev + jnp.sum(kw, axis=0, keepdims=True)
        m_st[h] = m_new

    q_mem = proj[:, 2 * hq + 2 * hv:]
    for h in range(MEM_HEADS):
        _memory_output(_memory_scores(q_mem, mkv_ref, h), mkv_ref, cat_ref, hv, h)
    mix = _dot(cat_ref[...], wout_ref[...])
    o_ref[...] = _layer_norm(ALPHA * x + mix, g_ref[...], b_ref[...])


def _mixer_a(x2d, wmain, wgc, wgr, bgc, bgr, memkv, wout, g, b, batch, seq):
    ns = seq // TS_A
    width = MLSTM_HEADS * MLSTM_V + MEM_HEADS * MEM_DIM
    return pl.pallas_call(
        _mixer_a_kernel,
        out_shape=jax.ShapeDtypeStruct(x2d.shape, F32),
        grid=(batch, ns),
        in_specs=[pl.BlockSpec((TS_A, D_MODEL), lambda bi, si: (bi * ns + si, 0)),
                  _const_spec(wmain.shape), _const_spec(wgc.shape), _const_spec(wgr.shape),
                  _const_spec(bgc.shape), _const_spec(bgr.shape),
                  pl.BlockSpec((N_MEM, 2 * MEM_HEADS * MEM_DIM), lambda bi, si: (bi, 0)),
                  _const_spec(wout.shape), _const_spec(g.shape), _const_spec(b.shape)],
        out_specs=pl.BlockSpec((TS_A, D_MODEL), lambda bi, si: (bi * ns + si, 0)),
        scratch_shapes=[pltpu.VMEM((MLSTM_HEADS, MLSTM_QK, MLSTM_V), F32),
                        pltpu.VMEM((MLSTM_HEADS, 1, MLSTM_QK), F32),
                        pltpu.VMEM((MLSTM_HEADS, 1, 1), F32),
                        pltpu.VMEM((TS_A, width), BF16)],
        compiler_params=pltpu.CompilerParams(dimension_semantics=("parallel", "arbitrary"),
                                             vmem_limit_bytes=VMEM_LIMIT),
        name="mixer_a",
    )(x2d, wmain, wgc, wgr, bgc, bgr, memkv, wout, g, b)


def _ffn_kernel(x_ref, wup_ref, wdn_ref, g_ref, b_ref, o_ref):
    x = x_ref[...]
    xb = x.astype(BF16)
    acc = jnp.zeros(x.shape, F32)
    for j in range(D_FF // FF_CHUNK):
        hid = _dot(xb, wup_ref[:, j * FF_CHUNK:(j + 1) * FF_CHUNK])
        hid = jnp.square(jnp.maximum(hid, 0.0)).astype(BF16)
        acc = acc + _dot(hid, wdn_ref[j * FF_CHUNK:(j + 1) * FF_CHUNK, :])
    o_ref[...] = _layer_norm(ALPHA * x + acc, g_ref[...], b_ref[...])


def _ffn(x2d, wup, wdn, g, b):
    t = x2d.shape[0]
    return pl.pallas_call(
        _ffn_kernel,
        out_shape=jax.ShapeDtypeStruct(x2d.shape, F32),
        grid=(t // TM_FFN,),
        in_specs=[pl.BlockSpec((TM_FFN, D_MODEL), lambda i: (i, 0)),
                  _const_spec(wup.shape), _const_spec(wdn.shape),
                  _const_spec(g.shape), _const_spec(b.shape)],
        out_specs=pl.BlockSpec((TM_FFN, D_MODEL), lambda i: (i, 0)),
        compiler_params=pltpu.CompilerParams(dimension_semantics=("parallel",),
                                             vmem_limit_bytes=VMEM_LIMIT),
        name="ffn",
    )(x2d, wup, wdn, g, b)


def _shared_kv_kernel(x_ref, wd_ref, gk_ref, wuk_ref, wuvt_ref, c_ref, s_ref, k_ref, vt_ref):
    xb = x_ref[...].astype(BF16)
    d = _dot(xb, wd_ref[...])
    ckv = d[:, :KV_LORA]
    ckv = ckv * lax.rsqrt(jnp.mean(ckv * ckv, axis=-1, keepdims=True) + RMS_EPS) * gk_ref[...]
    ckv = ckv.astype(BF16)
    k_rope = (d[:, KV_LORA:KV_LORA + LANES] * c_ref[...]
              + d[:, KV_LORA + LANES:KV_LORA + 2 * LANES] * s_ref[...])
    k_nope = _dot(ckv, wuk_ref[...])
    for h in range(MLA_HEADS):
        k_ref[:, h * HEAD_PAD:(h + 1) * HEAD_PAD] = (
            k_nope[:, h * HEAD_PAD:(h + 1) * HEAD_PAD] + k_rope).astype(BF16)
    vt = _dot_nt(wuvt_ref[...], ckv).astype(BF16)
    ones = jnp.ones((BF16_SUBLANES, vt.shape[1]), BF16)
    rows = []
    for h in range(MLA_HEADS):
        rows += [vt[h * MLA_V:(h + 1) * MLA_V], ones]
    vt = jnp.concatenate(rows, axis=0)
    for j in range(vt_ref.shape[0]):
        vt_ref[j] = vt[:, j * TK_B:(j + 1) * TK_B]


def _shared_kv(x2d, wd, gk, wuk, wuvt, ctab, stab):
    t = x2d.shape[0]
    kw, vw = MLA_HEADS * HEAD_PAD, MLA_HEADS * VT_ROWS
    per_step = TM_KV // TK_B
    return pl.pallas_call(
        _shared_kv_kernel,
        out_shape=(jax.ShapeDtypeStruct((t, kw), BF16),
                   jax.ShapeDtypeStruct((t // TK_B, vw, TK_B), BF16)),
        grid=(t // TM_KV,),
        in_specs=[pl.BlockSpec((TM_KV, D_MODEL), lambda i: (i, 0)),
                  _const_spec(wd.shape), _const_spec(gk.shape),
                  _const_spec(wuk.shape), _const_spec(wuvt.shape),
                  pl.BlockSpec((TM_KV, LANES), lambda i: (i, 0)),
                  pl.BlockSpec((TM_KV, LANES), lambda i: (i, 0))],
        out_specs=(pl.BlockSpec((TM_KV, kw), lambda i: (i, 0)),
                   pl.BlockSpec((per_step, vw, TK_B), lambda i: (i, 0, 0))),
        compiler_params=pltpu.CompilerParams(dimension_semantics=("parallel",),
                                             vmem_limit_bytes=VMEM_LIMIT),
        name="shared_kv",
    )(x2d, wd, gk, wuk, wuvt, ctab, stab)


def _mixer_b_kernel(x_ref, win_ref, gq_ref, wuq_ref, wuqs_ref, c_ref, s_ref, k_ref, vt_ref,
                    mkv_ref, wout_ref, g_ref, b_ref, o_ref, q_sc, m_sc, acc_sc, ot_sc, cat_ref):
    tq = x_ref.shape[0]
    qi = pl.program_id(1)
    q_scale = (MLA_NOPE + MLA_ROPE) ** -0.5 * LOG2_E
    x = x_ref[...]
    xb = x.astype(BF16)
    proj = _dot(xb, win_ref[...])
    cq = proj[:, :Q_LORA]
    cq = cq * lax.rsqrt(jnp.mean(cq * cq, axis=-1, keepdims=True) + RMS_EPS) * gq_ref[...]
    cq = cq.astype(BF16)
    q_lin = _dot(cq, wuq_ref[...])
    q_swp = _dot(cq, wuqs_ref[...])
    ctab = c_ref[...]
    stab = s_ref[...]
    for h in range(MLA_HEADS):
        sl = slice(h * HEAD_PAD, (h + 1) * HEAD_PAD)
        q_sc[:, sl] = ((q_lin[:, sl] * ctab + q_swp[:, sl] * stab) * q_scale).astype(BF16)

    m_sc[...] = jnp.full(m_sc.shape, -jnp.inf, F32)
    acc_sc[...] = jnp.zeros_like(acc_sc)
    key_chunk = lax.broadcasted_iota(jnp.int32, (TK_B, tq), 0) // MLA_CHUNK
    qry_chunk = lax.broadcasted_iota(jnp.int32, (TK_B, tq), 1) // MLA_CHUNK
    allowed = key_chunk <= qry_chunk

    def scores(j, h):
        kb = k_ref[pl.ds(pl.multiple_of(j * TK_B, TK_B), TK_B), h * HEAD_PAD:(h + 1) * HEAD_PAD]
        return _dot_nt(kb, q_sc[:, h * HEAD_PAD:(h + 1) * HEAD_PAD])

    def key_tile(j, masked):
        s_tiles = [scores(j, h) for h in range(QK_AHEAD)]
        for h in range(MLA_HEADS):
            if h + QK_AHEAD < MLA_HEADS:
                s_tiles.append(scores(j, h + QK_AHEAD))
            s = s_tiles[h]
            if masked:
                s = jnp.where(allowed, s, -jnp.inf)
            m_old = m_sc[h]
            m_new = jnp.maximum(m_old, jnp.max(s, axis=0, keepdims=True))
            corr = jnp.exp2(m_old - m_new)
            p = jnp.exp2(s - m_new).astype(BF16)
            acc_sc[h] = corr * acc_sc[h] + _dot(vt_ref[j, h * VT_ROWS:(h + 1) * VT_ROWS, :], p)
            m_sc[h] = m_new

    def full_tile(j, carry):
        key_tile(j, False)
        return carry

    lax.fori_loop(0, qi, full_tile, 0)
    key_tile(qi, True)
    for h in range(MLA_HEADS):
        acc = acc_sc[h]
        ot_sc[h * MLA_V:(h + 1) * MLA_V, :] = acc[:MLA_V] / acc[MLA_V:MLA_V + 1]
    cat_ref[:, :MLA_HEADS * MLA_V] = ot_sc[...].T.astype(BF16)

    q_mem = proj[:, Q_LORA:]
    mem_scores = [_memory_scores(q_mem, mkv_ref, h) for h in range(MEM_HEADS)]
    for h in range(MEM_HEADS):
        _memory_output(mem_scores[h], mkv_ref, cat_ref, MLA_HEADS * MLA_V, h)
    mix = _dot(cat_ref[...], wout_ref[...])
    o_ref[...] = _layer_norm(ALPHA * x + mix, g_ref[...], b_ref[...])


def _mixer_b(x2d, win, gq, wuq, wuqs, ctab, stab, k_all, vt_all, memkv, wout, g, b, batch, seq):
    assert TQ_B == TK_B
    nq = seq // TQ_B
    width = MLA_HEADS * MLA_V + MEM_HEADS * MEM_DIM
    tile = lambda bi, qi: (bi * nq + qi, 0)
    return pl.pallas_call(
        _mixer_b_kernel,
        out_shape=jax.ShapeDtypeStruct(x2d.shape, F32),
        grid=(batch, nq),
        in_specs=[pl.BlockSpec((TQ_B, D_MODEL), tile),
                  _const_spec(win.shape), _const_spec(gq.shape),
                  _const_spec(wuq.shape), _const_spec(wuqs.shape),
                  pl.BlockSpec((TQ_B, LANES), tile), pl.BlockSpec((TQ_B, LANES), tile),
                  pl.BlockSpec((seq, k_all.shape[1]), lambda bi, qi: (bi, 0)),
                  pl.BlockSpec((nq,) + vt_all.shape[1:], lambda bi, qi: (bi, 0, 0)),
                  pl.BlockSpec((N_MEM, 2 * MEM_HEADS * MEM_DIM), lambda bi, qi: (bi, 1)),
                  _const_spec(wout.shape), _const_spec(g.shape), _const_spec(b.shape)],
        out_specs=pl.BlockSpec((TQ_B, D_MODEL), tile),
        scratch_shapes=[pltpu.VMEM((TQ_B, MLA_HEADS * HEAD_PAD), BF16),
                        pltpu.VMEM((MLA_HEADS, 1, TQ_B), F32),
                        pltpu.VMEM((MLA_HEADS, VT_ROWS, TQ_B), F32),
                        pltpu.VMEM((MLA_HEADS * MLA_V, TQ_B), F32),
                        pltpu.VMEM((TQ_B, width), BF16)],
        compiler_params=pltpu.CompilerParams(dimension_semantics=("parallel", "arbitrary"),
                                             vmem_limit_bytes=VMEM_LIMIT),
        name="mixer_b",
    )(x2d, win, gq, wuq, wuqs, ctab, stab, k_all, vt_all, memkv, wout, g, b)


def _pad_heads(w, heads, dim):
    r = w.shape[0]
    w = w.reshape(r, heads, dim)
    w = jnp.pad(w, ((0, 0), (0, 0), (0, HEAD_PAD - dim)))
    return w.reshape(r, heads * HEAD_PAD)


def _swap_rope_halves(w, heads):
    r = w.shape[0]
    w = w.reshape(r, heads, MLA_NOPE + MLA_ROPE)
    x1 = w[..., MLA_NOPE:MLA_NOPE + ROPE_HALF]
    x2 = w[..., MLA_NOPE + ROPE_HALF:]
    return jnp.concatenate([jnp.zeros_like(w[..., :MLA_NOPE]), x2, x1], axis=-1).reshape(r, -1)


def kernel(x, mem, positions, a_w_in, a_b_igate, a_b_fgate, a_w_mem_kv, a_w_out, kv_w_down, kv_norm_g, kv_w_uk, kv_w_uv, b_w_in, b_q_norm_g, b_w_uq, b_w_mem_kv, b_w_out, ln1_g, ln1_b, ffn_w_up, ffn_w_down, ln2_g, ln2_b):
    batch, seq, _ = x.shape
    t = batch * seq
    x2d = x.reshape(t, D_MODEL)
    row = lambda v: v.reshape(1, -1).astype(F32)

    inv_freq = ROPE_THETA ** (-jnp.arange(0, MLA_ROPE, 2, dtype=F32) / MLA_ROPE)
    ctab, stab = _rope_tables(positions.reshape(1, t).astype(F32), inv_freq.reshape(ROPE_HALF, 1))

    memkv = _mem_kv(mem.reshape(batch * N_MEM, D_MODEL),
                    jnp.concatenate([a_w_mem_kv[0], b_w_mem_kv[0]], axis=1).astype(BF16))

    hq = MLSTM_HEADS * MLSTM_QK
    hv = MLSTM_HEADS * MLSTM_V
    g0 = 2 * hq + 2 * hv
    w_in = a_w_in[0]
    wmain = jnp.concatenate([w_in[:, :g0], w_in[:, g0 + 2 * MLSTM_HEADS:]], axis=1).astype(BF16)
    w_gate = w_in[:, g0:g0 + 2 * MLSTM_HEADS]
    wgc = jnp.pad(w_gate, ((0, 0), (0, LANES - 2 * MLSTM_HEADS))).astype(BF16)
    wgr = w_gate.T.astype(BF16)
    b_gate = jnp.concatenate([a_b_igate[0], a_b_fgate[0]]).astype(F32)
    bgc = jnp.pad(b_gate, (0, LANES - 2 * MLSTM_HEADS)).reshape(1, LANES)
    bgr = b_gate.reshape(2 * MLSTM_HEADS, 1)
    x2d = _mixer_a(x2d, wmain, wgc, wgr, bgc, bgr, memkv, a_w_out[0].astype(BF16),
                   row(ln1_g[0]), row(ln1_b[0]), batch, seq)
    x2d = _ffn(x2d, ffn_w_up[0].astype(BF16), ffn_w_down[0].astype(BF16),
               row(ln2_g[0]), row(ln2_b[0]))

    wd = jnp.zeros((D_MODEL, KV_LORA + 2 * LANES), F32)
    wd = wd.at[:, :KV_LORA].set(kv_w_down[:, :KV_LORA])
    r0 = KV_LORA + ROPE_LO
    wd = wd.at[:, r0:r0 + MLA_ROPE].set(kv_w_down[:, KV_LORA:])
    r1 = KV_LORA + LANES + ROPE_LO
    wd = wd.at[:, r1:r1 + ROPE_HALF].set(kv_w_down[:, KV_LORA + ROPE_HALF:])
    wd = wd.at[:, r1 + ROPE_HALF:r1 + MLA_ROPE].set(kv_w_down[:, KV_LORA:KV_LORA + ROPE_HALF])
    k_all, vt_all = _shared_kv(x2d, wd.astype(BF16), row(kv_norm_g),
                               _pad_heads(kv_w_uk, MLA_HEADS, MLA_NOPE).astype(BF16),
                               kv_w_uv.T.astype(BF16), ctab, stab)

    wuq = _pad_heads(b_w_uq[0], MLA_HEADS, MLA_NOPE + MLA_ROPE).astype(BF16)
    wuqs = _pad_heads(_swap_rope_halves(b_w_uq[0], MLA_HEADS), MLA_HEADS,
                      MLA_NOPE + MLA_ROPE).astype(BF16)
    x2d = _mixer_b(x2d, b_w_in[0].astype(BF16), row(b_q_norm_g[0]), wuq, wuqs, ctab, stab,
                   k_all, vt_all, memkv, b_w_out[0].astype(BF16),
                   row(ln1_g[1]), row(ln1_b[1]), batch, seq)
    x2d = _ffn(x2d, ffn_w_up[1].astype(BF16), ffn_w_down[1].astype(BF16),
               row(ln2_g[1]), row(ln2_b[1]))
    return x2d.reshape(batch, seq, D_MODEL)
```

```python
import functools

import jax
import jax.numpy as jnp
from jax import lax
from jax.experimental import pallas as pl
from jax.experimental.pallas import tpu as pltpu

F32 = jnp.float32
BF16 = jnp.bfloat16

D_MODEL = 1024
DEPTH = 2
N_MEM = 256
MLSTM_HEADS = 4
MLSTM_QK = 64
MLSTM_V = 128
MEM_HEADS = 4
MEM_DIM = 128
MLA_HEADS = 8
MLA_NOPE = 64
MLA_ROPE = 32
MLA_V = 64
Q_LORA = 256
KV_LORA = 256
D_FF = 4 * D_MODEL
ROPE_THETA = 10000.0
LN_EPS = 1e-5
RMS_EPS = 1e-6
ALPHA = (2 * DEPTH) ** 0.25
MLA_CHUNK = 64
BF16_SUBLANES = 16
VT_ROWS = MLA_V + BF16_SUBLANES
QK_AHEAD = 2
LOG2_E = 1.4426950408889634

LANES = 128
HEAD_PAD = 128
ROPE_LO = MLA_NOPE
ROPE_HALF = MLA_ROPE // 2

TS_A = 256
TQ_B = 256
TK_B = 256
TM_FFN = 512
TM_KV = 512
TM_ROPE = 2048
TM_MEM = 512
FF_CHUNK = 1024
FFN_SUBTILES = 2
VMEM_LIMIT = 56 * 1024 * 1024

NT_DIMS = (((1,), (1,)), ((), ()))
TN_DIMS = (((0,), (0,)), ((), ()))


def _dot(a, b):
    return jnp.dot(a, b, preferred_element_type=F32)


def _dot_nt(a, b):
    return lax.dot_general(a, b, NT_DIMS, preferred_element_type=F32)


def _layer_norm(y, g, b):
    mu = jnp.mean(y, axis=-1, keepdims=True)
    yc = y - mu
    var = jnp.mean(yc * yc, axis=-1, keepdims=True)
    return yc * lax.rsqrt(var + LN_EPS) * g + b


def _log_sigmoid(z):
    return jnp.minimum(z, 0.0) - jnp.log(1.0 + jnp.exp(-jnp.abs(z)))


def _out_proj_norm(x_ref, cat_ref, wout_ref, g_ref, b_ref, o_ref):
    mix = _dot(cat_ref[...], wout_ref[...])
    o_ref[...] = _layer_norm(ALPHA * x_ref[...] + mix, g_ref[...], b_ref[...])


def _const_spec(shape):
    nd = len(shape)
    return pl.BlockSpec(shape, lambda *_: (0,) * nd, pipeline_mode=pl.Buffered(1))


def _rope_table_kernel(pos_ref, invf_ref, c_ref, s_ref):
    ang = invf_ref[...] * pos_ref[...]
    cos = jnp.cos(ang)
    sin = jnp.sin(ang)
    tm = ang.shape[1]
    tail = LANES - ROPE_LO - MLA_ROPE
    ct = jnp.concatenate([jnp.ones((ROPE_LO, tm), F32), cos, cos, jnp.ones((tail, tm), F32)], axis=0)
    st = jnp.concatenate([jnp.zeros((ROPE_LO, tm), F32), -sin, sin, jnp.zeros((tail, tm), F32)], axis=0)
    c_ref[...] = ct.T
    s_ref[...] = st.T


def _rope_tables(pos_row, invf_col):
    t = pos_row.shape[1]
    return pl.pallas_call(
        _rope_table_kernel,
        out_shape=(jax.ShapeDtypeStruct((t, LANES), F32),) * 2,
        grid=(t // TM_ROPE,),
        in_specs=[pl.BlockSpec((1, TM_ROPE), lambda i: (0, i)),
                  _const_spec((ROPE_HALF, 1))],
        out_specs=(pl.BlockSpec((TM_ROPE, LANES), lambda i: (i, 0)),) * 2,
        compiler_params=pltpu.CompilerParams(dimension_semantics=("parallel",)),
        name="rope_tables",
    )(pos_row, invf_col)


def _mem_kv_kernel(mem_ref, w_ref, o_ref):
    o_ref[...] = _dot(mem_ref[...].astype(BF16), w_ref[...]).astype(BF16)


def _mem_kv(mem2d, w):
    r, n = mem2d.shape[0], w.shape[1]
    return pl.pallas_call(
        _mem_kv_kernel,
        out_shape=jax.ShapeDtypeStruct((r, n), BF16),
        grid=(r // TM_MEM,),
        in_specs=[pl.BlockSpec((TM_MEM, D_MODEL), lambda i: (i, 0)),
                  _const_spec(w.shape)],
        out_specs=pl.BlockSpec((TM_MEM, n), lambda i: (i, 0)),
        compiler_params=pltpu.CompilerParams(dimension_semantics=("parallel",),
                                             vmem_limit_bytes=VMEM_LIMIT),
        name="mem_kv",
    )(mem2d, w)


def _memory_scores(q_all, mkv_ref, h):
    q_scale = MEM_DIM ** -0.5 * LOG2_E
    lo = h * MEM_DIM
    qh = (q_all[:, lo:lo + MEM_DIM] * q_scale).astype(BF16)
    return _dot_nt(qh, mkv_ref[:, lo:lo + MEM_DIM])


def _memory_output(s, mkv_ref, cat_ref, col0, h):
    lo = h * MEM_DIM
    v0 = MEM_HEADS * MEM_DIM + lo
    p = jnp.exp2(s - jnp.max(s, axis=-1, keepdims=True))
    den = jnp.sum(p, axis=-1, keepdims=True)
    o = _dot(p.astype(BF16), mkv_ref[:, v0:v0 + MEM_DIM]) / den
    cat_ref[:, col0 + lo:col0 + lo + MEM_DIM] = o.astype(BF16)


def _lane_scan(x, combine, fill):
    n = x.shape[1]
    lane = lax.broadcasted_iota(jnp.int32, x.shape, 1)
    d = 1
    while d < n:
        x = combine(x, jnp.where(lane >= d, pltpu.roll(x, d, axis=1), fill))
        d *= 2
    return x


def _mixer_a_kernel(x_ref, wmain_ref, wgr_ref, bgr_ref, mkv_ref, wout_ref, g_ref, b_ref, o_ref,
                    c_st, m_st, cat_ref):
    ts = x_ref.shape[0]
    hq = MLSTM_HEADS * MLSTM_QK
    hv = MLSTM_HEADS * MLSTM_V

    @pl.when(pl.program_id(1) == 0)
    def _():
        c_st[...] = jnp.zeros_like(c_st)
        m_st[...] = jnp.zeros_like(m_st)

    xb = x_ref[...].astype(BF16)
    gate_r = _dot_nt(wgr_ref[...], xb) + bgr_ref[...]
    head_row = lax.broadcasted_iota(jnp.int32, gate_r.shape, 0) < MLSTM_HEADS
    cum_f = _lane_scan(_log_sigmoid(gate_r), jnp.add, 0.0)
    a = jnp.where(head_row, gate_r, 0.0)
    b = jnp.where(head_row, pltpu.roll(cum_f, MLSTM_HEADS, axis=0), 0.0)
    r = a - b
    m_intra = b + _lane_scan(r, jnp.maximum, -jnp.inf)
    g_tot = b[:, ts - 1:ts]
    m_prev = m_st[...]
    m_inter = b + m_prev
    m_t = jnp.maximum(m_inter, m_intra)
    inter = jnp.exp(m_inter - m_t)
    m_new = jnp.maximum(g_tot + m_prev, jnp.max(g_tot + r, axis=1, keepdims=True))
    decay = jnp.exp(g_tot + m_prev - m_new)
    m_st[...] = m_new
    r2 = r * LOG2_E
    stack = jnp.concatenate([(b - m_t) * LOG2_E, inter, jnp.exp(-m_t), jnp.exp(g_tot + r - m_new),
                             jnp.zeros((LANES - 4 * 8, ts), F32)], axis=0)
    cols = stack.T
    proj = _dot(xb, wmain_ref[...])
    causal = (lax.broadcasted_iota(jnp.int32, (ts, ts), 1)
              <= lax.broadcasted_iota(jnp.int32, (ts, ts), 0))
    one_hot = (lax.broadcasted_iota(jnp.int32, (ts, LANES), 1) == 0).astype(BF16)

    for h in range(MLSTM_HEADS):
        q = proj[:, h * MLSTM_QK:(h + 1) * MLSTM_QK].astype(BF16)
        k = proj[:, hq + h * MLSTM_QK:hq + (h + 1) * MLSTM_QK] * (MLSTM_QK ** -0.5)
        v = proj[:, 2 * hq + h * MLSTM_V:2 * hq + (h + 1) * MLSTM_V].astype(BF16)
        v_ext = jnp.concatenate([v, one_hot], axis=1)
        o_pre = proj[:, 2 * hq + hv + h * MLSTM_V:2 * hq + hv + (h + 1) * MLSTM_V]
        c_prev = c_st[h]
        inter_c = cols[:, 8 + h:9 + h]

        expo = jnp.where(causal, cols[:, h:h + 1] + r2[h:h + 1, :], -jnp.inf)
        p = _dot_nt(q, k.astype(BF16)) * jnp.exp2(expo)
        num = _dot(p.astype(BF16), v_ext)
        qc = _dot(q, c_prev.astype(BF16))
        nq = num[:, MLSTM_V:MLSTM_V + 1] + inter_c * qc[:, MLSTM_V:MLSTM_V + 1]
        den = jnp.maximum(jnp.abs(nq), cols[:, 16 + h:17 + h])
        hh = (num[:, :MLSTM_V] + inter_c * qc[:, :MLSTM_V]) / den
        hh = hh * jax.nn.sigmoid(o_pre)
        cat_ref[:, h * MLSTM_V:(h + 1) * MLSTM_V] = hh.astype(BF16)

        kw = (k * cols[:, 24 + h:25 + h]).astype(BF16)
        c_st[h] = decay[h:h + 1, :] * c_prev + lax.dot_general(kw, v_ext, TN_DIMS,
                                                               preferred_element_type=F32)

    q_mem = proj[:, 2 * hq + 2 * hv:]
    for h in range(MEM_HEADS):
        _memory_output(_memory_scores(q_mem, mkv_ref, h), mkv_ref, cat_ref, hv, h)
    _out_proj_norm(x_ref, cat_ref, wout_ref, g_ref, b_ref, o_ref)


def _mixer_a(x2d, wmain, wgr, bgr, memkv, wout, g, b, batch, seq):
    ns = seq // TS_A
    width = MLSTM_HEADS * MLSTM_V + MEM_HEADS * MEM_DIM
    return pl.pallas_call(
        _mixer_a_kernel,
        out_shape=jax.ShapeDtypeStruct(x2d.shape, F32),
        grid=(batch, ns),
        in_specs=[pl.BlockSpec((TS_A, D_MODEL), lambda bi, si: (bi * ns + si, 0)),
                  _const_spec(wmain.shape), _const_spec(wgr.shape), _const_spec(bgr.shape),
                  pl.BlockSpec((N_MEM, 2 * MEM_HEADS * MEM_DIM), lambda bi, si: (bi, 0)),
                  _const_spec(wout.shape), _const_spec(g.shape), _const_spec(b.shape)],
        out_specs=pl.BlockSpec((TS_A, D_MODEL), lambda bi, si: (bi * ns + si, 0)),
        scratch_shapes=[pltpu.VMEM((MLSTM_HEADS, MLSTM_QK, 2 * MLSTM_V), F32),
                        pltpu.VMEM((2 * MLSTM_HEADS, 1), F32),
                        pltpu.VMEM((TS_A, width), BF16)],
        compiler_params=pltpu.CompilerParams(dimension_semantics=("parallel", "arbitrary"),
                                             vmem_limit_bytes=VMEM_LIMIT),
        name="mixer_a",
    )(x2d, wmain, wgr, bgr, memkv, wout, g, b)


def _ffn_kernel(x_ref, wup_ref, wdn_ref, g_ref, b_ref, o_ref):
    rows = x_ref.shape[0] // FFN_SUBTILES
    for r in range(FFN_SUBTILES):
        x = x_ref[r * rows:(r + 1) * rows, :]
        xb = x.astype(BF16)
        acc = jnp.zeros(x.shape, F32)
        for j in range(D_FF // FF_CHUNK):
            hid = _dot(xb, wup_ref[:, j * FF_CHUNK:(j + 1) * FF_CHUNK])
            hid = jnp.square(jnp.maximum(hid, 0.0)).astype(BF16)
            acc = acc + _dot(hid, wdn_ref[j * FF_CHUNK:(j + 1) * FF_CHUNK, :])
        o_ref[r * rows:(r + 1) * rows, :] = _layer_norm(ALPHA * x + acc, g_ref[...], b_ref[...])


def _ffn(x2d, wup, wdn, g, b):
    t = x2d.shape[0]
    return pl.pallas_call(
        _ffn_kernel,
        out_shape=jax.ShapeDtypeStruct(x2d.shape, F32),
        grid=(t // TM_FFN,),
        in_specs=[pl.BlockSpec((TM_FFN, D_MODEL), lambda i: (i, 0)),
                  _const_spec(wup.shape), _const_spec(wdn.shape),
                  _const_spec(g.shape), _const_spec(b.shape)],
        out_specs=pl.BlockSpec((TM_FFN, D_MODEL), lambda i: (i, 0)),
        compiler_params=pltpu.CompilerParams(dimension_semantics=("parallel",),
                                             vmem_limit_bytes=VMEM_LIMIT),
        name="ffn",
    )(x2d, wup, wdn, g, b)


def _shared_kv_kernel(x_ref, wd_ref, gk_ref, wuk_ref, wuvt_ref, c_ref, s_ref, k_ref, vt_ref):
    xb = x_ref[...].astype(BF16)
    d = _dot(xb, wd_ref[...])
    ckv = d[:, :KV_LORA]
    ckv = ckv * lax.rsqrt(jnp.mean(ckv * ckv, axis=-1, keepdims=True) + RMS_EPS) * gk_ref[...]
    ckv = ckv.astype(BF16)
    k_rope = (d[:, KV_LORA:KV_LORA + LANES] * c_ref[...]
              + d[:, KV_LORA + LANES:KV_LORA + 2 * LANES] * s_ref[...])
    k_nope = _dot(ckv, wuk_ref[...])
    for h in range(MLA_HEADS):
        k_ref[:, h * HEAD_PAD:(h + 1) * HEAD_PAD] = (
            k_nope[:, h * HEAD_PAD:(h + 1) * HEAD_PAD] + k_rope).astype(BF16)
    vt = _dot_nt(wuvt_ref[...], ckv).astype(BF16)
    ones = jnp.ones((BF16_SUBLANES, vt.shape[1]), BF16)
    rows = []
    for h in range(MLA_HEADS):
        rows += [vt[h * MLA_V:(h + 1) * MLA_V], ones]
    vt = jnp.concatenate(rows, axis=0)
    for j in range(vt_ref.shape[0]):
        vt_ref[j] = vt[:, j * TK_B:(j + 1) * TK_B]


def _shared_kv(x2d, wd, gk, wuk, wuvt, ctab, stab):
    t = x2d.shape[0]
    kw, vw = MLA_HEADS * HEAD_PAD, MLA_HEADS * VT_ROWS
    per_step = TM_KV // TK_B
    return pl.pallas_call(
        _shared_kv_kernel,
        out_shape=(jax.ShapeDtypeStruct((t, kw), BF16),
                   jax.ShapeDtypeStruct((t // TK_B, vw, TK_B), BF16)),
        grid=(t // TM_KV,),
        in_specs=[pl.BlockSpec((TM_KV, D_MODEL), lambda i: (i, 0)),
                  _const_spec(wd.shape), _const_spec(gk.shape),
                  _const_spec(wuk.shape), _const_spec(wuvt.shape),
                  pl.BlockSpec((TM_KV, LANES), lambda i: (i, 0)),
                  pl.BlockSpec((TM_KV, LANES), lambda i: (i, 0))],
        out_specs=(pl.BlockSpec((TM_KV, kw), lambda i: (i, 0)),
                   pl.BlockSpec((per_step, vw, TK_B), lambda i: (i, 0, 0))),
        compiler_params=pltpu.CompilerParams(dimension_semantics=("parallel",),
                                             vmem_limit_bytes=VMEM_LIMIT),
        name="shared_kv",
    )(x2d, wd, gk, wuk, wuvt, ctab, stab)


def _mixer_b_kernel(x_ref, win_ref, gq_ref, wuq_ref, wuqs_ref, c_ref, s_ref, k_ref, vt_ref,
                    mkv_ref, wout_ref, g_ref, b_ref, o_ref, q_sc, sa_sc, sb_sc, m_sc, acc_sc,
                    ot_sc, cat_ref):
    tq = x_ref.shape[0]
    qi = pl.program_id(1)
    q_scale = (MLA_NOPE + MLA_ROPE) ** -0.5 * LOG2_E
    x = x_ref[...]
    xb = x.astype(BF16)
    proj = _dot(xb, win_ref[...])
    cq = proj[:, :Q_LORA]
    cq = cq * lax.rsqrt(jnp.mean(cq * cq, axis=-1, keepdims=True) + RMS_EPS) * gq_ref[...]
    cq = cq.astype(BF16)
    q_lin = _dot(cq, wuq_ref[...])
    q_swp = _dot(cq, wuqs_ref[...])
    ctab = c_ref[...]
    stab = s_ref[...]
    for h in range(MLA_HEADS):
        sl = slice(h * HEAD_PAD, (h + 1) * HEAD_PAD)
        q_sc[:, sl] = ((q_lin[:, sl] * ctab + q_swp[:, sl] * stab) * q_scale).astype(BF16)

    m_sc[...] = jnp.full(m_sc.shape, -jnp.inf, F32)
    acc_sc[...] = jnp.zeros_like(acc_sc)
    key_chunk = lax.broadcasted_iota(jnp.int32, (TK_B, tq), 0) // MLA_CHUNK
    qry_chunk = lax.broadcasted_iota(jnp.int32, (TK_B, tq), 1) // MLA_CHUNK
    allowed = key_chunk <= qry_chunk

    def scores(j, h):
        kb = k_ref[pl.ds(pl.multiple_of(j * TK_B, TK_B), TK_B), h * HEAD_PAD:(h + 1) * HEAD_PAD]
        return _dot_nt(kb, q_sc[:, h * HEAD_PAD:(h + 1) * HEAD_PAD])

    def softmax_pv(j, h, src, masked):
        s = src[h]
        if masked:
            s = jnp.where(allowed, s, -jnp.inf)
        m_old = m_sc[h]
        m_new = jnp.maximum(m_old, jnp.max(s, axis=0, keepdims=True))
        corr = jnp.exp2(m_old - m_new)
        p = jnp.exp2(s - m_new).astype(BF16)
        acc_sc[h] = corr * acc_sc[h] + _dot(vt_ref[j, h * VT_ROWS:(h + 1) * VT_ROWS, :], p)
        m_sc[h] = m_new

    def key_tile(j, src, dst, masked):
        if dst is not None:
            for h in range(QK_AHEAD):
                dst[h] = scores(j + 1, h)
        for h in range(MLA_HEADS):
            softmax_pv(j, h, src, masked)
            if dst is not None and h + QK_AHEAD < MLA_HEADS:
                dst[h + QK_AHEAD] = scores(j + 1, h + QK_AHEAD)

    for h in range(MLA_HEADS):
        sa_sc[h] = scores(0, h)

    def tile_pair(k, carry):
        key_tile(2 * k, sa_sc, sb_sc, False)
        key_tile(2 * k + 1, sb_sc, sa_sc, False)
        return carry

    lax.fori_loop(0, qi // 2, tile_pair, 0)

    @pl.when(qi % 2 == 1)
    def _():
        key_tile(qi - 1, sa_sc, sb_sc, False)
        key_tile(qi, sb_sc, None, True)

    @pl.when(qi % 2 == 0)
    def _():
        key_tile(qi, sa_sc, None, True)
    for h in range(MLA_HEADS):
        acc = acc_sc[h]
        ot_sc[h * MLA_V:(h + 1) * MLA_V, :] = acc[:MLA_V] / acc[MLA_V:MLA_V + 1]
    cat_ref[:, :MLA_HEADS * MLA_V] = ot_sc[...].T.astype(BF16)

    q_mem = proj[:, Q_LORA:]
    mem_scores = [_memory_scores(q_mem, mkv_ref, h) for h in range(MEM_HEADS)]
    for h in range(MEM_HEADS):
        _memory_output(mem_scores[h], mkv_ref, cat_ref, MLA_HEADS * MLA_V, h)
    _out_proj_norm(x_ref, cat_ref, wout_ref, g_ref, b_ref, o_ref)


def _mixer_b(x2d, win, gq, wuq, wuqs, ctab, stab, k_all, vt_all, memkv, wout, g, b, batch, seq):
    assert TQ_B == TK_B
    nq = seq // TQ_B
    width = MLA_HEADS * MLA_V + MEM_HEADS * MEM_DIM
    tile = lambda bi, qi: (bi * nq + qi, 0)
    return pl.pallas_call(
        _mixer_b_kernel,
        out_shape=jax.ShapeDtypeStruct(x2d.shape, F32),
        grid=(batch, nq),
        in_specs=[pl.BlockSpec((TQ_B, D_MODEL), tile),
                  _const_spec(win.shape), _const_spec(gq.shape),
                  _const_spec(wuq.shape), _const_spec(wuqs.shape),
                  pl.BlockSpec((TQ_B, LANES), tile), pl.BlockSpec((TQ_B, LANES), tile),
                  pl.BlockSpec((seq, k_all.shape[1]), lambda bi, qi: (bi, 0)),
                  pl.BlockSpec((nq,) + vt_all.shape[1:], lambda bi, qi: (bi, 0, 0)),
                  pl.BlockSpec((N_MEM, 2 * MEM_HEADS * MEM_DIM), lambda bi, qi: (bi, 1)),
                  _const_spec(wout.shape), _const_spec(g.shape), _const_spec(b.shape)],
        out_specs=pl.BlockSpec((TQ_B, D_MODEL), tile),
        scratch_shapes=[pltpu.VMEM((TQ_B, MLA_HEADS * HEAD_PAD), BF16),
                        pltpu.VMEM((MLA_HEADS, TK_B, TQ_B), F32),
                        pltpu.VMEM((MLA_HEADS, TK_B, TQ_B), F32),
                        pltpu.VMEM((MLA_HEADS, 1, TQ_B), F32),
                        pltpu.VMEM((MLA_HEADS, VT_ROWS, TQ_B), F32),
                        pltpu.VMEM((MLA_HEADS * MLA_V, TQ_B), F32),
                        pltpu.VMEM((TQ_B, width), BF16)],
        compiler_params=pltpu.CompilerParams(dimension_semantics=("parallel", "arbitrary"),
                                             vmem_limit_bytes=VMEM_LIMIT),
        name="mixer_b",
    )(x2d, win, gq, wuq, wuqs, ctab, stab, k_all, vt_all, memkv, wout, g, b)


def _pad_heads(w, heads, dim):
    r = w.shape[0]
    w = w.reshape(r, heads, dim)
    w = jnp.pad(w, ((0, 0), (0, 0), (0, HEAD_PAD - dim)))
    return w.reshape(r, heads * HEAD_PAD)


def _swap_rope_halves(w, heads):
    r = w.shape[0]
    w = w.reshape(r, heads, MLA_NOPE + MLA_ROPE)
    x1 = w[..., MLA_NOPE:MLA_NOPE + ROPE_HALF]
    x2 = w[..., MLA_NOPE + ROPE_HALF:]
    return jnp.concatenate([jnp.zeros_like(w[..., :MLA_NOPE]), x2, x1], axis=-1).reshape(r, -1)


def kernel(x, mem, positions, a_w_in, a_b_igate, a_b_fgate, a_w_mem_kv, a_w_out, kv_w_down, kv_norm_g, kv_w_uk, kv_w_uv, b_w_in, b_q_norm_g, b_w_uq, b_w_mem_kv, b_w_out, ln1_g, ln1_b, ffn_w_up, ffn_w_down, ln2_g, ln2_b):
    batch, seq, _ = x.shape
    t = batch * seq
    x2d = x.reshape(t, D_MODEL)
    row = lambda v: v.reshape(1, -1).astype(F32)

    inv_freq = ROPE_THETA ** (-jnp.arange(0, MLA_ROPE, 2, dtype=F32) / MLA_ROPE)
    ctab, stab = _rope_tables(positions.reshape(1, t).astype(F32), inv_freq.reshape(ROPE_HALF, 1))

    memkv = _mem_kv(mem.reshape(batch * N_MEM, D_MODEL),
                    jnp.concatenate([a_w_mem_kv[0], b_w_mem_kv[0]], axis=1).astype(BF16))

    hq = MLSTM_HEADS * MLSTM_QK
    hv = MLSTM_HEADS * MLSTM_V
    g0 = 2 * hq + 2 * hv
    w_in = a_w_in[0]
    wmain = jnp.concatenate([w_in[:, :g0], w_in[:, g0 + 2 * MLSTM_HEADS:]], axis=1).astype(BF16)
    wgr = w_in[:, g0:g0 + 2 * MLSTM_HEADS].T.astype(BF16)
    bgr = jnp.concatenate([a_b_igate[0], a_b_fgate[0]]).astype(F32).reshape(2 * MLSTM_HEADS, 1)
    x2d = _mixer_a(x2d, wmain, wgr, bgr, memkv, a_w_out[0].astype(BF16),
                   row(ln1_g[0]), row(ln1_b[0]), batch, seq)
    x2d = _ffn(x2d, ffn_w_up[0].astype(BF16), ffn_w_down[0].astype(BF16),
               row(ln2_g[0]), row(ln2_b[0]))

    wd = jnp.zeros((D_MODEL, KV_LORA + 2 * LANES), F32)
    wd = wd.at[:, :KV_LORA].set(kv_w_down[:, :KV_LORA])
    r0 = KV_LORA + ROPE_LO
    wd = wd.at[:, r0:r0 + MLA_ROPE].set(kv_w_down[:, KV_LORA:])
    r1 = KV_LORA + LANES + ROPE_LO
    wd = wd.at[:, r1:r1 + ROPE_HALF].set(kv_w_down[:, KV_LORA + ROPE_HALF:])
    wd = wd.at[:, r1 + ROPE_HALF:r1 + MLA_ROPE].set(kv_w_down[:, KV_LORA:KV_LORA + ROPE_HALF])
    k_all, vt_all = _shared_kv(x2d, wd.astype(BF16), row(kv_norm_g),
                               _pad_heads(kv_w_uk, MLA_HEADS, MLA_NOPE).astype(BF16),
                               kv_w_uv.T.astype(BF16), ctab, stab)

    wuq = _pad_heads(b_w_uq[0], MLA_HEADS, MLA_NOPE + MLA_ROPE).astype(BF16)
    wuqs = _pad_heads(_swap_rope_halves(b_w_uq[0], MLA_HEADS), MLA_HEADS,
                      MLA_NOPE + MLA_ROPE).astype(BF16)
    x2d = _mixer_b(x2d, b_w_in[0].astype(BF16), row(b_q_norm_g[0]), wuq, wuqs, ctab, stab,
                   k_all, vt_all, memkv, b_w_out[0].astype(BF16),
                   row(ln1_g[1]), row(ln1_b[1]), batch, seq)
    x2d = _ffn(x2d, ffn_w_up[1].astype(BF16), ffn_w_down[1].astype(BF16),
               row(ln2_g[1]), row(ln2_b[1]))
    return x2d.reshape(batch, seq, D_MODEL)
```

```python
import functools

import jax
import jax.numpy as jnp
from jax import lax
from jax.experimental import pallas as pl
from jax.experimental.pallas import tpu as pltpu

F32 = jnp.float32
BF16 = jnp.bfloat16

D_MODEL = 1024
DEPTH = 2
N_MEM = 256
MLSTM_HEADS = 4
MLSTM_QK = 64
MLSTM_V = 128
MEM_HEADS = 4
MEM_DIM = 128
MLA_HEADS = 8
MLA_NOPE = 64
MLA_ROPE = 32
MLA_V = 64
Q_LORA = 256
KV_LORA = 256
D_FF = 4 * D_MODEL
ROPE_THETA = 10000.0
LN_EPS = 1e-5
RMS_EPS = 1e-6
ALPHA = (2 * DEPTH) ** 0.25
MLA_CHUNK = 64
BF16_SUBLANES = 16
VT_ROWS = MLA_V + BF16_SUBLANES
QK_AHEAD = 2
LOG2_E = 1.4426950408889634

LANES = 128
HEAD_PAD = 128
ROPE_LO = MLA_NOPE
ROPE_HALF = MLA_ROPE // 2

MLSTM_CHUNK = 256
MLSTM_CHUNKS = 2
TS_A = MLSTM_CHUNK * MLSTM_CHUNKS
PROJ_GROUP = 256
TQ_B = 256
TK_B = 256
TM_FFN = 512
TM_KV = 512
TM_ROPE = 2048
TM_MEM = 512
FF_CHUNK = 1024
FFN_SUBTILES = 2
VMEM_LIMIT = 56 * 1024 * 1024

NT_DIMS = (((1,), (1,)), ((), ()))
TN_DIMS = (((0,), (0,)), ((), ()))


def _dot(a, b):
    return jnp.dot(a, b, preferred_element_type=F32)


def _dot_nt(a, b):
    return lax.dot_general(a, b, NT_DIMS, preferred_element_type=F32)


def _layer_norm(y, g, b):
    mu = jnp.mean(y, axis=-1, keepdims=True)
    yc = y - mu
    var = jnp.mean(yc * yc, axis=-1, keepdims=True)
    return yc * lax.rsqrt(var + LN_EPS) * g + b


def _log_sigmoid(z):
    return jnp.minimum(z, 0.0) - jnp.log(1.0 + jnp.exp(-jnp.abs(z)))


def _out_proj_norm(x_ref, cat_ref, wout_ref, g_ref, b_ref, o_ref):
    mix = _dot(cat_ref[...], wout_ref[...])
    o_ref[...] = _layer_norm(ALPHA * x_ref[...] + mix, g_ref[...], b_ref[...])


def _const_spec(shape):
    nd = len(shape)
    return pl.BlockSpec(shape, lambda *_: (0,) * nd, pipeline_mode=pl.Buffered(1))


def _rope_table_kernel(pos_ref, invf_ref, c_ref, s_ref):
    ang = invf_ref[...] * pos_ref[...]
    cos = jnp.cos(ang)
    sin = jnp.sin(ang)
    tm = ang.shape[1]
    tail = LANES - ROPE_LO - MLA_ROPE
    ct = jnp.concatenate([jnp.ones((ROPE_LO, tm), F32), cos, cos, jnp.ones((tail, tm), F32)], axis=0)
    st = jnp.concatenate([jnp.zeros((ROPE_LO, tm), F32), -sin, sin, jnp.zeros((tail, tm), F32)], axis=0)
    c_ref[...] = ct.T
    s_ref[...] = st.T


def _rope_tables(pos_row, invf_col):
    t = pos_row.shape[1]
    return pl.pallas_call(
        _rope_table_kernel,
        out_shape=(jax.ShapeDtypeStruct((t, LANES), F32),) * 2,
        grid=(t // TM_ROPE,),
        in_specs=[pl.BlockSpec((1, TM_ROPE), lambda i: (0, i)),
                  _const_spec((ROPE_HALF, 1))],
        out_specs=(pl.BlockSpec((TM_ROPE, LANES), lambda i: (i, 0)),) * 2,
        compiler_params=pltpu.CompilerParams(dimension_semantics=("parallel",)),
        name="rope_tables",
    )(pos_row, invf_col)


def _mem_kv_kernel(mem_ref, w_ref, o_ref):
    o_ref[...] = _dot(mem_ref[...].astype(BF16), w_ref[...]).astype(BF16)


def _mem_kv(mem2d, w):
    r, n = mem2d.shape[0], w.shape[1]
    return pl.pallas_call(
        _mem_kv_kernel,
        out_shape=jax.ShapeDtypeStruct((r, n), BF16),
        grid=(r // TM_MEM,),
        in_specs=[pl.BlockSpec((TM_MEM, D_MODEL), lambda i: (i, 0)),
                  _const_spec(w.shape)],
        out_specs=pl.BlockSpec((TM_MEM, n), lambda i: (i, 0)),
        compiler_params=pltpu.CompilerParams(dimension_semantics=("parallel",),
                                             vmem_limit_bytes=VMEM_LIMIT),
        name="mem_kv",
    )(mem2d, w)


def _memory_scores(q_all, mkv_ref, h):
    q_scale = MEM_DIM ** -0.5 * LOG2_E
    lo = h * MEM_DIM
    qh = (q_all[:, lo:lo + MEM_DIM] * q_scale).astype(BF16)
    return _dot_nt(qh, mkv_ref[:, lo:lo + MEM_DIM])


def _memory_output(s, mkv_ref, cat_ref, col0, h):
    lo = h * MEM_DIM
    v0 = MEM_HEADS * MEM_DIM + lo
    p = jnp.exp2(s - jnp.max(s, axis=-1, keepdims=True)).astype(BF16)
    v_ext = jnp.concatenate([mkv_ref[:, v0:v0 + MEM_DIM], jnp.ones((N_MEM, LANES), BF16)], axis=1)
    o = _dot(p, v_ext)
    cat_ref[:, col0 + lo:col0 + lo + MEM_DIM] = (o[:, :MEM_DIM] / o[:, MEM_DIM:]).astype(BF16)


def _lane_scan(x, combine, fill):
    n = x.shape[1]
    lane = lax.broadcasted_iota(jnp.int32, x.shape, 1)
    d = 1
    while d < n:
        x = combine(x, jnp.where(lane >= d, pltpu.roll(x, d, axis=1), fill))
        d *= 2
    return x


def _mlstm_gates(gate_all, m_prev, n_chunks):
    length = gate_all.shape[1] // n_chunks
    rows = 2 * MLSTM_HEADS
    gate = jnp.concatenate([gate_all[:, c * length:(c + 1) * length] for c in range(n_chunks)],
                           axis=0)
    head_row = lax.broadcasted_iota(jnp.int32, gate.shape, 0) % rows < MLSTM_HEADS
    cum_f = _lane_scan(_log_sigmoid(gate), jnp.add, 0.0)
    a_all = jnp.where(head_row, gate, 0.0)
    b_all = jnp.where(head_row, pltpu.roll(cum_f, gate.shape[0] - MLSTM_HEADS, axis=0), 0.0)
    r_all = a_all - b_all
    mi_all = b_all + _lane_scan(r_all, jnp.maximum, -jnp.inf)
    out = []
    for c in range(n_chunks):
        b, r, m_intra = (t[c * rows:(c + 1) * rows] for t in (b_all, r_all, mi_all))
        g_tot = b[:, length - 1:length]
        m_inter = b + m_prev
        m_t = jnp.maximum(m_inter, m_intra)
        m_new = jnp.maximum(g_tot + m_prev, jnp.max(g_tot + r, axis=1, keepdims=True))
        decay = jnp.exp(g_tot + m_prev - m_new)
        stack = jnp.concatenate([(b - m_t) * LOG2_E, jnp.exp(m_inter - m_t), jnp.exp(-m_t),
                                 jnp.exp(g_tot + r - m_new),
                                 jnp.zeros((LANES - 4 * rows, length), F32)], axis=0)
        out.append((r * LOG2_E, stack.T, decay))
        m_prev = m_new
    return out, m_prev


def _mixer_a_kernel(x_ref, wmain_ref, wgr_ref, bgr_ref, mkv_ref, wout_ref, g_ref, b_ref, o_ref,
                    c_st, m_st, cat_ref):
    hq = MLSTM_HEADS * MLSTM_QK
    hv = MLSTM_HEADS * MLSTM_V
    cl = MLSTM_CHUNK
    n_groups = wmain_ref.shape[1] // PROJ_GROUP

    @pl.when(pl.program_id(1) == 0)
    def _():
        c_st[...] = jnp.zeros_like(c_st)
        m_st[...] = jnp.zeros_like(m_st)

    causal = (lax.broadcasted_iota(jnp.int32, (cl, cl), 1)
              <= lax.broadcasted_iota(jnp.int32, (cl, cl), 0))
    ones_blk = jnp.ones((cl, LANES), BF16)
    lane_half = lax.broadcasted_iota(jnp.int32, (cl, LANES), 1) // MLSTM_QK

    xbs = [x_ref[c * cl:(c + 1) * cl, :].astype(BF16) for c in range(MLSTM_CHUNKS)]
    gate_all = jnp.concatenate([_dot_nt(wgr_ref[...], xb) for xb in xbs], axis=1) + bgr_ref[...]
    gates, m_st[...] = _mlstm_gates(gate_all, m_st[...], MLSTM_CHUNKS)

    def project(c, grp):
        return _dot(xbs[c], wmain_ref[:, grp * PROJ_GROUP:(grp + 1) * PROJ_GROUP])

    def head(c, proj, h):
        r2, cols, decay = gates[c]
        rows = slice(c * cl, (c + 1) * cl)
        blk = slice((h // 2) * LANES, (h // 2 + 1) * LANES)
        mine = lane_half == h % 2
        q = jnp.where(mine, proj[:, blk], 0.0).astype(BF16)
        k = proj[:, hq:2 * hq][:, blk] * (MLSTM_QK ** -0.5)
        v = proj[:, 2 * hq + h * MLSTM_V:2 * hq + (h + 1) * MLSTM_V].astype(BF16)
        v_ext = jnp.concatenate([v, ones_blk], axis=1)
        o_pre = proj[:, 2 * hq + hv + h * MLSTM_V:2 * hq + hv + (h + 1) * MLSTM_V]
        c_prev = c_st[h]
        inter_b = jnp.broadcast_to(cols[:, 8 + h:9 + h], (cl, MLSTM_V))
        einv_b = jnp.broadcast_to(cols[:, 16 + h:17 + h], (cl, MLSTM_V))

        expo = jnp.where(causal, cols[:, h:h + 1] + r2[h:h + 1, :], -jnp.inf)
        p = _dot_nt(q, k.astype(BF16)) * jnp.exp2(expo)
        num = _dot(p.astype(BF16), v_ext)
        qc = _dot(q, c_prev.astype(BF16))
        nq = num[:, MLSTM_V:] + inter_b * qc[:, MLSTM_V:]
        hh = (num[:, :MLSTM_V] + inter_b * qc[:, :MLSTM_V]) / jnp.maximum(jnp.abs(nq), einv_b)
        hh = hh * jax.nn.sigmoid(o_pre)
        cat_ref[rows, h * MLSTM_V:(h + 1) * MLSTM_V] = hh.astype(BF16)

        kw = jnp.where(mine, k * cols[:, 24 + h:25 + h], 0.0).astype(BF16)
        c_st[h] = decay[h:h + 1, :] * c_prev + lax.dot_general(kw, v_ext, TN_DIMS,
                                                               preferred_element_type=F32)

    proj = jnp.concatenate([project(0, grp) for grp in range(n_groups)], axis=1)
    for c in range(MLSTM_CHUNKS):
        rows = slice(c * cl, (c + 1) * cl)
        nxt = {}
        for h in range(MLSTM_HEADS):
            head(c, proj, h)
            if c + 1 < MLSTM_CHUNKS:
                for grp in range(h, n_groups, MLSTM_HEADS):
                    nxt[grp] = project(c + 1, grp)
        q_mem = proj[:, 2 * hq + 2 * hv:]
        for h in range(MEM_HEADS):
            _memory_output(_memory_scores(q_mem, mkv_ref, h), mkv_ref, cat_ref.at[rows], hv, h)
        _out_proj_norm(x_ref.at[rows], cat_ref.at[rows], wout_ref, g_ref, b_ref, o_ref.at[rows])
        if nxt:
            proj = jnp.concatenate([nxt[grp] for grp in range(n_groups)], axis=1)


def _mixer_a(x2d, wmain, wgr, bgr, memkv, wout, g, b, batch, seq):
    ns = seq // TS_A
    width = MLSTM_HEADS * MLSTM_V + MEM_HEADS * MEM_DIM
    return pl.pallas_call(
        _mixer_a_kernel,
        out_shape=jax.ShapeDtypeStruct(x2d.shape, F32),
        grid=(batch, ns),
        in_specs=[pl.BlockSpec((TS_A, D_MODEL), lambda bi, si: (bi * ns + si, 0)),
                  _const_spec(wmain.shape), _const_spec(wgr.shape), _const_spec(bgr.shape),
                  pl.BlockSpec((N_MEM, 2 * MEM_HEADS * MEM_DIM), lambda bi, si: (bi, 0)),
                  _const_spec(wout.shape), _const_spec(g.shape), _const_spec(b.shape)],
        out_specs=pl.BlockSpec((TS_A, D_MODEL), lambda bi, si: (bi * ns + si, 0)),
        scratch_shapes=[pltpu.VMEM((MLSTM_HEADS, LANES, 2 * MLSTM_V), F32),
                        pltpu.VMEM((2 * MLSTM_HEADS, 1), F32),
                        pltpu.VMEM((TS_A, width), BF16)],
        compiler_params=pltpu.CompilerParams(dimension_semantics=("parallel", "arbitrary"),
                                             vmem_limit_bytes=VMEM_LIMIT),
        name="mixer_a",
    )(x2d, wmain, wgr, bgr, memkv, wout, g, b)


def _ffn_kernel(x_ref, wup_ref, wdn_ref, g_ref, b_ref, o_ref):
    rows = x_ref.shape[0] // FFN_SUBTILES
    for r in range(FFN_SUBTILES):
        x = x_ref[r * rows:(r + 1) * rows, :]
        xb = x.astype(BF16)
        acc = jnp.zeros(x.shape, F32)
        for j in range(D_FF // FF_CHUNK):
            hid = _dot(xb, wup_ref[:, j * FF_CHUNK:(j + 1) * FF_CHUNK])
            hid = jnp.square(jnp.maximum(hid, 0.0)).astype(BF16)
            acc = acc + _dot(hid, wdn_ref[j * FF_CHUNK:(j + 1) * FF_CHUNK, :])
        o_ref[r * rows:(r + 1) * rows, :] = _layer_norm(ALPHA * x + acc, g_ref[...], b_ref[...])


def _ffn(x2d, wup, wdn, g, b):
    t = x2d.shape[0]
    return pl.pallas_call(
        _ffn_kernel,
        out_shape=jax.ShapeDtypeStruct(x2d.shape, F32),
        grid=(t // TM_FFN,),
        in_specs=[pl.BlockSpec((TM_FFN, D_MODEL), lambda i: (i, 0)),
                  _const_spec(wup.shape), _const_spec(wdn.shape),
                  _const_spec(g.shape), _const_spec(b.shape)],
        out_specs=pl.BlockSpec((TM_FFN, D_MODEL), lambda i: (i, 0)),
        compiler_params=pltpu.CompilerParams(dimension_semantics=("parallel",),
                                             vmem_limit_bytes=VMEM_LIMIT),
        name="ffn",
    )(x2d, wup, wdn, g, b)


def _shared_kv_kernel(x_ref, wd_ref, gk_ref, wuk_ref, wuvt_ref, c_ref, s_ref, k_ref, vt_ref):
    xb = x_ref[...].astype(BF16)
    d = _dot(xb, wd_ref[...])
    ckv = d[:, :KV_LORA]
    ckv = ckv * lax.rsqrt(jnp.mean(ckv * ckv, axis=-1, keepdims=True) + RMS_EPS) * gk_ref[...]
    ckv = ckv.astype(BF16)
    k_rope = (d[:, KV_LORA:KV_LORA + LANES] * c_ref[...]
              + d[:, KV_LORA + LANES:KV_LORA + 2 * LANES] * s_ref[...])
    k_nope = _dot(ckv, wuk_ref[...])
    for h in range(MLA_HEADS):
        k_ref[:, h * HEAD_PAD:(h + 1) * HEAD_PAD] = (
            k_nope[:, h * HEAD_PAD:(h + 1) * HEAD_PAD] + k_rope).astype(BF16)
    vt = _dot_nt(wuvt_ref[...], ckv).astype(BF16)
    ones = jnp.ones((BF16_SUBLANES, vt.shape[1]), BF16)
    rows = []
    for h in range(MLA_HEADS):
        rows += [vt[h * MLA_V:(h + 1) * MLA_V], ones]
    vt = jnp.concatenate(rows, axis=0)
    for j in range(vt_ref.shape[0]):
        vt_ref[j] = vt[:, j * TK_B:(j + 1) * TK_B]


def _shared_kv(x2d, wd, gk, wuk, wuvt, ctab, stab):
    t = x2d.shape[0]
    kw, vw = MLA_HEADS * HEAD_PAD, MLA_HEADS * VT_ROWS
    per_step = TM_KV // TK_B
    return pl.pallas_call(
        _shared_kv_kernel,
        out_shape=(jax.ShapeDtypeStruct((t, kw), BF16),
                   jax.ShapeDtypeStruct((t // TK_B, vw, TK_B), BF16)),
        grid=(t // TM_KV,),
        in_specs=[pl.BlockSpec((TM_KV, D_MODEL), lambda i: (i, 0)),
                  _const_spec(wd.shape), _const_spec(gk.shape),
                  _const_spec(wuk.shape), _const_spec(wuvt.shape),
                  pl.BlockSpec((TM_KV, LANES), lambda i: (i, 0)),
                  pl.BlockSpec((TM_KV, LANES), lambda i: (i, 0))],
        out_specs=(pl.BlockSpec((TM_KV, kw), lambda i: (i, 0)),
                   pl.BlockSpec((per_step, vw, TK_B), lambda i: (i, 0, 0))),
        compiler_params=pltpu.CompilerParams(dimension_semantics=("parallel",),
                                             vmem_limit_bytes=VMEM_LIMIT),
        name="shared_kv",
    )(x2d, wd, gk, wuk, wuvt, ctab, stab)


def _mixer_b_kernel(x_ref, win_ref, gq_ref, wuq_ref, wuqs_ref, c_ref, s_ref, k_ref, vt_ref,
                    mkv_ref, wout_ref, g_ref, b_ref, o_ref, q_sc, sa_sc, sb_sc, m_sc, acc_sc,
                    ot_sc, cat_ref):
    tq = x_ref.shape[0]
    qi = pl.program_id(1)
    q_scale = (MLA_NOPE + MLA_ROPE) ** -0.5 * LOG2_E
    x = x_ref[...]
    xb = x.astype(BF16)
    proj = _dot(xb, win_ref[...])
    cq = proj[:, :Q_LORA]
    cq = cq * lax.rsqrt(jnp.mean(cq * cq, axis=-1, keepdims=True) + RMS_EPS) * gq_ref[...]
    cq = cq.astype(BF16)
    q_lin = _dot(cq, wuq_ref[...])
    q_swp = _dot(cq, wuqs_ref[...])
    ctab = c_ref[...]
    stab = s_ref[...]
    for h in range(MLA_HEADS):
        sl = slice(h * HEAD_PAD, (h + 1) * HEAD_PAD)
        q_sc[:, sl] = ((q_lin[:, sl] * ctab + q_swp[:, sl] * stab) * q_scale).astype(BF16)

    m_sc[...] = jnp.full(m_sc.shape, -jnp.inf, F32)
    acc_sc[...] = jnp.zeros_like(acc_sc)
    key_chunk = lax.broadcasted_iota(jnp.int32, (TK_B, tq), 0) // MLA_CHUNK
    qry_chunk = lax.broadcasted_iota(jnp.int32, (TK_B, tq), 1) // MLA_CHUNK
    allowed = key_chunk <= qry_chunk

    def scores(j, h):
        kb = k_ref[pl.ds(pl.multiple_of(j * TK_B, TK_B), TK_B), h * HEAD_PAD:(h + 1) * HEAD_PAD]
        return _dot_nt(kb, q_sc[:, h * HEAD_PAD:(h + 1) * HEAD_PAD])

    def softmax_pv(j, h, src, masked):
        s = src[h]
        if masked:
            s = jnp.where(allowed, s, -jnp.inf)
        m_old = m_sc[h]
        m_new = jnp.maximum(m_old, jnp.max(s, axis=0, keepdims=True))
        corr = jnp.exp2(m_old - m_new)
        p = jnp.exp2(s - m_new).astype(BF16)
        acc_sc[h] = corr * acc_sc[h] + _dot(vt_ref[j, h * VT_ROWS:(h + 1) * VT_ROWS, :], p)
        m_sc[h] = m_new

    def key_tile(j, src, dst, masked):
        if dst is not None:
            for h in range(QK_AHEAD):
                dst[h] = scores(j + 1, h)
        for h in range(MLA_HEADS):
            softmax_pv(j, h, src, masked)
            if dst is not None and h + QK_AHEAD < MLA_HEADS:
                dst[h + QK_AHEAD] = scores(j + 1, h + QK_AHEAD)

    for h in range(MLA_HEADS):
        sa_sc[h] = scores(0, h)

    def tile_pair(k, carry):
        key_tile(2 * k, sa_sc, sb_sc, False)
        key_tile(2 * k + 1, sb_sc, sa_sc, False)
        return carry

    lax.fori_loop(0, qi // 2, tile_pair, 0)

    @pl.when(qi % 2 == 1)
    def _():
        key_tile(qi - 1, sa_sc, sb_sc, False)
        key_tile(qi, sb_sc, None, True)

    @pl.when(qi % 2 == 0)
    def _():
        key_tile(qi, sa_sc, None, True)
    for h in range(MLA_HEADS):
        acc = acc_sc[h]
        ot_sc[h * MLA_V:(h + 1) * MLA_V, :] = acc[:MLA_V] / acc[MLA_V:MLA_V + 1]
    cat_ref[:, :MLA_HEADS * MLA_V] = ot_sc[...].T.astype(BF16)

    q_mem = proj[:, Q_LORA:]
    mem_scores = [_memory_scores(q_mem, mkv_ref, h) for h in range(MEM_HEADS)]
    for h in range(MEM_HEADS):
        _memory_output(mem_scores[h], mkv_ref, cat_ref, MLA_HEADS * MLA_V, h)
    _out_proj_norm(x_ref, cat_ref, wout_ref, g_ref, b_ref, o_ref)


def _mixer_b(x2d, win, gq, wuq, wuqs, ctab, stab, k_all, vt_all, memkv, wout, g, b, batch, seq):
    assert TQ_B == TK_B
    nq = seq // TQ_B
    width = MLA_HEADS * MLA_V + MEM_HEADS * MEM_DIM
    tile = lambda bi, qi: (bi * nq + qi, 0)
    return pl.pallas_call(
        _mixer_b_kernel,
        out_shape=jax.ShapeDtypeStruct(x2d.shape, F32),
        grid=(batch, nq),
        in_specs=[pl.BlockSpec((TQ_B, D_MODEL), tile),
                  _const_spec(win.shape), _const_spec(gq.shape),
                  _const_spec(wuq.shape), _const_spec(wuqs.shape),
                  pl.BlockSpec((TQ_B, LANES), tile), pl.BlockSpec((TQ_B, LANES), tile),
                  pl.BlockSpec((seq, k_all.shape[1]), lambda bi, qi: (bi, 0)),
                  pl.BlockSpec((nq,) + vt_all.shape[1:], lambda bi, qi: (bi, 0, 0)),
                  pl.BlockSpec((N_MEM, 2 * MEM_HEADS * MEM_DIM), lambda bi, qi: (bi, 1)),
                  _const_spec(wout.shape), _const_spec(g.shape), _const_spec(b.shape)],
        out_specs=pl.BlockSpec((TQ_B, D_MODEL), tile),
        scratch_shapes=[pltpu.VMEM((TQ_B, MLA_HEADS * HEAD_PAD), BF16),
                        pltpu.VMEM((MLA_HEADS, TK_B, TQ_B), F32),
                        pltpu.VMEM((MLA_HEADS, TK_B, TQ_B), F32),
                        pltpu.VMEM((MLA_HEADS, 1, TQ_B), F32),
                        pltpu.VMEM((MLA_HEADS, VT_ROWS, TQ_B), F32),
                        pltpu.VMEM((MLA_HEADS * MLA_V, TQ_B), F32),
                        pltpu.VMEM((TQ_B, width), BF16)],
        compiler_params=pltpu.CompilerParams(dimension_semantics=("parallel", "arbitrary"),
                                             vmem_limit_bytes=VMEM_LIMIT),
        name="mixer_b",
    )(x2d, win, gq, wuq, wuqs, ctab, stab, k_all, vt_all, memkv, wout, g, b)


def _pad_heads(w, heads, dim):
    r = w.shape[0]
    w = w.reshape(r, heads, dim)
    w = jnp.pad(w, ((0, 0), (0, 0), (0, HEAD_PAD - dim)))
    return w.reshape(r, heads * HEAD_PAD)


def _swap_rope_halves(w, heads):
    r = w.shape[0]
    w = w.reshape(r, heads, MLA_NOPE + MLA_ROPE)
    x1 = w[..., MLA_NOPE:MLA_NOPE + ROPE_HALF]
    x2 = w[..., MLA_NOPE + ROPE_HALF:]
    return jnp.concatenate([jnp.zeros_like(w[..., :MLA_NOPE]), x2, x1], axis=-1).reshape(r, -1)


def kernel(x, mem, positions, a_w_in, a_b_igate, a_b_fgate, a_w_mem_kv, a_w_out, kv_w_down, kv_norm_g, kv_w_uk, kv_w_uv, b_w_in, b_q_norm_g, b_w_uq, b_w_mem_kv, b_w_out, ln1_g, ln1_b, ffn_w_up, ffn_w_down, ln2_g, ln2_b):
    batch, seq, _ = x.shape
    t = batch * seq
    x2d = x.reshape(t, D_MODEL)
    row = lambda v: v.reshape(1, -1).astype(F32)

    inv_freq = ROPE_THETA ** (-jnp.arange(0, MLA_ROPE, 2, dtype=F32) / MLA_ROPE)
    ctab, stab = _rope_tables(positions.reshape(1, t).astype(F32), inv_freq.reshape(ROPE_HALF, 1))

    memkv = _mem_kv(mem.reshape(batch * N_MEM, D_MODEL),
                    jnp.concatenate([a_w_mem_kv[0], b_w_mem_kv[0]], axis=1).astype(BF16))

    hq = MLSTM_HEADS * MLSTM_QK
    hv = MLSTM_HEADS * MLSTM_V
    g0 = 2 * hq + 2 * hv
    w_in = a_w_in[0]
    wmain = jnp.concatenate([w_in[:, :g0], w_in[:, g0 + 2 * MLSTM_HEADS:]], axis=1).astype(BF16)
    wgr = w_in[:, g0:g0 + 2 * MLSTM_HEADS].T.astype(BF16)
    bgr = jnp.concatenate([a_b_igate[0], a_b_fgate[0]]).astype(F32).reshape(2 * MLSTM_HEADS, 1)
    x2d = _mixer_a(x2d, wmain, wgr, bgr, memkv, a_w_out[0].astype(BF16),
                   row(ln1_g[0]), row(ln1_b[0]), batch, seq)
    x2d = _ffn(x2d, ffn_w_up[0].astype(BF16), ffn_w_down[0].astype(BF16),
               row(ln2_g[0]), row(ln2_b[0]))

    wd = jnp.zeros((D_MODEL, KV_LORA + 2 * LANES), F32)
    wd = wd.at[:, :KV_LORA].set(kv_w_down[:, :KV_LORA])
    r0 = KV_LORA + ROPE_LO
    wd = wd.at[:, r0:r0 + MLA_ROPE].set(kv_w_down[:, KV_LORA:])
    r1 = KV_LORA + LANES + ROPE_LO
    wd = wd.at[:, r1:r1 + ROPE_HALF].set(kv_w_down[:, KV_LORA + ROPE_HALF:])
    wd = wd.at[:, r1 + ROPE_HALF:r1 + MLA_ROPE].set(kv_w_down[:, KV_LORA:KV_LORA + ROPE_HALF])
    k_all, vt_all = _shared_kv(x2d, wd.astype(BF16), row(kv_norm_g),
                               _pad_heads(kv_w_uk, MLA_HEADS, MLA_NOPE).astype(BF16),
                               kv_w_uv.T.astype(BF16), ctab, stab)

    wuq = _pad_heads(b_w_uq[0], MLA_HEADS, MLA_NOPE + MLA_ROPE).astype(BF16)
    wuqs = _pad_heads(_swap_rope_halves(b_w_uq[0], MLA_HEADS), MLA_HEADS,
                      MLA_NOPE + MLA_ROPE).astype(BF16)
    x2d = _mixer_b(x2d, b_w_in[0].astype(BF16), row(b_q_norm_g[0]), wuq, wuqs, ctab, stab,
                   k_all, vt_all, memkv, b_w_out[0].astype(BF16),
                   row(ln1_g[1]), row(ln1_b[1]), batch, seq)
    x2d = _ffn(x2d, ffn_w_up[1].astype(BF16), ffn_w_down[1].astype(BF16),
               row(ln2_g[1]), row(ln2_b[1]))
    return x2d.reshape(batch, seq, D_MODEL)
```

```python
import functools

import jax
import jax.numpy as jnp
from jax import lax
from jax.experimental import pallas as pl
from jax.experimental.pallas import tpu as pltpu

F32 = jnp.float32
BF16 = jnp.bfloat16

D_MODEL = 1024
DEPTH = 2
N_MEM = 256
MLSTM_HEADS = 4
MLSTM_QK = 64
MLSTM_V = 128
MEM_HEADS = 4
MEM_DIM = 128
MLA_HEADS = 8
MLA_NOPE = 64
MLA_ROPE = 32
MLA_V = 64
Q_LORA = 256
KV_LORA = 256
D_FF = 4 * D_MODEL
ROPE_THETA = 10000.0
LN_EPS = 1e-5
RMS_EPS = 1e-6
ALPHA = (2 * DEPTH) ** 0.25
MLA_CHUNK = 64
BF16_SUBLANES = 16
VT_ROWS = MLA_V + BF16_SUBLANES
QK_AHEAD = 2
LOG2_E = 1.4426950408889634

LANES = 128
HEAD_PAD = 128
ROPE_LO = MLA_NOPE
ROPE_HALF = MLA_ROPE // 2

MLSTM_CHUNK = 256
MLSTM_CHUNKS = 2
TS_A = MLSTM_CHUNK * MLSTM_CHUNKS
PROJ_GROUP = 256
TQ_B = 256
TK_B = 256
TM_FFN = 1024
TM_ROPE = 2048
TM_MEM = 512
FF_CHUNK = 1024
FFN_ROWS = TK_B
VMEM_LIMIT = 56 * 1024 * 1024

NT_DIMS = (((1,), (1,)), ((), ()))
TN_DIMS = (((0,), (0,)), ((), ()))


def _dot(a, b):
    return jnp.dot(a, b, preferred_element_type=F32)


def _dot_nt(a, b):
    return lax.dot_general(a, b, NT_DIMS, preferred_element_type=F32)


def _layer_norm(y, g, b):
    mu = jnp.mean(y, axis=-1, keepdims=True)
    yc = y - mu
    var = jnp.mean(yc * yc, axis=-1, keepdims=True)
    return yc * lax.rsqrt(var + LN_EPS) * g + b


def _log_sigmoid(z):
    return jnp.minimum(z, 0.0) - jnp.log(1.0 + jnp.exp(-jnp.abs(z)))


def _out_proj_norm(x_ref, cat_ref, wout_ref, g_ref, b_ref, o_ref):
    mix = _dot(cat_ref[...], wout_ref[...])
    o_ref[...] = _layer_norm(ALPHA * x_ref[...] + mix, g_ref[...], b_ref[...])


def _const_spec(shape):
    nd = len(shape)
    return pl.BlockSpec(shape, lambda *_: (0,) * nd, pipeline_mode=pl.Buffered(1))


def _rope_table_kernel(pos_ref, invf_ref, c_ref, s_ref):
    ang = invf_ref[...] * pos_ref[...]
    cos = jnp.cos(ang)
    sin = jnp.sin(ang)
    tm = ang.shape[1]
    tail = LANES - ROPE_LO - MLA_ROPE
    ct = jnp.concatenate([jnp.ones((ROPE_LO, tm), F32), cos, cos, jnp.ones((tail, tm), F32)], axis=0)
    st = jnp.concatenate([jnp.zeros((ROPE_LO, tm), F32), -sin, sin, jnp.zeros((tail, tm), F32)], axis=0)
    c_ref[...] = ct.T
    s_ref[...] = st.T


def _rope_tables(pos_row, invf_col):
    t = pos_row.shape[1]
    return pl.pallas_call(
        _rope_table_kernel,
        out_shape=(jax.ShapeDtypeStruct((t, LANES), F32),) * 2,
        grid=(t // TM_ROPE,),
        in_specs=[pl.BlockSpec((1, TM_ROPE), lambda i: (0, i)),
                  _const_spec((ROPE_HALF, 1))],
        out_specs=(pl.BlockSpec((TM_ROPE, LANES), lambda i: (i, 0)),) * 2,
        compiler_params=pltpu.CompilerParams(dimension_semantics=("parallel",)),
        name="rope_tables",
    )(pos_row, invf_col)


def _mem_kv_kernel(mem_ref, w_ref, o_ref):
    o_ref[...] = _dot(mem_ref[...].astype(BF16), w_ref[...]).astype(BF16)


def _mem_kv(mem2d, w):
    r, n = mem2d.shape[0], w.shape[1]
    return pl.pallas_call(
        _mem_kv_kernel,
        out_shape=jax.ShapeDtypeStruct((r, n), BF16),
        grid=(r // TM_MEM,),
        in_specs=[pl.BlockSpec((TM_MEM, D_MODEL), lambda i: (i, 0)),
                  _const_spec(w.shape)],
        out_specs=pl.BlockSpec((TM_MEM, n), lambda i: (i, 0)),
        compiler_params=pltpu.CompilerParams(dimension_semantics=("parallel",),
                                             vmem_limit_bytes=VMEM_LIMIT),
        name="mem_kv",
    )(mem2d, w)


def _memory_scores(q_all, mkv_ref, h):
    q_scale = MEM_DIM ** -0.5 * LOG2_E
    lo = h * MEM_DIM
    qh = (q_all[:, lo:lo + MEM_DIM] * q_scale).astype(BF16)
    return _dot_nt(qh, mkv_ref[:, lo:lo + MEM_DIM])


def _memory_output(s, mkv_ref, cat_ref, col0, h):
    lo = h * MEM_DIM
    v0 = MEM_HEADS * MEM_DIM + lo
    p = jnp.exp2(s - jnp.max(s, axis=-1, keepdims=True)).astype(BF16)
    v_ext = jnp.concatenate([mkv_ref[:, v0:v0 + MEM_DIM], jnp.ones((N_MEM, LANES), BF16)], axis=1)
    o = _dot(p, v_ext)
    cat_ref[:, col0 + lo:col0 + lo + MEM_DIM] = (o[:, :MEM_DIM] / o[:, MEM_DIM:]).astype(BF16)


def _lane_scan(x, combine, fill):
    n = x.shape[1]
    lane = lax.broadcasted_iota(jnp.int32, x.shape, 1)
    d = 1
    while d < n:
        x = combine(x, jnp.where(lane >= d, pltpu.roll(x, d, axis=1), fill))
        d *= 2
    return x


def _mlstm_gates(gate_all, m_prev, n_chunks):
    length = gate_all.shape[1] // n_chunks
    rows = 2 * MLSTM_HEADS
    gate = jnp.concatenate([gate_all[:, c * length:(c + 1) * length] for c in range(n_chunks)],
                           axis=0)
    head_row = lax.broadcasted_iota(jnp.int32, gate.shape, 0) % rows < MLSTM_HEADS
    cum_f = _lane_scan(_log_sigmoid(gate), jnp.add, 0.0)
    a_all = jnp.where(head_row, gate, 0.0)
    b_all = jnp.where(head_row, pltpu.roll(cum_f, gate.shape[0] - MLSTM_HEADS, axis=0), 0.0)
    r_all = a_all - b_all
    mi_all = b_all + _lane_scan(r_all, jnp.maximum, -jnp.inf)
    out = []
    for c in range(n_chunks):
        b, r, m_intra = (t[c * rows:(c + 1) * rows] for t in (b_all, r_all, mi_all))
        g_tot = b[:, length - 1:length]
        m_inter = b + m_prev
        m_t = jnp.maximum(m_inter, m_intra)
        m_new = jnp.maximum(g_tot + m_prev, jnp.max(g_tot + r, axis=1, keepdims=True))
        decay = jnp.exp(g_tot + m_prev - m_new)
        stack = jnp.concatenate([(b - m_t) * LOG2_E, jnp.exp(m_inter - m_t), jnp.exp(-m_t),
                                 jnp.exp(g_tot + r - m_new),
                                 jnp.zeros((LANES - 4 * rows, length), F32)], axis=0)
        out.append((r * LOG2_E, stack.T, decay))
        m_prev = m_new
    return out, m_prev


def _mixer_a_kernel(x_ref, wmain_ref, wgr_ref, bgr_ref, mkv_ref, wout_ref, g_ref, b_ref, o_ref,
                    c_st, m_st, cat_ref):
    hq = MLSTM_HEADS * MLSTM_QK
    hv = MLSTM_HEADS * MLSTM_V
    cl = MLSTM_CHUNK
    n_groups = wmain_ref.shape[1] // PROJ_GROUP

    @pl.when(pl.program_id(1) == 0)
    def _():
        c_st[...] = jnp.zeros_like(c_st)
        m_st[...] = jnp.zeros_like(m_st)

    causal = (lax.broadcasted_iota(jnp.int32, (cl, cl), 1)
              <= lax.broadcasted_iota(jnp.int32, (cl, cl), 0))
    ones_blk = jnp.ones((cl, LANES), BF16)
    lane_half = lax.broadcasted_iota(jnp.int32, (cl, LANES), 1) // MLSTM_QK

    xbs = [x_ref[c * cl:(c + 1) * cl, :].astype(BF16) for c in range(MLSTM_CHUNKS)]
    gate_all = jnp.concatenate([_dot_nt(wgr_ref[...], xb) for xb in xbs], axis=1) + bgr_ref[...]
    gates, m_st[...] = _mlstm_gates(gate_all, m_st[...], MLSTM_CHUNKS)

    def project(c, grp):
        return _dot(xbs[c], wmain_ref[:, grp * PROJ_GROUP:(grp + 1) * PROJ_GROUP])

    def head(c, proj, h):
        r2, cols, decay = gates[c]
        rows = slice(c * cl, (c + 1) * cl)
        blk = slice((h // 2) * LANES, (h // 2 + 1) * LANES)
        mine = lane_half == h % 2
        q = jnp.where(mine, proj[:, blk], 0.0).astype(BF16)
        k = proj[:, hq:2 * hq][:, blk] * (MLSTM_QK ** -0.5)
        v = proj[:, 2 * hq + h * MLSTM_V:2 * hq + (h + 1) * MLSTM_V].astype(BF16)
        v_ext = jnp.concatenate([v, ones_blk], axis=1)
        o_pre = proj[:, 2 * hq + hv + h * MLSTM_V:2 * hq + hv + (h + 1) * MLSTM_V]
        c_prev = c_st[h]
        inter_b = jnp.broadcast_to(cols[:, 8 + h:9 + h], (cl, MLSTM_V))
        einv_b = jnp.broadcast_to(cols[:, 16 + h:17 + h], (cl, MLSTM_V))

        expo = jnp.where(causal, cols[:, h:h + 1] + r2[h:h + 1, :], -jnp.inf)
        p = _dot_nt(q, k.astype(BF16)) * jnp.exp2(expo)
        num = _dot(p.astype(BF16), v_ext)
        qc = _dot(q, c_prev.astype(BF16))
        nq = num[:, MLSTM_V:] + inter_b * qc[:, MLSTM_V:]
        hh = (num[:, :MLSTM_V] + inter_b * qc[:, :MLSTM_V]) / jnp.maximum(jnp.abs(nq), einv_b)
        hh = hh * jax.nn.sigmoid(o_pre)
        cat_ref[rows, h * MLSTM_V:(h + 1) * MLSTM_V] = hh.astype(BF16)

        kw = jnp.where(mine, k * cols[:, 24 + h:25 + h], 0.0).astype(BF16)
        c_st[h] = decay[h:h + 1, :] * c_prev + lax.dot_general(kw, v_ext, TN_DIMS,
                                                               preferred_element_type=F32)

    proj = jnp.concatenate([project(0, grp) for grp in range(n_groups)], axis=1)
    for c in range(MLSTM_CHUNKS):
        rows = slice(c * cl, (c + 1) * cl)
        nxt = {}
        for h in range(MLSTM_HEADS):
            head(c, proj, h)
            if c + 1 < MLSTM_CHUNKS:
                for grp in range(h, n_groups, MLSTM_HEADS):
                    nxt[grp] = project(c + 1, grp)
        q_mem = proj[:, 2 * hq + 2 * hv:]
        for h in range(MEM_HEADS):
            _memory_output(_memory_scores(q_mem, mkv_ref, h), mkv_ref, cat_ref.at[rows], hv, h)
        _out_proj_norm(x_ref.at[rows], cat_ref.at[rows], wout_ref, g_ref, b_ref, o_ref.at[rows])
        if nxt:
            proj = jnp.concatenate([nxt[grp] for grp in range(n_groups)], axis=1)


def _mixer_a(x2d, wmain, wgr, bgr, memkv, wout, g, b, batch, seq):
    ns = seq // TS_A
    width = MLSTM_HEADS * MLSTM_V + MEM_HEADS * MEM_DIM
    return pl.pallas_call(
        _mixer_a_kernel,
        out_shape=jax.ShapeDtypeStruct(x2d.shape, F32),
        grid=(batch, ns),
        in_specs=[pl.BlockSpec((TS_A, D_MODEL), lambda bi, si: (bi * ns + si, 0)),
                  _const_spec(wmain.shape), _const_spec(wgr.shape), _const_spec(bgr.shape),
                  pl.BlockSpec((N_MEM, 2 * MEM_HEADS * MEM_DIM), lambda bi, si: (bi, 0)),
                  _const_spec(wout.shape), _const_spec(g.shape), _const_spec(b.shape)],
        out_specs=pl.BlockSpec((TS_A, D_MODEL), lambda bi, si: (bi * ns + si, 0)),
        scratch_shapes=[pltpu.VMEM((MLSTM_HEADS, LANES, 2 * MLSTM_V), F32),
                        pltpu.VMEM((2 * MLSTM_HEADS, 1), F32),
                        pltpu.VMEM((TS_A, width), BF16)],
        compiler_params=pltpu.CompilerParams(dimension_semantics=("parallel", "arbitrary"),
                                             vmem_limit_bytes=VMEM_LIMIT),
        name="mixer_a",
    )(x2d, wmain, wgr, bgr, memkv, wout, g, b)


def _shared_kv_rows(y, ctab, stab, wd_ref, gk_ref, wuk_ref, wuvt_ref):
    d = _dot(y.astype(BF16), wd_ref[...])
    ckv = d[:, :KV_LORA]
    ckv = ckv * lax.rsqrt(jnp.mean(ckv * ckv, axis=-1, keepdims=True) + RMS_EPS) * gk_ref[...]
    ckv = ckv.astype(BF16)
    k_rope = (d[:, KV_LORA:KV_LORA + LANES] * ctab
              + d[:, KV_LORA + LANES:KV_LORA + 2 * LANES] * stab)
    k_nope = _dot(ckv, wuk_ref[...])
    k = jnp.concatenate([(k_nope[:, h * HEAD_PAD:(h + 1) * HEAD_PAD] + k_rope).astype(BF16)
                         for h in range(MLA_HEADS)], axis=1)
    vt = _dot_nt(wuvt_ref[...], ckv).astype(BF16)
    ones = jnp.ones((BF16_SUBLANES, vt.shape[1]), BF16)
    pieces = []
    for h in range(MLA_HEADS):
        pieces += [vt[h * MLA_V:(h + 1) * MLA_V], ones]
    return k, jnp.concatenate(pieces, axis=0)


def _ffn_rows(x, wup_ref, wdn_ref, g_ref, b_ref):
    xb = x.astype(BF16)
    acc = jnp.zeros(x.shape, F32)
    for j in range(D_FF // FF_CHUNK):
        hid = _dot(xb, wup_ref[:, j * FF_CHUNK:(j + 1) * FF_CHUNK])
        hid = jnp.square(jnp.maximum(hid, 0.0)).astype(BF16)
        acc = acc + _dot(hid, wdn_ref[j * FF_CHUNK:(j + 1) * FF_CHUNK, :])
    return _layer_norm(ALPHA * x + acc, g_ref[...], b_ref[...])


def _ffn_kernel(x_ref, wup_ref, wdn_ref, g_ref, b_ref, o_ref):
    for r in range(x_ref.shape[0] // FFN_ROWS):
        rows = slice(r * FFN_ROWS, (r + 1) * FFN_ROWS)
        o_ref[rows, :] = _ffn_rows(x_ref[rows, :], wup_ref, wdn_ref, g_ref, b_ref)


def _ffn_kv_kernel(x_ref, wup_ref, wdn_ref, g_ref, b_ref, c_ref, s_ref, wd_ref, gk_ref, wuk_ref,
                   wuvt_ref, o_ref, k_ref, vt_ref):
    n_sub = x_ref.shape[0] // FFN_ROWS
    ys = []
    for r in range(n_sub + 1):
        if r < n_sub:
            rows = slice(r * FFN_ROWS, (r + 1) * FFN_ROWS)
            ys.append(_ffn_rows(x_ref[rows, :], wup_ref, wdn_ref, g_ref, b_ref))
            o_ref[rows, :] = ys[r]
        if r > 0:
            prev = slice((r - 1) * FFN_ROWS, r * FFN_ROWS)
            k_ref[prev, :], vt_ref[r - 1] = _shared_kv_rows(
                ys[r - 1], c_ref[prev, :], s_ref[prev, :], wd_ref, gk_ref, wuk_ref, wuvt_ref)


def _ffn(x2d, wup, wdn, g, b, kv_args=None):
    t = x2d.shape[0]
    tile = pl.BlockSpec((TM_FFN, D_MODEL), lambda i: (i, 0))
    in_specs = [tile, _const_spec(wup.shape), _const_spec(wdn.shape),
                _const_spec(g.shape), _const_spec(b.shape)]
    params = pltpu.CompilerParams(dimension_semantics=("parallel",), vmem_limit_bytes=VMEM_LIMIT)
    if kv_args is None:
        return pl.pallas_call(
            _ffn_kernel, out_shape=jax.ShapeDtypeStruct(x2d.shape, F32), grid=(t // TM_FFN,),
            in_specs=in_specs, out_specs=tile, compiler_params=params, name="ffn",
        )(x2d, wup, wdn, g, b)
    kw, vw = MLA_HEADS * HEAD_PAD, MLA_HEADS * VT_ROWS
    table = pl.BlockSpec((TM_FFN, LANES), lambda i: (i, 0))
    in_specs += [table, table] + [_const_spec(w.shape) for w in kv_args[2:]]
    return pl.pallas_call(
        _ffn_kv_kernel,
        out_shape=(jax.ShapeDtypeStruct(x2d.shape, F32), jax.ShapeDtypeStruct((t, kw), BF16),
                   jax.ShapeDtypeStruct((t // TK_B, vw, TK_B), BF16)),
        grid=(t // TM_FFN,), in_specs=in_specs,
        out_specs=(tile, pl.BlockSpec((TM_FFN, kw), lambda i: (i, 0)),
                   pl.BlockSpec((TM_FFN // TK_B, vw, TK_B), lambda i: (i, 0, 0))),
        compiler_params=params, name="ffn_kv",
    )(x2d, wup, wdn, g, b, *kv_args)


def _mixer_b_kernel(x_ref, win_ref, gq_ref, wuq_ref, wuqs_ref, c_ref, s_ref, k_ref, vt_ref,
                    mkv_ref, wout_ref, g_ref, b_ref, o_ref, q_sc, sa_sc, sb_sc, m_sc, acc_sc,
                    ot_sc, cat_ref):
    tq = x_ref.shape[0]
    qi = pl.program_id(1)
    q_scale = (MLA_NOPE + MLA_ROPE) ** -0.5 * LOG2_E
    x = x_ref[...]
    xb = x.astype(BF16)
    proj = _dot(xb, win_ref[...])
    cq = proj[:, :Q_LORA]
    cq = cq * lax.rsqrt(jnp.mean(cq * cq, axis=-1, keepdims=True) + RMS_EPS) * gq_ref[...]
    cq = cq.astype(BF16)
    q_lin = _dot(cq, wuq_ref[...])
    q_swp = _dot(cq, wuqs_ref[...])
    ctab = c_ref[...]
    stab = s_ref[...]
    for h in range(MLA_HEADS):
        sl = slice(h * HEAD_PAD, (h + 1) * HEAD_PAD)
        q_sc[:, sl] = ((q_lin[:, sl] * ctab + q_swp[:, sl] * stab) * q_scale).astype(BF16)

    m_sc[...] = jnp.full(m_sc.shape, -jnp.inf, F32)
    acc_sc[...] = jnp.zeros_like(acc_sc)
    key_chunk = lax.broadcasted_iota(jnp.int32, (TK_B, tq), 0) // MLA_CHUNK
    qry_chunk = lax.broadcasted_iota(jnp.int32, (TK_B, tq), 1) // MLA_CHUNK
    allowed = key_chunk <= qry_chunk

    def scores(j, h):
        kb = k_ref[pl.ds(pl.multiple_of(j * TK_B, TK_B), TK_B), h * HEAD_PAD:(h + 1) * HEAD_PAD]
        return _dot_nt(kb, q_sc[:, h * HEAD_PAD:(h + 1) * HEAD_PAD])

    def softmax_pv(j, h, src, masked):
        s = src[h]
        if masked:
            s = jnp.where(allowed, s, -jnp.inf)
        m_old = m_sc[h]
        m_new = jnp.maximum(m_old, jnp.max(s, axis=0, keepdims=True))
        corr = jnp.exp2(m_old - m_new)
        p = jnp.exp2(s - m_new).astype(BF16)
        acc_sc[h] = corr * acc_sc[h] + _dot(vt_ref[j, h * VT_ROWS:(h + 1) * VT_ROWS, :], p)
        m_sc[h] = m_new

    def key_tile(j, src, dst, masked):
        if dst is not None:
            for h in range(QK_AHEAD):
                dst[h] = scores(j + 1, h)
        for h in range(MLA_HEADS):
            softmax_pv(j, h, src, masked)
            if dst is not None and h + QK_AHEAD < MLA_HEADS:
                dst[h + QK_AHEAD] = scores(j + 1, h + QK_AHEAD)

    for h in range(MLA_HEADS):
        sa_sc[h] = scores(0, h)

    def tile_pair(k, carry):
        key_tile(2 * k, sa_sc, sb_sc, False)
        key_tile(2 * k + 1, sb_sc, sa_sc, False)
        return carry

    lax.fori_loop(0, qi // 2, tile_pair, 0)

    @pl.when(qi % 2 == 1)
    def _():
        key_tile(qi - 1, sa_sc, sb_sc, False)
        key_tile(qi, sb_sc, None, True)

    @pl.when(qi % 2 == 0)
    def _():
        key_tile(qi, sa_sc, None, True)
    for h in range(MLA_HEADS):
        acc = acc_sc[h]
        ot_sc[h * MLA_V:(h + 1) * MLA_V, :] = acc[:MLA_V] / acc[MLA_V:MLA_V + 1]
    cat_ref[:, :MLA_HEADS * MLA_V] = ot_sc[...].T.astype(BF16)

    q_mem = proj[:, Q_LORA:]
    mem_scores = [_memory_scores(q_mem, mkv_ref, h) for h in range(MEM_HEADS)]
    for h in range(MEM_HEADS):
        _memory_output(mem_scores[h], mkv_ref, cat_ref, MLA_HEADS * MLA_V, h)
    _out_proj_norm(x_ref, cat_ref, wout_ref, g_ref, b_ref, o_ref)


def _mixer_b(x2d, win, gq, wuq, wuqs, ctab, stab, k_all, vt_all, memkv, wout, g, b, batch, seq):
    assert TQ_B == TK_B
    nq = seq // TQ_B
    width = MLA_HEADS * MLA_V + MEM_HEADS * MEM_DIM
    tile = lambda bi, qi: (bi * nq + qi, 0)
    return pl.pallas_call(
        _mixer_b_kernel,
        out_shape=jax.ShapeDtypeStruct(x2d.shape, F32),
        grid=(batch, nq),
        in_specs=[pl.BlockSpec((TQ_B, D_MODEL), tile),
                  _const_spec(win.shape), _const_spec(gq.shape),
                  _const_spec(wuq.shape), _const_spec(wuqs.shape),
                  pl.BlockSpec((TQ_B, LANES), tile), pl.BlockSpec((TQ_B, LANES), tile),
                  pl.BlockSpec((seq, k_all.shape[1]), lambda bi, qi: (bi, 0)),
                  pl.BlockSpec((nq,) + vt_all.shape[1:], lambda bi, qi: (bi, 0, 0)),
                  pl.BlockSpec((N_MEM, 2 * MEM_HEADS * MEM_DIM), lambda bi, qi: (bi, 1)),
                  _const_spec(wout.shape), _const_spec(g.shape), _const_spec(b.shape)],
        out_specs=pl.BlockSpec((TQ_B, D_MODEL), tile),
        scratch_shapes=[pltpu.VMEM((TQ_B, MLA_HEADS * HEAD_PAD), BF16),
                        pltpu.VMEM((MLA_HEADS, TK_B, TQ_B), F32),
                        pltpu.VMEM((MLA_HEADS, TK_B, TQ_B), F32),
                        pltpu.VMEM((MLA_HEADS, 1, TQ_B), F32),
                        pltpu.VMEM((MLA_HEADS, VT_ROWS, TQ_B), F32),
                        pltpu.VMEM((MLA_HEADS * MLA_V, TQ_B), F32),
                        pltpu.VMEM((TQ_B, width), BF16)],
        compiler_params=pltpu.CompilerParams(dimension_semantics=("parallel", "arbitrary"),
                                             vmem_limit_bytes=VMEM_LIMIT),
        name="mixer_b",
    )(x2d, win, gq, wuq, wuqs, ctab, stab, k_all, vt_all, memkv, wout, g, b)


def _pad_heads(w, heads, dim):
    r = w.shape[0]
    w = w.reshape(r, heads, dim)
    w = jnp.pad(w, ((0, 0), (0, 0), (0, HEAD_PAD - dim)))
    return w.reshape(r, heads * HEAD_PAD)


def _swap_rope_halves(w, heads):
    r = w.shape[0]
    w = w.reshape(r, heads, MLA_NOPE + MLA_ROPE)
    x1 = w[..., MLA_NOPE:MLA_NOPE + ROPE_HALF]
    x2 = w[..., MLA_NOPE + ROPE_HALF:]
    return jnp.concatenate([jnp.zeros_like(w[..., :MLA_NOPE]), x2, x1], axis=-1).reshape(r, -1)


def kernel(x, mem, positions, a_w_in, a_b_igate, a_b_fgate, a_w_mem_kv, a_w_out, kv_w_down, kv_norm_g, kv_w_uk, kv_w_uv, b_w_in, b_q_norm_g, b_w_uq, b_w_mem_kv, b_w_out, ln1_g, ln1_b, ffn_w_up, ffn_w_down, ln2_g, ln2_b):
    batch, seq, _ = x.shape
    t = batch * seq
    x2d = x.reshape(t, D_MODEL)
    row = lambda v: v.reshape(1, -1).astype(F32)

    inv_freq = ROPE_THETA ** (-jnp.arange(0, MLA_ROPE, 2, dtype=F32) / MLA_ROPE)
    ctab, stab = _rope_tables(positions.reshape(1, t).astype(F32), inv_freq.reshape(ROPE_HALF, 1))

    memkv = _mem_kv(mem.reshape(batch * N_MEM, D_MODEL),
                    jnp.concatenate([a_w_mem_kv[0], b_w_mem_kv[0]], axis=1).astype(BF16))

    hq = MLSTM_HEADS * MLSTM_QK
    hv = MLSTM_HEADS * MLSTM_V
    g0 = 2 * hq + 2 * hv
    w_in = a_w_in[0]
    wmain = jnp.concatenate([w_in[:, :g0], w_in[:, g0 + 2 * MLSTM_HEADS:]], axis=1).astype(BF16)
    wgr = w_in[:, g0:g0 + 2 * MLSTM_HEADS].T.astype(BF16)
    bgr = jnp.concatenate([a_b_igate[0], a_b_fgate[0]]).astype(F32).reshape(2 * MLSTM_HEADS, 1)
    x2d = _mixer_a(x2d, wmain, wgr, bgr, memkv, a_w_out[0].astype(BF16),
                   row(ln1_g[0]), row(ln1_b[0]), batch, seq)

    wd = jnp.zeros((D_MODEL, KV_LORA + 2 * LANES), F32)
    wd = wd.at[:, :KV_LORA].set(kv_w_down[:, :KV_LORA])
    r0 = KV_LORA + ROPE_LO
    wd = wd.at[:, r0:r0 + MLA_ROPE].set(kv_w_down[:, KV_LORA:])
    r1 = KV_LORA + LANES + ROPE_LO
    wd = wd.at[:, r1:r1 + ROPE_HALF].set(kv_w_down[:, KV_LORA + ROPE_HALF:])
    wd = wd.at[:, r1 + ROPE_HALF:r1 + MLA_ROPE].set(kv_w_down[:, KV_LORA:KV_LORA + ROPE_HALF])
    kv_args = (ctab, stab, wd.astype(BF16), row(kv_norm_g),
               _pad_heads(kv_w_uk, MLA_HEADS, MLA_NOPE).astype(BF16), kv_w_uv.T.astype(BF16))
    x2d, k_all, vt_all = _ffn(x2d, ffn_w_up[0].astype(BF16), ffn_w_down[0].astype(BF16),
                              row(ln2_g[0]), row(ln2_b[0]), kv_args)

    wuq = _pad_heads(b_w_uq[0], MLA_HEADS, MLA_NOPE + MLA_ROPE).astype(BF16)
    wuqs = _pad_heads(_swap_rope_halves(b_w_uq[0], MLA_HEADS), MLA_HEADS,
                      MLA_NOPE + MLA_ROPE).astype(BF16)
    x2d = _mixer_b(x2d, b_w_in[0].astype(BF16), row(b_q_norm_g[0]), wuq, wuqs, ctab, stab,
                   k_all, vt_all, memkv, b_w_out[0].astype(BF16),
                   row(ln1_g[1]), row(ln1_b[1]), batch, seq)
    x2d = _ffn(x2d, ffn_w_up[1].astype(BF16), ffn_w_down[1].astype(BF16),
               row(ln2_g[1]), row(ln2_b[1]))
    return x2d.reshape(batch, seq, D_MODEL)
```

```python
import functools

import jax
import jax.numpy as jnp
from jax import lax
from jax.experimental import pallas as pl
from jax.experimental.pallas import tpu as pltpu

F32 = jnp.float32
BF16 = jnp.bfloat16

D_MODEL = 1024
DEPTH = 2
N_MEM = 256
MLSTM_HEADS = 4
MLSTM_QK = 64
MLSTM_V = 128
MEM_HEADS = 4
MEM_DIM = 128
MLA_HEADS = 8
MLA_NOPE = 64
MLA_ROPE = 32
MLA_V = 64
Q_LORA = 256
KV_LORA = 256
D_FF = 4 * D_MODEL
ROPE_THETA = 10000.0
LN_EPS = 1e-5
RMS_EPS = 1e-6
ALPHA = (2 * DEPTH) ** 0.25
MLA_CHUNK = 64
BF16_SUBLANES = 16
VT_ROWS = MLA_V + BF16_SUBLANES
QK_AHEAD = 2
LOG2_E = 1.4426950408889634

LANES = 128
HEAD_PAD = 128
ROPE_LO = MLA_NOPE
ROPE_HALF = MLA_ROPE // 2

MLSTM_CHUNK = 256
MLSTM_CHUNKS = 2
TS_A = MLSTM_CHUNK * MLSTM_CHUNKS
PROJ_GROUP = 256
TQ_B = 256
TK_B = 256
TM_FFN = 1024
TM_ROPE = 2048
TM_MEM = 512
FF_CHUNK = 1024
FFN_ROWS = TK_B
VMEM_LIMIT = 56 * 1024 * 1024

NT_DIMS = (((1,), (1,)), ((), ()))
TN_DIMS = (((0,), (0,)), ((), ()))


def _dot(a, b):
    return jnp.dot(a, b, preferred_element_type=F32)


def _dot_nt(a, b):
    return lax.dot_general(a, b, NT_DIMS, preferred_element_type=F32)


def _layer_norm(y, g, b):
    mu = jnp.mean(y, axis=-1, keepdims=True)
    yc = y - mu
    var = jnp.mean(yc * yc, axis=-1, keepdims=True)
    return yc * lax.rsqrt(var + LN_EPS) * g + b


def _log_sigmoid(z):
    return jnp.minimum(z, 0.0) - jnp.log(1.0 + jnp.exp(-jnp.abs(z)))


def _out_proj_norm(x_ref, cat_ref, wout_ref, g_ref, b_ref, o_ref):
    mix = _dot(cat_ref[...], wout_ref[...])
    o_ref[...] = _layer_norm(ALPHA * x_ref[...] + mix, g_ref[...], b_ref[...])


def _const_spec(shape):
    nd = len(shape)
    return pl.BlockSpec(shape, lambda *_: (0,) * nd, pipeline_mode=pl.Buffered(1))


def _rope_table_kernel(pos_ref, invf_ref, c_ref, s_ref):
    ang = invf_ref[...] * pos_ref[...]
    cos = jnp.cos(ang)
    sin = jnp.sin(ang)
    tm = ang.shape[1]
    tail = LANES - ROPE_LO - MLA_ROPE
    ct = jnp.concatenate([jnp.ones((ROPE_LO, tm), F32), cos, cos, jnp.ones((tail, tm), F32)], axis=0)
    st = jnp.concatenate([jnp.zeros((ROPE_LO, tm), F32), -sin, sin, jnp.zeros((tail, tm), F32)], axis=0)
    c_ref[...] = ct.T
    s_ref[...] = st.T


def _rope_tables(pos_row, invf_col):
    t = pos_row.shape[1]
    return pl.pallas_call(
        _rope_table_kernel,
        out_shape=(jax.ShapeDtypeStruct((t, LANES), F32),) * 2,
        grid=(t // TM_ROPE,),
        in_specs=[pl.BlockSpec((1, TM_ROPE), lambda i: (0, i)),
                  _const_spec((ROPE_HALF, 1))],
        out_specs=(pl.BlockSpec((TM_ROPE, LANES), lambda i: (i, 0)),) * 2,
        compiler_params=pltpu.CompilerParams(dimension_semantics=("parallel",)),
        name="rope_tables",
    )(pos_row, invf_col)


def _mem_kv_kernel(mem_ref, w_ref, o_ref):
    o_ref[...] = _dot(mem_ref[...].astype(BF16), w_ref[...]).astype(BF16)


def _mem_kv(mem2d, w):
    r, n = mem2d.shape[0], w.shape[1]
    return pl.pallas_call(
        _mem_kv_kernel,
        out_shape=jax.ShapeDtypeStruct((r, n), BF16),
        grid=(r // TM_MEM,),
        in_specs=[pl.BlockSpec((TM_MEM, D_MODEL), lambda i: (i, 0)),
                  _const_spec(w.shape)],
        out_specs=pl.BlockSpec((TM_MEM, n), lambda i: (i, 0)),
        compiler_params=pltpu.CompilerParams(dimension_semantics=("parallel",),
                                             vmem_limit_bytes=VMEM_LIMIT),
        name="mem_kv",
    )(mem2d, w)


def _memory_scores(q_all, mkv_ref, h):
    q_scale = MEM_DIM ** -0.5 * LOG2_E
    lo = h * MEM_DIM
    qh = (q_all[:, lo:lo + MEM_DIM] * q_scale).astype(BF16)
    return _dot_nt(qh, mkv_ref[:, lo:lo + MEM_DIM])


def _memory_output(s, mkv_ref, cat_ref, col0, h):
    lo = h * MEM_DIM
    v0 = MEM_HEADS * MEM_DIM + lo
    p = jnp.exp2(s - jnp.max(s, axis=-1, keepdims=True)).astype(BF16)
    v_ext = jnp.concatenate([mkv_ref[:, v0:v0 + MEM_DIM], jnp.ones((N_MEM, LANES), BF16)], axis=1)
    o = _dot(p, v_ext)
    cat_ref[:, col0 + lo:col0 + lo + MEM_DIM] = (o[:, :MEM_DIM] / o[:, MEM_DIM:]).astype(BF16)


def _lane_scan(x, combine, fill):
    n = x.shape[1]
    lane = lax.broadcasted_iota(jnp.int32, x.shape, 1)
    d = 1
    while d < n:
        x = combine(x, jnp.where(lane >= d, pltpu.roll(x, d, axis=1), fill))
        d *= 2
    return x


def _mlstm_gates(gate_all, m_prev, n_chunks):
    length = gate_all.shape[1] // n_chunks
    rows = 2 * MLSTM_HEADS
    gate = jnp.concatenate([gate_all[:, c * length:(c + 1) * length] for c in range(n_chunks)],
                           axis=0)
    head_row = lax.broadcasted_iota(jnp.int32, gate.shape, 0) % rows < MLSTM_HEADS
    cum_f = _lane_scan(_log_sigmoid(gate), jnp.add, 0.0)
    a_all = jnp.where(head_row, gate, 0.0)
    b_all = jnp.where(head_row, pltpu.roll(cum_f, gate.shape[0] - MLSTM_HEADS, axis=0), 0.0)
    r_all = a_all - b_all
    mi_all = b_all + _lane_scan(r_all, jnp.maximum, -jnp.inf)
    out = []
    for c in range(n_chunks):
        b, r, m_intra = (t[c * rows:(c + 1) * rows] for t in (b_all, r_all, mi_all))
        g_tot = b[:, length - 1:length]
        m_inter = b + m_prev
        m_t = jnp.maximum(m_inter, m_intra)
        m_new = jnp.maximum(g_tot + m_prev, jnp.max(g_tot + r, axis=1, keepdims=True))
        decay = jnp.exp(g_tot + m_prev - m_new)
        stack = jnp.concatenate([(b - m_t) * LOG2_E, jnp.exp(m_inter - m_t), jnp.exp(-m_t),
                                 jnp.exp(g_tot + r - m_new),
                                 jnp.zeros((LANES - 4 * rows, length), F32)], axis=0)
        out.append((r * LOG2_E, stack.T, decay))
        m_prev = m_new
    return out, m_prev


def _mixer_a_kernel(x_ref, wmain_ref, wgr_ref, bgr_ref, mkv_ref, wout_ref, g_ref, b_ref, o_ref,
                    c_st, m_st, cat_ref):
    hq = MLSTM_HEADS * MLSTM_QK
    hv = MLSTM_HEADS * MLSTM_V
    cl = MLSTM_CHUNK
    n_groups = wmain_ref.shape[1] // PROJ_GROUP

    @pl.when(pl.program_id(1) == 0)
    def _():
        c_st[...] = jnp.zeros_like(c_st)
        m_st[...] = jnp.zeros_like(m_st)

    causal = (lax.broadcasted_iota(jnp.int32, (cl, cl), 1)
              <= lax.broadcasted_iota(jnp.int32, (cl, cl), 0))
    ones_blk = jnp.ones((cl, LANES), BF16)
    lane_half = lax.broadcasted_iota(jnp.int32, (cl, LANES), 1) // MLSTM_QK

    xbs = [x_ref[c * cl:(c + 1) * cl, :].astype(BF16) for c in range(MLSTM_CHUNKS)]
    gate_all = jnp.concatenate([_dot_nt(wgr_ref[...], xb) for xb in xbs], axis=1) + bgr_ref[...]
    gates, m_st[...] = _mlstm_gates(gate_all, m_st[...], MLSTM_CHUNKS)

    def project(c, grp):
        return _dot(xbs[c], wmain_ref[:, grp * PROJ_GROUP:(grp + 1) * PROJ_GROUP])

    def head(c, proj, h):
        r2, cols, decay = gates[c]
        rows = slice(c * cl, (c + 1) * cl)
        blk = slice((h // 2) * LANES, (h // 2 + 1) * LANES)
        mine = lane_half == h % 2
        q = jnp.where(mine, proj[:, blk], 0.0).astype(BF16)
        k = proj[:, hq:2 * hq][:, blk] * (MLSTM_QK ** -0.5)
        v = proj[:, 2 * hq + h * MLSTM_V:2 * hq + (h + 1) * MLSTM_V].astype(BF16)
        v_ext = jnp.concatenate([v, ones_blk], axis=1)
        o_pre = proj[:, 2 * hq + hv + h * MLSTM_V:2 * hq + hv + (h + 1) * MLSTM_V]
        c_prev = c_st[h]
        inter_b = jnp.broadcast_to(cols[:, 8 + h:9 + h], (cl, MLSTM_V))
        einv_b = jnp.broadcast_to(cols[:, 16 + h:17 + h], (cl, MLSTM_V))

        expo = jnp.where(causal, cols[:, h:h + 1] + r2[h:h + 1, :], -jnp.inf)
        p = _dot_nt(q, k.astype(BF16)) * jnp.exp2(expo)
        num = _dot(p.astype(BF16), v_ext)
        qc = _dot(q, c_prev.astype(BF16))
        nq = num[:, MLSTM_V:] + inter_b * qc[:, MLSTM_V:]
        hh = (num[:, :MLSTM_V] + inter_b * qc[:, :MLSTM_V]) / jnp.maximum(jnp.abs(nq), einv_b)
        hh = hh * jax.nn.sigmoid(o_pre)
        cat_ref[rows, h * MLSTM_V:(h + 1) * MLSTM_V] = hh.astype(BF16)

        kw = jnp.where(mine, k * cols[:, 24 + h:25 + h], 0.0).astype(BF16)
        c_st[h] = decay[h:h + 1, :] * c_prev + lax.dot_general(kw, v_ext, TN_DIMS,
                                                               preferred_element_type=F32)

    proj = jnp.concatenate([project(0, grp) for grp in range(n_groups)], axis=1)
    for c in range(MLSTM_CHUNKS):
        rows = slice(c * cl, (c + 1) * cl)
        nxt = {}
        for h in range(MLSTM_HEADS):
            head(c, proj, h)
            if c + 1 < MLSTM_CHUNKS:
                for grp in range(h, n_groups, MLSTM_HEADS):
                    nxt[grp] = project(c + 1, grp)
        q_mem = proj[:, 2 * hq + 2 * hv:]
        for h in range(MEM_HEADS):
            _memory_output(_memory_scores(q_mem, mkv_ref, h), mkv_ref, cat_ref.at[rows], hv, h)
        _out_proj_norm(x_ref.at[rows], cat_ref.at[rows], wout_ref, g_ref, b_ref, o_ref.at[rows])
        if nxt:
            proj = jnp.concatenate([nxt[grp] for grp in range(n_groups)], axis=1)


def _mixer_a(x2d, wmain, wgr, bgr, memkv, wout, g, b, batch, seq):
    ns = seq // TS_A
    width = MLSTM_HEADS * MLSTM_V + MEM_HEADS * MEM_DIM
    return pl.pallas_call(
        _mixer_a_kernel,
        out_shape=jax.ShapeDtypeStruct(x2d.shape, F32),
        grid=(batch, ns),
        in_specs=[pl.BlockSpec((TS_A, D_MODEL), lambda bi, si: (bi * ns + si, 0)),
                  _const_spec(wmain.shape), _const_spec(wgr.shape), _const_spec(bgr.shape),
                  pl.BlockSpec((N_MEM, 2 * MEM_HEADS * MEM_DIM), lambda bi, si: (bi, 0)),
                  _const_spec(wout.shape), _const_spec(g.shape), _const_spec(b.shape)],
        out_specs=pl.BlockSpec((TS_A, D_MODEL), lambda bi, si: (bi * ns + si, 0)),
        scratch_shapes=[pltpu.VMEM((MLSTM_HEADS, LANES, 2 * MLSTM_V), F32),
                        pltpu.VMEM((2 * MLSTM_HEADS, 1), F32),
                        pltpu.VMEM((TS_A, width), BF16)],
        compiler_params=pltpu.CompilerParams(dimension_semantics=("parallel", "arbitrary"),
                                             vmem_limit_bytes=VMEM_LIMIT),
        name="mixer_a",
    )(x2d, wmain, wgr, bgr, memkv, wout, g, b)


def _shared_kv_rows(y, ctab, stab, wd_ref, gk_ref, wuk_ref, wuvt_ref):
    d = _dot(y.astype(BF16), wd_ref[...])
    ckv = d[:, :KV_LORA]
    ckv = ckv * lax.rsqrt(jnp.mean(ckv * ckv, axis=-1, keepdims=True) + RMS_EPS) * gk_ref[...]
    ckv = ckv.astype(BF16)
    k_rope = (d[:, KV_LORA:KV_LORA + LANES] * ctab
              + d[:, KV_LORA + LANES:KV_LORA + 2 * LANES] * stab)
    k_nope = _dot(ckv, wuk_ref[...])
    k = jnp.concatenate([(k_nope[:, h * HEAD_PAD:(h + 1) * HEAD_PAD] + k_rope).astype(BF16)
                         for h in range(MLA_HEADS)], axis=1)
    vt = _dot_nt(wuvt_ref[...], ckv).astype(BF16)
    ones = jnp.ones((BF16_SUBLANES, vt.shape[1]), BF16)
    pieces = []
    for h in range(MLA_HEADS):
        pieces += [vt[h * MLA_V:(h + 1) * MLA_V], ones]
    return k, jnp.concatenate(pieces, axis=0)


def _ffn_rows(x, wup_ref, wdn_ref, g_ref, b_ref):
    xb = x.astype(BF16)
    acc = jnp.zeros(x.shape, F32)
    for j in range(D_FF // FF_CHUNK):
        hid = _dot(xb, wup_ref[:, j * FF_CHUNK:(j + 1) * FF_CHUNK])
        hid = jnp.square(jnp.maximum(hid, 0.0)).astype(BF16)
        acc = acc + _dot(hid, wdn_ref[j * FF_CHUNK:(j + 1) * FF_CHUNK, :])
    return _layer_norm(ALPHA * x + acc, g_ref[...], b_ref[...])


def _ffn_kernel(x_ref, wup_ref, wdn_ref, g_ref, b_ref, o_ref):
    for r in range(x_ref.shape[0] // FFN_ROWS):
        rows = slice(r * FFN_ROWS, (r + 1) * FFN_ROWS)
        o_ref[rows, :] = _ffn_rows(x_ref[rows, :], wup_ref, wdn_ref, g_ref, b_ref)


def _ffn_kv_kernel(x_ref, wup_ref, wdn_ref, g_ref, b_ref, c_ref, s_ref, wd_ref, gk_ref, wuk_ref,
                   wuvt_ref, o_ref, k_ref, vt_ref):
    n_sub = x_ref.shape[0] // FFN_ROWS
    ys = []
    for r in range(n_sub + 1):
        if r < n_sub:
            rows = slice(r * FFN_ROWS, (r + 1) * FFN_ROWS)
            ys.append(_ffn_rows(x_ref[rows, :], wup_ref, wdn_ref, g_ref, b_ref))
            o_ref[rows, :] = ys[r]
        if r > 0:
            prev = slice((r - 1) * FFN_ROWS, r * FFN_ROWS)
            k_ref[prev, :], vt_ref[r - 1] = _shared_kv_rows(
                ys[r - 1], c_ref[prev, :], s_ref[prev, :], wd_ref, gk_ref, wuk_ref, wuvt_ref)


def _ffn(x2d, wup, wdn, g, b, kv_args=None):
    t = x2d.shape[0]
    tile = pl.BlockSpec((TM_FFN, D_MODEL), lambda i: (i, 0))
    in_specs = [tile, _const_spec(wup.shape), _const_spec(wdn.shape),
                _const_spec(g.shape), _const_spec(b.shape)]
    params = pltpu.CompilerParams(dimension_semantics=("parallel",), vmem_limit_bytes=VMEM_LIMIT)
    if kv_args is None:
        return pl.pallas_call(
            _ffn_kernel, out_shape=jax.ShapeDtypeStruct(x2d.shape, F32), grid=(t // TM_FFN,),
            in_specs=in_specs, out_specs=tile, compiler_params=params, name="ffn",
        )(x2d, wup, wdn, g, b)
    kw, vw = MLA_HEADS * HEAD_PAD, MLA_HEADS * VT_ROWS
    table = pl.BlockSpec((TM_FFN, LANES), lambda i: (i, 0))
    in_specs += [table, table] + [_const_spec(w.shape) for w in kv_args[2:]]
    return pl.pallas_call(
        _ffn_kv_kernel,
        out_shape=(jax.ShapeDtypeStruct(x2d.shape, F32), jax.ShapeDtypeStruct((t, kw), BF16),
                   jax.ShapeDtypeStruct((t // TK_B, vw, TK_B), BF16)),
        grid=(t // TM_FFN,), in_specs=in_specs,
        out_specs=(tile, pl.BlockSpec((TM_FFN, kw), lambda i: (i, 0)),
                   pl.BlockSpec((TM_FFN // TK_B, vw, TK_B), lambda i: (i, 0, 0))),
        compiler_params=params, name="ffn_kv",
    )(x2d, wup, wdn, g, b, *kv_args)


def _mixer_b_kernel(x_ref, win_ref, gq_ref, wuq_ref, wuqs_ref, c_ref, s_ref, k_ref, vt_ref,
                    mkv_ref, wout_ref, g_ref, b_ref, o_ref, q_sc, sa_sc, sb_sc, m_sc, acc_sc,
                    ot_sc, cat_ref):
    tq = TQ_B
    step = pl.program_id(1)
    q_scale = (MLA_NOPE + MLA_ROPE) ** -0.5 * LOG2_E
    key_chunk = lax.broadcasted_iota(jnp.int32, (TK_B, tq), 0) // MLA_CHUNK
    qry_chunk = lax.broadcasted_iota(jnp.int32, (TK_B, tq), 1) // MLA_CHUNK
    allowed = key_chunk <= qry_chunk

    def in_proj(rows):
        return _dot(x_ref[rows, :].astype(BF16), win_ref[...])

    def queries(rows, proj):
        cq = proj[:, :Q_LORA]
        cq = cq * lax.rsqrt(jnp.mean(cq * cq, axis=-1, keepdims=True) + RMS_EPS) * gq_ref[...]
        cq = cq.astype(BF16)
        q_lin = _dot(cq, wuq_ref[...])
        q_swp = _dot(cq, wuqs_ref[...])
        ctab = c_ref[rows, :]
        stab = s_ref[rows, :]
        for h in range(MLA_HEADS):
            sl = slice(h * HEAD_PAD, (h + 1) * HEAD_PAD)
            q_sc[rows, sl] = ((q_lin[:, sl] * ctab + q_swp[:, sl] * stab) * q_scale).astype(BF16)

    def attention(rows, odd):
        qi = 2 * step + odd

        def scores(j, h):
            kb = k_ref[pl.ds(pl.multiple_of(j * TK_B, TK_B), TK_B), h * HEAD_PAD:(h + 1) * HEAD_PAD]
            return _dot_nt(kb, q_sc[rows, h * HEAD_PAD:(h + 1) * HEAD_PAD])

        def softmax_pv(j, h, src, masked):
            s = src[h]
            if masked:
                s = jnp.where(allowed, s, -jnp.inf)
            m_old = m_sc[h]
            m_new = jnp.maximum(m_old, jnp.max(s, axis=0, keepdims=True))
            corr = jnp.exp2(m_old - m_new)
            p = jnp.exp2(s - m_new).astype(BF16)
            acc_sc[h] = corr * acc_sc[h] + _dot(vt_ref[j, h * VT_ROWS:(h + 1) * VT_ROWS, :], p)
            m_sc[h] = m_new

        def key_tile(j, src, dst, masked):
            if dst is not None:
                for h in range(QK_AHEAD):
                    dst[h] = scores(j + 1, h)
            for h in range(MLA_HEADS):
                softmax_pv(j, h, src, masked)
                if dst is not None and h + QK_AHEAD < MLA_HEADS:
                    dst[h + QK_AHEAD] = scores(j + 1, h + QK_AHEAD)

        m_sc[...] = jnp.full(m_sc.shape, -jnp.inf, F32)
        acc_sc[...] = jnp.zeros_like(acc_sc)
        for h in range(MLA_HEADS):
            sa_sc[h] = scores(0, h)

        def tile_pair(k, carry):
            key_tile(2 * k, sa_sc, sb_sc, False)
            key_tile(2 * k + 1, sb_sc, sa_sc, False)
            return carry

        lax.fori_loop(0, step, tile_pair, 0)
        if odd:
            key_tile(qi - 1, sa_sc, sb_sc, False)
            key_tile(qi, sb_sc, None, True)
        else:
            key_tile(qi, sa_sc, None, True)
        for h in range(MLA_HEADS):
            acc = acc_sc[h]
            ot_sc[h * MLA_V:(h + 1) * MLA_V, :] = acc[:MLA_V] / acc[MLA_V:MLA_V + 1]
        cat_ref[rows, :MLA_HEADS * MLA_V] = ot_sc[...].T.astype(BF16)

    rows0, rows1 = slice(0, tq), slice(tq, 2 * tq)
    proj0 = in_proj(rows0)
    queries(rows0, proj0)
    attention(rows0, 0)
    q_mem0 = proj0[:, Q_LORA:]
    mem_scores = [_memory_scores(q_mem0, mkv_ref, h) for h in range(MEM_HEADS)]
    proj1 = in_proj(rows1)
    for h in range(MEM_HEADS):
        _memory_output(mem_scores[h], mkv_ref, cat_ref.at[rows0], MLA_HEADS * MLA_V, h)
    queries(rows1, proj1)
    _out_proj_norm(x_ref.at[rows0], cat_ref.at[rows0], wout_ref, g_ref, b_ref, o_ref.at[rows0])
    attention(rows1, 1)
    q_mem1 = proj1[:, Q_LORA:]
    mem_scores = [_memory_scores(q_mem1, mkv_ref, h) for h in range(MEM_HEADS)]
    for h in range(MEM_HEADS):
        _memory_output(mem_scores[h], mkv_ref, cat_ref.at[rows1], MLA_HEADS * MLA_V, h)
    _out_proj_norm(x_ref.at[rows1], cat_ref.at[rows1], wout_ref, g_ref, b_ref, o_ref.at[rows1])


def _mixer_b(x2d, win, gq, wuq, wuqs, ctab, stab, k_all, vt_all, memkv, wout, g, b, batch, seq):
    assert TQ_B == TK_B
    ts = 2 * TQ_B
    ns = seq // ts
    width = MLA_HEADS * MLA_V + MEM_HEADS * MEM_DIM
    tile = lambda bi, si: (bi * ns + si, 0)
    return pl.pallas_call(
        _mixer_b_kernel,
        out_shape=jax.ShapeDtypeStruct(x2d.shape, F32),
        grid=(batch, ns),
        in_specs=[pl.BlockSpec((ts, D_MODEL), tile),
                  _const_spec(win.shape), _const_spec(gq.shape),
                  _const_spec(wuq.shape), _const_spec(wuqs.shape),
                  pl.BlockSpec((ts, LANES), tile), pl.BlockSpec((ts, LANES), tile),
                  pl.BlockSpec((seq, k_all.shape[1]), lambda bi, si: (bi, 0)),
                  pl.BlockSpec((seq // TK_B,) + vt_all.shape[1:], lambda bi, si: (bi, 0, 0)),
                  pl.BlockSpec((N_MEM, 2 * MEM_HEADS * MEM_DIM), lambda bi, si: (bi, 1)),
                  _const_spec(wout.shape), _const_spec(g.shape), _const_spec(b.shape)],
        out_specs=pl.BlockSpec((ts, D_MODEL), tile),
        scratch_shapes=[pltpu.VMEM((ts, MLA_HEADS * HEAD_PAD), BF16),
                        pltpu.VMEM((MLA_HEADS, TK_B, TQ_B), F32),
                        pltpu.VMEM((MLA_HEADS, TK_B, TQ_B), F32),
                        pltpu.VMEM((MLA_HEADS, 1, TQ_B), F32),
                        pltpu.VMEM((MLA_HEADS, VT_ROWS, TQ_B), F32),
                        pltpu.VMEM((MLA_HEADS * MLA_V, TQ_B), F32),
                        pltpu.VMEM((ts, width), BF16)],
        compiler_params=pltpu.CompilerParams(dimension_semantics=("parallel", "arbitrary"),
                                             vmem_limit_bytes=VMEM_LIMIT),
        name="mixer_b",
    )(x2d, win, gq, wuq, wuqs, ctab, stab, k_all, vt_all, memkv, wout, g, b)


def _pad_heads(w, heads, dim):
    r = w.shape[0]
    w = w.reshape(r, heads, dim)
    w = jnp.pad(w, ((0, 0), (0, 0), (0, HEAD_PAD - dim)))
    return w.reshape(r, heads * HEAD_PAD)


def _swap_rope_halves(w, heads):
    r = w.shape[0]
    w = w.reshape(r, heads, MLA_NOPE + MLA_ROPE)
    x1 = w[..., MLA_NOPE:MLA_NOPE + ROPE_HALF]
    x2 = w[..., MLA_NOPE + ROPE_HALF:]
    return jnp.concatenate([jnp.zeros_like(w[..., :MLA_NOPE]), x2, x1], axis=-1).reshape(r, -1)


def kernel(x, mem, positions, a_w_in, a_b_igate, a_b_fgate, a_w_mem_kv, a_w_out, kv_w_down, kv_norm_g, kv_w_uk, kv_w_uv, b_w_in, b_q_norm_g, b_w_uq, b_w_mem_kv, b_w_out, ln1_g, ln1_b, ffn_w_up, ffn_w_down, ln2_g, ln2_b):
    batch, seq, _ = x.shape
    t = batch * seq
    x2d = x.reshape(t, D_MODEL)
    row = lambda v: v.reshape(1, -1).astype(F32)

    inv_freq = ROPE_THETA ** (-jnp.arange(0, MLA_ROPE, 2, dtype=F32) / MLA_ROPE)
    ctab, stab = _rope_tables(positions.reshape(1, t).astype(F32), inv_freq.reshape(ROPE_HALF, 1))

    memkv = _mem_kv(mem.reshape(batch * N_MEM, D_MODEL),
                    jnp.concatenate([a_w_mem_kv[0], b_w_mem_kv[0]], axis=1).astype(BF16))

    hq = MLSTM_HEADS * MLSTM_QK
    hv = MLSTM_HEADS * MLSTM_V
    g0 = 2 * hq + 2 * hv
    w_in = a_w_in[0]
    wmain = jnp.concatenate([w_in[:, :g0], w_in[:, g0 + 2 * MLSTM_HEADS:]], axis=1).astype(BF16)
    wgr = w_in[:, g0:g0 + 2 * MLSTM_HEADS].T.astype(BF16)
    bgr = jnp.concatenate([a_b_igate[0], a_b_fgate[0]]).astype(F32).reshape(2 * MLSTM_HEADS, 1)
    x2d = _mixer_a(x2d, wmain, wgr, bgr, memkv, a_w_out[0].astype(BF16),
                   row(ln1_g[0]), row(ln1_b[0]), batch, seq)

    wd = jnp.zeros((D_MODEL, KV_LORA + 2 * LANES), F32)
    wd = wd.at[:, :KV_LORA].set(kv_w_down[:, :KV_LORA])
    r0 = KV_LORA + ROPE_LO
    wd = wd.at[:, r0:r0 + MLA_ROPE].set(kv_w_down[:, KV_LORA:])
    r1 = KV_LORA + LANES + ROPE_LO
    wd = wd.at[:, r1:r1 + ROPE_HALF].set(kv_w_down[:, KV_LORA + ROPE_HALF:])
    wd = wd.at[:, r1 + ROPE_HALF:r1 + MLA_ROPE].set(kv_w_down[:, KV_LORA:KV_LORA + ROPE_HALF])
    kv_args = (ctab, stab, wd.astype(BF16), row(kv_norm_g),
               _pad_heads(kv_w_uk, MLA_HEADS, MLA_NOPE).astype(BF16), kv_w_uv.T.astype(BF16))
    x2d, k_all, vt_all = _ffn(x2d, ffn_w_up[0].astype(BF16), ffn_w_down[0].astype(BF16),
                              row(ln2_g[0]), row(ln2_b[0]), kv_args)

    wuq = _pad_heads(b_w_uq[0], MLA_HEADS, MLA_NOPE + MLA_ROPE).astype(BF16)
    wuqs = _pad_heads(_swap_rope_halves(b_w_uq[0], MLA_HEADS), MLA_HEADS,
                      MLA_NOPE + MLA_ROPE).astype(BF16)
    x2d = _mixer_b(x2d, b_w_in[0].astype(BF16), row(b_q_norm_g[0]), wuq, wuqs, ctab, stab,
                   k_all, vt_all, memkv, b_w_out[0].astype(BF16),
                   row(ln1_g[1]), row(ln1_b[1]), batch, seq)
    x2d = _ffn(x2d, ffn_w_up[1].astype(BF16), ffn_w_down[1].astype(BF16),
               row(ln2_g[1]), row(ln2_b[1]))
    return x2d.reshape(batch, seq, D_MODEL)
```

```python
import functools

import jax
import jax.numpy as jnp
from jax import lax
from jax.experimental import pallas as pl
from jax.experimental.pallas import tpu as pltpu

F32 = jnp.float32
BF16 = jnp.bfloat16

D_MODEL = 1024
DEPTH = 2
N_MEM = 256
MLSTM_HEADS = 4
MLSTM_QK = 64
MLSTM_V = 128
MEM_HEADS = 4
MEM_DIM = 128
MLA_HEADS = 8
MLA_NOPE = 64
MLA_ROPE = 32
MLA_V = 64
Q_LORA = 256
KV_LORA = 256
D_FF = 4 * D_MODEL
ROPE_THETA = 10000.0
LN_EPS = 1e-5
RMS_EPS = 1e-6
ALPHA = (2 * DEPTH) ** 0.25
MLA_CHUNK = 64
BF16_SUBLANES = 16
VT_ROWS = MLA_V + BF16_SUBLANES
QK_AHEAD = 2
LOG2_E = 1.4426950408889634

LANES = 128
HEAD_PAD = 128
ROPE_LO = MLA_NOPE
ROPE_HALF = MLA_ROPE // 2

MLSTM_CHUNK = 256
MLSTM_CHUNKS = 2
TS_A = MLSTM_CHUNK * MLSTM_CHUNKS
HEADS_LOCKSTEP = 2
TQ_B = 256
TK_B = 256
MIXB_TILES = 4
TM_FFN = 1024
TM_ROPE = 2048
TM_MEM = 512
FF_CHUNK = 1024
FFN_ROWS = TK_B
VMEM_LIMIT = 56 * 1024 * 1024

NT_DIMS = (((1,), (1,)), ((), ()))
TN_DIMS = (((0,), (0,)), ((), ()))


def _dot(a, b):
    return jnp.dot(a, b, preferred_element_type=F32)


def _dot_nt(a, b):
    return lax.dot_general(a, b, NT_DIMS, preferred_element_type=F32)


def _layer_norm(y, g, b):
    mu = jnp.mean(y, axis=-1, keepdims=True)
    yc = y - mu
    var = jnp.mean(yc * yc, axis=-1, keepdims=True)
    return yc * lax.rsqrt(var + LN_EPS) * g + b


def _log_sigmoid(z):
    return jnp.minimum(z, 0.0) - jnp.log(1.0 + jnp.exp(-jnp.abs(z)))


def _out_proj_norm(x_ref, cat_ref, wout_ref, g_ref, b_ref, o_ref):
    mix = _dot(cat_ref[...], wout_ref[...])
    o_ref[...] = _layer_norm(ALPHA * x_ref[...] + mix, g_ref[...], b_ref[...])


def _const_spec(shape):
    nd = len(shape)
    return pl.BlockSpec(shape, lambda *_: (0,) * nd, pipeline_mode=pl.Buffered(1))


def _rope_table_kernel(pos_ref, invf_ref, c_ref, s_ref):
    ang = invf_ref[...] * pos_ref[...]
    cos = jnp.cos(ang)
    sin = jnp.sin(ang)
    tm = ang.shape[1]
    tail = LANES - ROPE_LO - MLA_ROPE
    ct = jnp.concatenate([jnp.ones((ROPE_LO, tm), F32), cos, cos, jnp.ones((tail, tm), F32)], axis=0)
    st = jnp.concatenate([jnp.zeros((ROPE_LO, tm), F32), -sin, sin, jnp.zeros((tail, tm), F32)], axis=0)
    c_ref[...] = ct.T
    s_ref[...] = st.T


def _rope_tables(pos_row, invf_col):
    t = pos_row.shape[1]
    return pl.pallas_call(
        _rope_table_kernel,
        out_shape=(jax.ShapeDtypeStruct((t, LANES), F32),) * 2,
        grid=(t // TM_ROPE,),
        in_specs=[pl.BlockSpec((1, TM_ROPE), lambda i: (0, i)),
                  _const_spec((ROPE_HALF, 1))],
        out_specs=(pl.BlockSpec((TM_ROPE, LANES), lambda i: (i, 0)),) * 2,
        compiler_params=pltpu.CompilerParams(dimension_semantics=("parallel",)),
        name="rope_tables",
    )(pos_row, invf_col)


def _mem_kv_kernel(mem_ref, w_ref, o_ref):
    o_ref[...] = _dot(mem_ref[...].astype(BF16), w_ref[...]).astype(BF16)


def _mem_kv(mem2d, w):
    r, n = mem2d.shape[0], w.shape[1]
    return pl.pallas_call(
        _mem_kv_kernel,
        out_shape=jax.ShapeDtypeStruct((r, n), BF16),
        grid=(r // TM_MEM,),
        in_specs=[pl.BlockSpec((TM_MEM, D_MODEL), lambda i: (i, 0)),
                  _const_spec(w.shape)],
        out_specs=pl.BlockSpec((TM_MEM, n), lambda i: (i, 0)),
        compiler_params=pltpu.CompilerParams(dimension_semantics=("parallel",),
                                             vmem_limit_bytes=VMEM_LIMIT),
        name="mem_kv",
    )(mem2d, w)


def _memory_scores(q_all, mkv_ref, h):
    q_scale = MEM_DIM ** -0.5 * LOG2_E
    lo = h * MEM_DIM
    qh = (q_all[:, lo:lo + MEM_DIM] * q_scale).astype(BF16)
    return _dot_nt(qh, mkv_ref[:, lo:lo + MEM_DIM])


def _memory_output(s, mkv_ref, cat_ref, col0, h):
    lo = h * MEM_DIM
    v0 = MEM_HEADS * MEM_DIM + lo
    p = jnp.exp2(s - jnp.max(s, axis=-1, keepdims=True)).astype(BF16)
    v_ext = jnp.concatenate([mkv_ref[:, v0:v0 + MEM_DIM], jnp.ones((N_MEM, LANES), BF16)], axis=1)
    o = _dot(p, v_ext)
    cat_ref[:, col0 + lo:col0 + lo + MEM_DIM] = (o[:, :MEM_DIM] / o[:, MEM_DIM:]).astype(BF16)


def _lane_scan(x, combine, fill):
    n = x.shape[1]
    lane = lax.broadcasted_iota(jnp.int32, x.shape, 1)
    d = 1
    while d < n:
        x = combine(x, jnp.where(lane >= d, pltpu.roll(x, d, axis=1), fill))
        d *= 2
    return x


def _mlstm_gates(gate_all, m_prev, n_chunks):
    length = gate_all.shape[1] // n_chunks
    rows = 2 * MLSTM_HEADS
    gate = jnp.concatenate([gate_all[:, c * length:(c + 1) * length] for c in range(n_chunks)],
                           axis=0)
    head_row = lax.broadcasted_iota(jnp.int32, gate.shape, 0) % rows < MLSTM_HEADS
    cum_f = _lane_scan(_log_sigmoid(gate), jnp.add, 0.0)
    a_all = jnp.where(head_row, gate, 0.0)
    b_all = jnp.where(head_row, pltpu.roll(cum_f, gate.shape[0] - MLSTM_HEADS, axis=0), 0.0)
    r_all = a_all - b_all
    mi_all = b_all + _lane_scan(r_all, jnp.maximum, -jnp.inf)
    out = []
    for c in range(n_chunks):
        b, r, m_intra = (t[c * rows:(c + 1) * rows] for t in (b_all, r_all, mi_all))
        g_tot = b[:, length - 1:length]
        m_inter = b + m_prev
        m_t = jnp.maximum(m_inter, m_intra)
        m_new = jnp.maximum(g_tot + m_prev, jnp.max(g_tot + r, axis=1, keepdims=True))
        decay = jnp.exp(g_tot + m_prev - m_new)
        stack = jnp.concatenate([(b - m_t) * LOG2_E, jnp.exp(m_inter - m_t), jnp.exp(-m_t),
                                 jnp.exp(g_tot + r - m_new),
                                 jnp.zeros((LANES - 4 * rows, length), F32)], axis=0)
        out.append((r * LOG2_E, stack.T, decay))
        m_prev = m_new
    return out, m_prev


def _mixer_a_kernel(x_ref, wmain_ref, wgr_ref, bgr_ref, mkv_ref, wout_ref, g_ref, b_ref, o_ref,
                    c_st, m_st, cat_ref):
    hq = MLSTM_HEADS * MLSTM_QK
    hv = MLSTM_HEADS * MLSTM_V
    cl = MLSTM_CHUNK

    @pl.when(pl.program_id(1) == 0)
    def _():
        c_st[...] = jnp.zeros_like(c_st)
        m_st[...] = jnp.zeros_like(m_st)

    causal = (lax.broadcasted_iota(jnp.int32, (cl, cl), 1)
              <= lax.broadcasted_iota(jnp.int32, (cl, cl), 0))
    ones_blk = jnp.ones((cl, LANES), BF16)
    lane_half = lax.broadcasted_iota(jnp.int32, (cl, LANES), 1) // MLSTM_QK

    xbs = [x_ref[c * cl:(c + 1) * cl, :].astype(BF16) for c in range(MLSTM_CHUNKS)]
    gate_all = jnp.concatenate([_dot_nt(wgr_ref[...], xb) for xb in xbs], axis=1) + bgr_ref[...]
    gates, m_st[...] = _mlstm_gates(gate_all, m_st[...], MLSTM_CHUNKS)

    def head_scores(c, proj, h):
        blk = slice((h // 2) * LANES, (h // 2 + 1) * LANES)
        mine = lane_half == h % 2
        q = jnp.where(mine, proj[:, blk], 0.0).astype(BF16)
        k = proj[:, hq:2 * hq][:, blk] * (MLSTM_QK ** -0.5)
        v = proj[:, 2 * hq + h * MLSTM_V:2 * hq + (h + 1) * MLSTM_V].astype(BF16)
        v_ext = jnp.concatenate([v, ones_blk], axis=1)
        return q, k, v_ext, mine, _dot_nt(q, k.astype(BF16))

    def head_pv(c, h, parts):
        r2, cols, _ = gates[c]
        _, _, v_ext, _, qk = parts
        expo = jnp.where(causal, cols[:, h:h + 1] + r2[h:h + 1, :], -jnp.inf)
        p = qk * jnp.exp2(expo)
        return _dot(p.astype(BF16), v_ext)

    def head_finish(c, proj, h, parts, num):
        _, cols, decay = gates[c]
        q, k, v_ext, mine, _ = parts
        rows = slice(c * cl, (c + 1) * cl)
        o_pre = proj[:, 2 * hq + hv + h * MLSTM_V:2 * hq + hv + (h + 1) * MLSTM_V]
        c_prev = c_st[h]
        inter_b = jnp.broadcast_to(cols[:, 8 + h:9 + h], (cl, MLSTM_V))
        einv_b = jnp.broadcast_to(cols[:, 16 + h:17 + h], (cl, MLSTM_V))
        qc = _dot(q, c_prev.astype(BF16))
        nq = num[:, MLSTM_V:] + inter_b * qc[:, MLSTM_V:]
        hh = (num[:, :MLSTM_V] + inter_b * qc[:, :MLSTM_V]) / jnp.maximum(jnp.abs(nq), einv_b)
        hh = hh * jax.nn.sigmoid(o_pre)
        cat_ref[rows, h * MLSTM_V:(h + 1) * MLSTM_V] = hh.astype(BF16)
        kw = jnp.where(mine, k * cols[:, 24 + h:25 + h], 0.0).astype(BF16)
        c_st[h] = decay[h:h + 1, :] * c_prev + lax.dot_general(kw, v_ext, TN_DIMS,
                                                               preferred_element_type=F32)

    chunks = range(MLSTM_CHUNKS)
    projs = [_dot(xbs[c], wmain_ref[...]) for c in chunks]
    for h0 in range(0, MLSTM_HEADS, HEADS_LOCKSTEP):
        group = [(c, h) for h in range(h0, h0 + HEADS_LOCKSTEP) for c in chunks]
        parts = [head_scores(c, projs[c], h) for c, h in group]
        nums = [head_pv(c, h, parts[i]) for i, (c, h) in enumerate(group)]
        for i, (c, h) in enumerate(group):
            head_finish(c, projs[c], h, parts[i], nums[i])
    for c in chunks:
        rows = slice(c * cl, (c + 1) * cl)
        q_mem = projs[c][:, 2 * hq + 2 * hv:]
        for h in range(MEM_HEADS):
            _memory_output(_memory_scores(q_mem, mkv_ref, h), mkv_ref, cat_ref.at[rows], hv, h)
        _out_proj_norm(x_ref.at[rows], cat_ref.at[rows], wout_ref, g_ref, b_ref, o_ref.at[rows])


def _mixer_a(x2d, wmain, wgr, bgr, memkv, wout, g, b, batch, seq):
    ns = seq // TS_A
    width = MLSTM_HEADS * MLSTM_V + MEM_HEADS * MEM_DIM
    return pl.pallas_call(
        _mixer_a_kernel,
        out_shape=jax.ShapeDtypeStruct(x2d.shape, F32),
        grid=(batch, ns),
        in_specs=[pl.BlockSpec((TS_A, D_MODEL), lambda bi, si: (bi * ns + si, 0)),
                  _const_spec(wmain.shape), _const_spec(wgr.shape), _const_spec(bgr.shape),
                  pl.BlockSpec((N_MEM, 2 * MEM_HEADS * MEM_DIM), lambda bi, si: (bi, 0)),
                  _const_spec(wout.shape), _const_spec(g.shape), _const_spec(b.shape)],
        out_specs=pl.BlockSpec((TS_A, D_MODEL), lambda bi, si: (bi * ns + si, 0)),
        scratch_shapes=[pltpu.VMEM((MLSTM_HEADS, LANES, 2 * MLSTM_V), F32),
                        pltpu.VMEM((2 * MLSTM_HEADS, 1), F32),
                        pltpu.VMEM((TS_A, width), BF16)],
        compiler_params=pltpu.CompilerParams(dimension_semantics=("parallel", "arbitrary"),
                                             vmem_limit_bytes=VMEM_LIMIT),
        name="mixer_a",
    )(x2d, wmain, wgr, bgr, memkv, wout, g, b)


def _shared_kv_rows(y, ctab, stab, wd_ref, gk_ref, wuk_ref, wuvt_ref):
    d = _dot(y.astype(BF16), wd_ref[...])
    ckv = d[:, :KV_LORA]
    ckv = ckv * lax.rsqrt(jnp.mean(ckv * ckv, axis=-1, keepdims=True) + RMS_EPS) * gk_ref[...]
    ckv = ckv.astype(BF16)
    k_rope = (d[:, KV_LORA:KV_LORA + LANES] * ctab
              + d[:, KV_LORA + LANES:KV_LORA + 2 * LANES] * stab)
    k_nope = _dot(ckv, wuk_ref[...])
    k = jnp.concatenate([(k_nope[:, h * HEAD_PAD:(h + 1) * HEAD_PAD] + k_rope).astype(BF16)
                         for h in range(MLA_HEADS)], axis=1)
    vt = _dot_nt(wuvt_ref[...], ckv).astype(BF16)
    ones = jnp.ones((BF16_SUBLANES, vt.shape[1]), BF16)
    pieces = []
    for h in range(MLA_HEADS):
        pieces += [vt[h * MLA_V:(h + 1) * MLA_V], ones]
    return k, jnp.concatenate(pieces, axis=0)


def _ffn_rows(x, wup_ref, wdn_ref, g_ref, b_ref):
    xb = x.astype(BF16)
    acc = jnp.zeros(x.shape, F32)
    for j in range(D_FF // FF_CHUNK):
        hid = _dot(xb, wup_ref[:, j * FF_CHUNK:(j + 1) * FF_CHUNK])
        hid = jnp.square(jnp.maximum(hid, 0.0)).astype(BF16)
        acc = acc + _dot(hid, wdn_ref[j * FF_CHUNK:(j + 1) * FF_CHUNK, :])
    return _layer_norm(ALPHA * x + acc, g_ref[...], b_ref[...])


def _ffn_kernel(x_ref, wup_ref, wdn_ref, g_ref, b_ref, o_ref):
    for r in range(x_ref.shape[0] // FFN_ROWS):
        rows = slice(r * FFN_ROWS, (r + 1) * FFN_ROWS)
        o_ref[rows, :] = _ffn_rows(x_ref[rows, :], wup_ref, wdn_ref, g_ref, b_ref)


def _ffn_kv_kernel(x_ref, wup_ref, wdn_ref, g_ref, b_ref, c_ref, s_ref, wd_ref, gk_ref, wuk_ref,
                   wuvt_ref, o_ref, k_ref, vt_ref):
    n_sub = x_ref.shape[0] // FFN_ROWS
    ys = []
    for r in range(n_sub + 1):
        if r < n_sub:
            rows = slice(r * FFN_ROWS, (r + 1) * FFN_ROWS)
            ys.append(_ffn_rows(x_ref[rows, :], wup_ref, wdn_ref, g_ref, b_ref))
            o_ref[rows, :] = ys[r]
        if r > 0:
            prev = slice((r - 1) * FFN_ROWS, r * FFN_ROWS)
            k_ref[prev, :], vt_ref[r - 1] = _shared_kv_rows(
                ys[r - 1], c_ref[prev, :], s_ref[prev, :], wd_ref, gk_ref, wuk_ref, wuvt_ref)


def _ffn(x2d, wup, wdn, g, b, kv_args=None):
    t = x2d.shape[0]
    tile = pl.BlockSpec((TM_FFN, D_MODEL), lambda i: (i, 0))
    in_specs = [tile, _const_spec(wup.shape), _const_spec(wdn.shape),
                _const_spec(g.shape), _const_spec(b.shape)]
    params = pltpu.CompilerParams(dimension_semantics=("parallel",), vmem_limit_bytes=VMEM_LIMIT)
    if kv_args is None:
        return pl.pallas_call(
            _ffn_kernel, out_shape=jax.ShapeDtypeStruct(x2d.shape, F32), grid=(t // TM_FFN,),
            in_specs=in_specs, out_specs=tile, compiler_params=params, name="ffn",
        )(x2d, wup, wdn, g, b)
    kw, vw = MLA_HEADS * HEAD_PAD, MLA_HEADS * VT_ROWS
    table = pl.BlockSpec((TM_FFN, LANES), lambda i: (i, 0))
    in_specs += [table, table] + [_const_spec(w.shape) for w in kv_args[2:]]
    return pl.pallas_call(
        _ffn_kv_kernel,
        out_shape=(jax.ShapeDtypeStruct(x2d.shape, F32), jax.ShapeDtypeStruct((t, kw), BF16),
                   jax.ShapeDtypeStruct((t // TK_B, vw, TK_B), BF16)),
        grid=(t // TM_FFN,), in_specs=in_specs,
        out_specs=(tile, pl.BlockSpec((TM_FFN, kw), lambda i: (i, 0)),
                   pl.BlockSpec((TM_FFN // TK_B, vw, TK_B), lambda i: (i, 0, 0))),
        compiler_params=params, name="ffn_kv",
    )(x2d, wup, wdn, g, b, *kv_args)


def _mixer_b_kernel(x_ref, win_ref, gq_ref, wuq_ref, wuqs_ref, c_ref, s_ref, k_ref, vt_ref,
                    mkv_ref, wout_ref, g_ref, b_ref, o_ref, q_sc, sa_sc, sb_sc, m_sc, acc_sc,
                    ot_sc, cat_ref):
    tq = TQ_B
    step = pl.program_id(1)
    q_scale = (MLA_NOPE + MLA_ROPE) ** -0.5 * LOG2_E
    key_chunk = lax.broadcasted_iota(jnp.int32, (TK_B, tq), 0) // MLA_CHUNK
    qry_chunk = lax.broadcasted_iota(jnp.int32, (TK_B, tq), 1) // MLA_CHUNK
    allowed = key_chunk <= qry_chunk

    def in_proj(rows):
        return _dot(x_ref[rows, :].astype(BF16), win_ref[...])

    def queries(rows, proj):
        cq = proj[:, :Q_LORA]
        cq = cq * lax.rsqrt(jnp.mean(cq * cq, axis=-1, keepdims=True) + RMS_EPS) * gq_ref[...]
        cq = cq.astype(BF16)
        q_lin = _dot(cq, wuq_ref[...])
        q_swp = _dot(cq, wuqs_ref[...])
        ctab = c_ref[rows, :]
        stab = s_ref[rows, :]
        for h in range(MLA_HEADS):
            sl = slice(h * HEAD_PAD, (h + 1) * HEAD_PAD)
            q_sc[rows, sl] = ((q_lin[:, sl] * ctab + q_swp[:, sl] * stab) * q_scale).astype(BF16)

    def attention(rows, idx):
        qi = MIXB_TILES * step + idx
        odd = idx % 2

        def scores(j, h):
            kb = k_ref[pl.ds(pl.multiple_of(j * TK_B, TK_B), TK_B), h * HEAD_PAD:(h + 1) * HEAD_PAD]
            return _dot_nt(kb, q_sc[rows, h * HEAD_PAD:(h + 1) * HEAD_PAD])

        def softmax_pv(j, h, src, masked):
            s = src[h]
            if masked:
                s = jnp.where(allowed, s, -jnp.inf)
            m_old = m_sc[h]
            m_new = jnp.maximum(m_old, jnp.max(s, axis=0, keepdims=True))
            corr = jnp.exp2(m_old - m_new)
            p = jnp.exp2(s - m_new).astype(BF16)
            acc_sc[h] = corr * acc_sc[h] + _dot(vt_ref[j, h * VT_ROWS:(h + 1) * VT_ROWS, :], p)
            m_sc[h] = m_new

        def key_tile(j, src, dst, masked):
            if dst is not None:
                for h in range(QK_AHEAD):
                    dst[h] = scores(j + 1, h)
            for h in range(MLA_HEADS):
                softmax_pv(j, h, src, masked)
                if dst is not None and h + QK_AHEAD < MLA_HEADS:
                    dst[h + QK_AHEAD] = scores(j + 1, h + QK_AHEAD)

        m_sc[...] = jnp.full(m_sc.shape, -jnp.inf, F32)
        acc_sc[...] = jnp.zeros_like(acc_sc)
        for h in range(MLA_HEADS):
            sa_sc[h] = scores(0, h)

        def tile_pair(k, carry):
            key_tile(2 * k, sa_sc, sb_sc, False)
            key_tile(2 * k + 1, sb_sc, sa_sc, False)
            return carry

        lax.fori_loop(0, qi // 2, tile_pair, 0)
        if odd:
            key_tile(qi - 1, sa_sc, sb_sc, False)
            key_tile(qi, sb_sc, None, True)
        else:
            key_tile(qi, sa_sc, None, True)
        for h in range(MLA_HEADS):
            acc = acc_sc[h]
            ot_sc[h * MLA_V:(h + 1) * MLA_V, :] = acc[:MLA_V] / acc[MLA_V:MLA_V + 1]
        cat_ref[rows, :MLA_HEADS * MLA_V] = ot_sc[...].T.astype(BF16)

    tiles = [slice(i * tq, (i + 1) * tq) for i in range(MIXB_TILES)]
    proj = in_proj(tiles[0])
    queries(tiles[0], proj)
    for i, rows in enumerate(tiles):
        attention(rows, i)
        q_mem = proj[:, Q_LORA:]
        mem_scores = [_memory_scores(q_mem, mkv_ref, h) for h in range(MEM_HEADS)]
        if i + 1 < MIXB_TILES:
            proj = in_proj(tiles[i + 1])
        for h in range(MEM_HEADS):
            _memory_output(mem_scores[h], mkv_ref, cat_ref.at[rows], MLA_HEADS * MLA_V, h)
        if i + 1 < MIXB_TILES:
            queries(tiles[i + 1], proj)
        _out_proj_norm(x_ref.at[rows], cat_ref.at[rows], wout_ref, g_ref, b_ref, o_ref.at[rows])


def _mixer_b(x2d, win, gq, wuq, wuqs, ctab, stab, k_all, vt_all, memkv, wout, g, b, batch, seq):
    assert TQ_B == TK_B
    assert MIXB_TILES % 2 == 0
    ts = MIXB_TILES * TQ_B
    ns = seq // ts
    width = MLA_HEADS * MLA_V + MEM_HEADS * MEM_DIM
    tile = lambda bi, si: (bi * ns + si, 0)
    return pl.pallas_call(
        _mixer_b_kernel,
        out_shape=jax.ShapeDtypeStruct(x2d.shape, F32),
        grid=(batch, ns),
        in_specs=[pl.BlockSpec((ts, D_MODEL), tile),
                  _const_spec(win.shape), _const_spec(gq.shape),
                  _const_spec(wuq.shape), _const_spec(wuqs.shape),
                  pl.BlockSpec((ts, LANES), tile), pl.BlockSpec((ts, LANES), tile),
                  pl.BlockSpec((seq, k_all.shape[1]), lambda bi, si: (bi, 0)),
                  pl.BlockSpec((seq // TK_B,) + vt_all.shape[1:], lambda bi, si: (bi, 0, 0)),
                  pl.BlockSpec((N_MEM, 2 * MEM_HEADS * MEM_DIM), lambda bi, si: (bi, 1)),
                  _const_spec(wout.shape), _const_spec(g.shape), _const_spec(b.shape)],
        out_specs=pl.BlockSpec((ts, D_MODEL), tile),
        scratch_shapes=[pltpu.VMEM((ts, MLA_HEADS * HEAD_PAD), BF16),
                        pltpu.VMEM((MLA_HEADS, TK_B, TQ_B), F32),
                        pltpu.VMEM((MLA_HEADS, TK_B, TQ_B), F32),
                        pltpu.VMEM((MLA_HEADS, 1, TQ_B), F32),
                        pltpu.VMEM((MLA_HEADS, VT_ROWS, TQ_B), F32),
                        pltpu.VMEM((MLA_HEADS * MLA_V, TQ_B), F32),
                        pltpu.VMEM((ts, width), BF16)],
        compiler_params=pltpu.CompilerParams(dimension_semantics=("parallel", "arbitrary"),
                                             vmem_limit_bytes=VMEM_LIMIT),
        name="mixer_b",
    )(x2d, win, gq, wuq, wuqs, ctab, stab, k_all, vt_all, memkv, wout, g, b)


def _pad_heads(w, heads, dim):
    r = w.shape[0]
    w = w.reshape(r, heads, dim)
    w = jnp.pad(w, ((0, 0), (0, 0), (0, HEAD_PAD - dim)))
    return w.reshape(r, heads * HEAD_PAD)


def _swap_rope_halves(w, heads):
    r = w.shape[0]
    w = w.reshape(r, heads, MLA_NOPE + MLA_ROPE)
    x1 = w[..., MLA_NOPE:MLA_NOPE + ROPE_HALF]
    x2 = w[..., MLA_NOPE + ROPE_HALF:]
    return jnp.concatenate([jnp.zeros_like(w[..., :MLA_NOPE]), x2, x1], axis=-1).reshape(r, -1)


def kernel(x, mem, positions, a_w_in, a_b_igate, a_b_fgate, a_w_mem_kv, a_w_out, kv_w_down, kv_norm_g, kv_w_uk, kv_w_uv, b_w_in, b_q_norm_g, b_w_uq, b_w_mem_kv, b_w_out, ln1_g, ln1_b, ffn_w_up, ffn_w_down, ln2_g, ln2_b):
    batch, seq, _ = x.shape
    t = batch * seq
    x2d = x.reshape(t, D_MODEL)
    row = lambda v: v.reshape(1, -1).astype(F32)

    inv_freq = ROPE_THETA ** (-jnp.arange(0, MLA_ROPE, 2, dtype=F32) / MLA_ROPE)
    ctab, stab = _rope_tables(positions.reshape(1, t).astype(F32), inv_freq.reshape(ROPE_HALF, 1))

    memkv = _mem_kv(mem.reshape(batch * N_MEM, D_MODEL),
                    jnp.concatenate([a_w_mem_kv[0], b_w_mem_kv[0]], axis=1).astype(BF16))

    hq = MLSTM_HEADS * MLSTM_QK
    hv = MLSTM_HEADS * MLSTM_V
    g0 = 2 * hq + 2 * hv
    w_in = a_w_in[0]
    wmain = jnp.concatenate([w_in[:, :g0], w_in[:, g0 + 2 * MLSTM_HEADS:]], axis=1).astype(BF16)
    wgr = w_in[:, g0:g0 + 2 * MLSTM_HEADS].T.astype(BF16)
    bgr = jnp.concatenate([a_b_igate[0], a_b_fgate[0]]).astype(F32).reshape(2 * MLSTM_HEADS, 1)
    x2d = _mixer_a(x2d, wmain, wgr, bgr, memkv, a_w_out[0].astype(BF16),
                   row(ln1_g[0]), row(ln1_b[0]), batch, seq)

    wd = jnp.zeros((D_MODEL, KV_LORA + 2 * LANES), F32)
    wd = wd.at[:, :KV_LORA].set(kv_w_down[:, :KV_LORA])
    r0 = KV_LORA + ROPE_LO
    wd = wd.at[:, r0:r0 + MLA_ROPE].set(kv_w_down[:, KV_LORA:])
    r1 = KV_LORA + LANES + ROPE_LO
    wd = wd.at[:, r1:r1 + ROPE_HALF].set(kv_w_down[:, KV_LORA + ROPE_HALF:])
    wd = wd.at[:, r1 + ROPE_HALF:r1 + MLA_ROPE].set(kv_w_down[:, KV_LORA:KV_LORA + ROPE_HALF])
    kv_args = (ctab, stab, wd.astype(BF16), row(kv_norm_g),
               _pad_heads(kv_w_uk, MLA_HEADS, MLA_NOPE).astype(BF16), kv_w_uv.T.astype(BF16))
    x2d, k_all, vt_all = _ffn(x2d, ffn_w_up[0].astype(BF16), ffn_w_down[0].astype(BF16),
                              row(ln2_g[0]), row(ln2_b[0]), kv_args)

    wuq = _pad_heads(b_w_uq[0], MLA_HEADS, MLA_NOPE + MLA_ROPE).astype(BF16)
    wuqs = _pad_heads(_swap_rope_halves(b_w_uq[0], MLA_HEADS), MLA_HEADS,
                      MLA_NOPE + MLA_ROPE).astype(BF16)
    x2d = _mixer_b(x2d, b_w_in[0].astype(BF16), row(b_q_norm_g[0]), wuq, wuqs, ctab, stab,
                   k_all, vt_all, memkv, b_w_out[0].astype(BF16),
                   row(ln1_g[1]), row(ln1_b[1]), batch, seq)
    x2d = _ffn(x2d, ffn_w_up[1].astype(BF16), ffn_w_down[1].astype(BF16),
               row(ln2_g[1]), row(ln2_b[1]))
    return x2d.reshape(batch, seq, D_MODEL)
```

```python
import functools

import jax
import jax.numpy as jnp
from jax import lax
from jax.experimental import pallas as pl
from jax.experimental.pallas import tpu as pltpu

F32 = jnp.float32
BF16 = jnp.bfloat16

D_MODEL = 1024
DEPTH = 2
N_MEM = 256
MLSTM_HEADS = 4
MLSTM_QK = 64
MLSTM_V = 128
MEM_HEADS = 4
MEM_DIM = 128
MLA_HEADS = 8
MLA_NOPE = 64
MLA_ROPE = 32
MLA_V = 64
Q_LORA = 256
KV_LORA = 256
D_FF = 4 * D_MODEL
ROPE_THETA = 10000.0
LN_EPS = 1e-5
RMS_EPS = 1e-6
ALPHA = (2 * DEPTH) ** 0.25
MLA_CHUNK = 64
BF16_SUBLANES = 16
VT_ROWS = MLA_V + BF16_SUBLANES
QK_AHEAD = 2
LOG2_E = 1.4426950408889634

LANES = 128
HEAD_PAD = 128
ROPE_LO = MLA_NOPE
ROPE_HALF = MLA_ROPE // 2

MLSTM_CHUNK = 256
MLSTM_CHUNKS = 2
TS_A = MLSTM_CHUNK * MLSTM_CHUNKS
HEADS_LOCKSTEP = 2
TQ_B = 256
TK_B = 256
MIXB_TILES = 4
TM_FFN = 1024
TM_ROPE = 2048
TM_MEM = 512
FF_CHUNK = 1024
FFN_ROWS = TK_B
VMEM_LIMIT = 56 * 1024 * 1024

NT_DIMS = (((1,), (1,)), ((), ()))
TN_DIMS = (((0,), (0,)), ((), ()))


def _dot(a, b):
    return jnp.dot(a, b, preferred_element_type=F32)


def _dot_nt(a, b):
    return lax.dot_general(a, b, NT_DIMS, preferred_element_type=F32)


def _layer_norm(y, g, b):
    mu = jnp.mean(y, axis=-1, keepdims=True)
    yc = y - mu
    var = jnp.mean(yc * yc, axis=-1, keepdims=True)
    return yc * lax.rsqrt(var + LN_EPS) * g + b


def _log_sigmoid(z):
    return jnp.minimum(z, 0.0) - jnp.log(1.0 + jnp.exp(-jnp.abs(z)))


def _out_proj_norm(x_ref, cat_ref, wout_ref, g_ref, b_ref, o_ref):
    mix = _dot(cat_ref[...], wout_ref[...])
    o_ref[...] = _layer_norm(ALPHA * x_ref[...] + mix, g_ref[...], b_ref[...])


def _const_spec(shape):
    nd = len(shape)
    return pl.BlockSpec(shape, lambda *_: (0,) * nd, pipeline_mode=pl.Buffered(1))


def _rope_table_kernel(pos_ref, invf_ref, c_ref, s_ref):
    ang = invf_ref[...] * pos_ref[...]
    cos = jnp.cos(ang)
    sin = jnp.sin(ang)
    tm = ang.shape[1]
    tail = LANES - ROPE_LO - MLA_ROPE
    ct = jnp.concatenate([jnp.ones((ROPE_LO, tm), F32), cos, cos, jnp.ones((tail, tm), F32)], axis=0)
    st = jnp.concatenate([jnp.zeros((ROPE_LO, tm), F32), -sin, sin, jnp.zeros((tail, tm), F32)], axis=0)
    c_ref[...] = ct.T
    s_ref[...] = st.T


def _rope_tables(pos_row, invf_col):
    t = pos_row.shape[1]
    return pl.pallas_call(
        _rope_table_kernel,
        out_shape=(jax.ShapeDtypeStruct((t, LANES), F32),) * 2,
        grid=(t // TM_ROPE,),
        in_specs=[pl.BlockSpec((1, TM_ROPE), lambda i: (0, i)),
                  _const_spec((ROPE_HALF, 1))],
        out_specs=(pl.BlockSpec((TM_ROPE, LANES), lambda i: (i, 0)),) * 2,
        compiler_params=pltpu.CompilerParams(dimension_semantics=("parallel",)),
        name="rope_tables",
    )(pos_row, invf_col)


def _mem_kv_kernel(mem_ref, w_ref, o_ref):
    o_ref[...] = _dot(mem_ref[...].astype(BF16), w_ref[...]).astype(BF16)


def _mem_kv(mem2d, w):
    r, n = mem2d.shape[0], w.shape[1]
    return pl.pallas_call(
        _mem_kv_kernel,
        out_shape=jax.ShapeDtypeStruct((r, n), BF16),
        grid=(r // TM_MEM,),
        in_specs=[pl.BlockSpec((TM_MEM, D_MODEL), lambda i: (i, 0)),
                  _const_spec(w.shape)],
        out_specs=pl.BlockSpec((TM_MEM, n), lambda i: (i, 0)),
        compiler_params=pltpu.CompilerParams(dimension_semantics=("parallel",),
                                             vmem_limit_bytes=VMEM_LIMIT),
        name="mem_kv",
    )(mem2d, w)


def _memory_scores(q_all, mkv_ref, h):
    q_scale = MEM_DIM ** -0.5 * LOG2_E
    lo = h * MEM_DIM
    qh = (q_all[:, lo:lo + MEM_DIM] * q_scale).astype(BF16)
    return _dot_nt(qh, mkv_ref[:, lo:lo + MEM_DIM])


def _memory_output(s, mkv_ref, cat_ref, col0, h):
    lo = h * MEM_DIM
    v0 = MEM_HEADS * MEM_DIM + lo
    p = jnp.exp2(s - jnp.max(s, axis=-1, keepdims=True)).astype(BF16)
    v_ext = jnp.concatenate([mkv_ref[:, v0:v0 + MEM_DIM], jnp.ones((N_MEM, LANES), BF16)], axis=1)
    o = _dot(p, v_ext)
    cat_ref[:, col0 + lo:col0 + lo + MEM_DIM] = (o[:, :MEM_DIM] / o[:, MEM_DIM:]).astype(BF16)


def _lane_scan(x, combine, fill):
    n = x.shape[1]
    lane = lax.broadcasted_iota(jnp.int32, x.shape, 1)
    d = 1
    while d < n:
        x = combine(x, jnp.where(lane >= d, pltpu.roll(x, d, axis=1), fill))
        d *= 2
    return x


def _mlstm_gates(gate_all, m_prev, n_chunks):
    length = gate_all.shape[1] // n_chunks
    rows = 2 * MLSTM_HEADS
    gate = jnp.concatenate([gate_all[:, c * length:(c + 1) * length] for c in range(n_chunks)],
                           axis=0)
    head_row = lax.broadcasted_iota(jnp.int32, gate.shape, 0) % rows < MLSTM_HEADS
    cum_f = _lane_scan(_log_sigmoid(gate), jnp.add, 0.0)
    a_all = jnp.where(head_row, gate, 0.0)
    b_all = jnp.where(head_row, pltpu.roll(cum_f, gate.shape[0] - MLSTM_HEADS, axis=0), 0.0)
    r_all = a_all - b_all
    mi_all = b_all + _lane_scan(r_all, jnp.maximum, -jnp.inf)
    out = []
    for c in range(n_chunks):
        b, r, m_intra = (t[c * rows:(c + 1) * rows] for t in (b_all, r_all, mi_all))
        g_tot = b[:, length - 1:length]
        m_inter = b + m_prev
        m_t = jnp.maximum(m_inter, m_intra)
        m_new = jnp.maximum(g_tot + m_prev, jnp.max(g_tot + r, axis=1, keepdims=True))
        decay = jnp.exp(g_tot + m_prev - m_new)
        stack = jnp.concatenate([(b - m_t) * LOG2_E, jnp.exp(m_inter - m_t), jnp.exp(-m_t),
                                 jnp.exp(g_tot + r - m_new),
                                 jnp.zeros((LANES - 4 * rows, length), F32)], axis=0)
        out.append((r * LOG2_E, stack.T, decay))
        m_prev = m_new
    return out, m_prev


def _mixer_a_kernel(x_ref, wmain_ref, wgr_ref, bgr_ref, mkv_ref, wout_ref, g_ref, b_ref, o_ref,
                    c_st, m_st, cat_ref):
    hq = MLSTM_HEADS * MLSTM_QK
    hv = MLSTM_HEADS * MLSTM_V
    cl = MLSTM_CHUNK

    @pl.when(pl.program_id(1) == 0)
    def _():
        c_st[...] = jnp.zeros_like(c_st)
        m_st[...] = jnp.zeros_like(m_st)

    causal = (lax.broadcasted_iota(jnp.int32, (cl, cl), 1)
              <= lax.broadcasted_iota(jnp.int32, (cl, cl), 0))
    ones_blk = jnp.ones((cl, LANES), BF16)
    lane_half = lax.broadcasted_iota(jnp.int32, (cl, LANES), 1) // MLSTM_QK

    xbs = [x_ref[c * cl:(c + 1) * cl, :].astype(BF16) for c in range(MLSTM_CHUNKS)]
    gate_all = jnp.concatenate([_dot_nt(wgr_ref[...], xb) for xb in xbs], axis=1) + bgr_ref[...]
    gates, m_st[...] = _mlstm_gates(gate_all, m_st[...], MLSTM_CHUNKS)

    def head_scores(c, proj, h):
        blk = slice((h // 2) * LANES, (h // 2 + 1) * LANES)
        mine = lane_half == h % 2
        q = jnp.where(mine, proj[:, blk], 0.0).astype(BF16)
        k = proj[:, hq:2 * hq][:, blk] * (MLSTM_QK ** -0.5)
        v = proj[:, 2 * hq + h * MLSTM_V:2 * hq + (h + 1) * MLSTM_V].astype(BF16)
        v_ext = jnp.concatenate([v, ones_blk], axis=1)
        return q, k, v_ext, mine, _dot_nt(q, k.astype(BF16))

    def head_pv(c, h, parts):
        r2, cols, _ = gates[c]
        _, _, v_ext, _, qk = parts
        expo = jnp.where(causal, cols[:, h:h + 1] + r2[h:h + 1, :], -jnp.inf)
        p = qk * jnp.exp2(expo)
        return _dot(p.astype(BF16), v_ext)

    def head_finish(c, proj, h, parts, num):
        _, cols, decay = gates[c]
        q, k, v_ext, mine, _ = parts
        rows = slice(c * cl, (c + 1) * cl)
        o_pre = proj[:, 2 * hq + hv + h * MLSTM_V:2 * hq + hv + (h + 1) * MLSTM_V]
        c_prev = c_st[h]
        inter_b = jnp.broadcast_to(cols[:, 8 + h:9 + h], (cl, MLSTM_V))
        einv_b = jnp.broadcast_to(cols[:, 16 + h:17 + h], (cl, MLSTM_V))
        qc = _dot(q, c_prev.astype(BF16))
        nq = num[:, MLSTM_V:] + inter_b * qc[:, MLSTM_V:]
        hh = (num[:, :MLSTM_V] + inter_b * qc[:, :MLSTM_V]) / jnp.maximum(jnp.abs(nq), einv_b)
        hh = hh * jax.nn.sigmoid(o_pre)
        cat_ref[rows, h * MLSTM_V:(h + 1) * MLSTM_V] = hh.astype(BF16)
        kw = jnp.where(mine, k * cols[:, 24 + h:25 + h], 0.0).astype(BF16)
        c_st[h] = decay[h:h + 1, :] * c_prev + lax.dot_general(kw, v_ext, TN_DIMS,
                                                               preferred_element_type=F32)

    chunks = range(MLSTM_CHUNKS)
    projs = [_dot(xbs[c], wmain_ref[...]) for c in chunks]
    for h0 in range(0, MLSTM_HEADS, HEADS_LOCKSTEP):
        group = [(c, h) for h in range(h0, h0 + HEADS_LOCKSTEP) for c in chunks]
        parts = [head_scores(c, projs[c], h) for c, h in group]
        nums = [head_pv(c, h, parts[i]) for i, (c, h) in enumerate(group)]
        for i, (c, h) in enumerate(group):
            head_finish(c, projs[c], h, parts[i], nums[i])
    for c in chunks:
        rows = slice(c * cl, (c + 1) * cl)
        q_mem = projs[c][:, 2 * hq + 2 * hv:]
        for h in range(MEM_HEADS):
            _memory_output(_memory_scores(q_mem, mkv_ref, h), mkv_ref, cat_ref.at[rows], hv, h)
        _out_proj_norm(x_ref.at[rows], cat_ref.at[rows], wout_ref, g_ref, b_ref, o_ref.at[rows])


def _mixer_a(x2d, wmain, wgr, bgr, memkv, wout, g, b, batch, seq):
    ns = seq // TS_A
    width = MLSTM_HEADS * MLSTM_V + MEM_HEADS * MEM_DIM
    return pl.pallas_call(
        _mixer_a_kernel,
        out_shape=jax.ShapeDtypeStruct(x2d.shape, F32),
        grid=(batch, ns),
        in_specs=[pl.BlockSpec((TS_A, D_MODEL), lambda bi, si: (bi * ns + si, 0)),
                  _const_spec(wmain.shape), _const_spec(wgr.shape), _const_spec(bgr.shape),
                  pl.BlockSpec((N_MEM, 2 * MEM_HEADS * MEM_DIM), lambda bi, si: (bi, 0)),
                  _const_spec(wout.shape), _const_spec(g.shape), _const_spec(b.shape)],
        out_specs=pl.BlockSpec((TS_A, D_MODEL), lambda bi, si: (bi * ns + si, 0)),
        scratch_shapes=[pltpu.VMEM((MLSTM_HEADS, LANES, 2 * MLSTM_V), F32),
                        pltpu.VMEM((2 * MLSTM_HEADS, 1), F32),
                        pltpu.VMEM((TS_A, width), BF16)],
        compiler_params=pltpu.CompilerParams(dimension_semantics=("parallel", "arbitrary"),
                                             vmem_limit_bytes=VMEM_LIMIT),
        name="mixer_a",
    )(x2d, wmain, wgr, bgr, memkv, wout, g, b)


def _shared_kv_rows(y, ctab, stab, wd_ref, gk_ref, wuk_ref, wuvt_ref):
    d = _dot(y.astype(BF16), wd_ref[...])
    ckv = d[:, :KV_LORA]
    ckv = ckv * lax.rsqrt(jnp.mean(ckv * ckv, axis=-1, keepdims=True) + RMS_EPS) * gk_ref[...]
    ckv = ckv.astype(BF16)
    k_rope = (d[:, KV_LORA:KV_LORA + LANES] * ctab
              + d[:, KV_LORA + LANES:KV_LORA + 2 * LANES] * stab)
    k_nope = _dot(ckv, wuk_ref[...])
    k = jnp.concatenate([(k_nope[:, h * HEAD_PAD:(h + 1) * HEAD_PAD] + k_rope).astype(BF16)
                         for h in range(MLA_HEADS)], axis=1)
    vt = _dot_nt(wuvt_ref[...], ckv).astype(BF16)
    ones = jnp.ones((BF16_SUBLANES, vt.shape[1]), BF16)
    pieces = []
    for h in range(MLA_HEADS):
        pieces += [vt[h * MLA_V:(h + 1) * MLA_V], ones]
    return k, jnp.concatenate(pieces, axis=0)


def _ffn_rows(x, wup_ref, wdn_ref, g_ref, b_ref):
    xb = x.astype(BF16)
    acc = jnp.zeros(x.shape, F32)
    for j in range(D_FF // FF_CHUNK):
        hid = _dot(xb, wup_ref[:, j * FF_CHUNK:(j + 1) * FF_CHUNK])
        hid = jnp.square(jnp.maximum(hid, 0.0)).astype(BF16)
        acc = acc + _dot(hid, wdn_ref[j * FF_CHUNK:(j + 1) * FF_CHUNK, :])
    return _layer_norm(ALPHA * x + acc, g_ref[...], b_ref[...])


def _ffn_kernel(x_ref, wup_ref, wdn_ref, g_ref, b_ref, o_ref):
    for r in range(x_ref.shape[0] // FFN_ROWS):
        rows = slice(r * FFN_ROWS, (r + 1) * FFN_ROWS)
        o_ref[rows, :] = _ffn_rows(x_ref[rows, :], wup_ref, wdn_ref, g_ref, b_ref)


def _ffn_kv_kernel(x_ref, wup_ref, wdn_ref, g_ref, b_ref, c_ref, s_ref, wd_ref, gk_ref, wuk_ref,
                   wuvt_ref, o_ref, k_ref, vt_ref):
    n_sub = x_ref.shape[0] // FFN_ROWS
    ys = []
    for r in range(n_sub + 1):
        if r < n_sub:
            rows = slice(r * FFN_ROWS, (r + 1) * FFN_ROWS)
            ys.append(_ffn_rows(x_ref[rows, :], wup_ref, wdn_ref, g_ref, b_ref))
            o_ref[rows, :] = ys[r]
        if r > 0:
            prev = slice((r - 1) * FFN_ROWS, r * FFN_ROWS)
            k_ref[prev, :], vt_ref[r - 1] = _shared_kv_rows(
                ys[r - 1], c_ref[prev, :], s_ref[prev, :], wd_ref, gk_ref, wuk_ref, wuvt_ref)


def _ffn(x2d, wup, wdn, g, b, kv_args=None):
    t = x2d.shape[0]
    tile = pl.BlockSpec((TM_FFN, D_MODEL), lambda i: (i, 0))
    in_specs = [tile, _const_spec(wup.shape), _const_spec(wdn.shape),
                _const_spec(g.shape), _const_spec(b.shape)]
    params = pltpu.CompilerParams(dimension_semantics=("parallel",), vmem_limit_bytes=VMEM_LIMIT)
    if kv_args is None:
        return pl.pallas_call(
            _ffn_kernel, out_shape=jax.ShapeDtypeStruct(x2d.shape, F32), grid=(t // TM_FFN,),
            in_specs=in_specs, out_specs=tile, compiler_params=params, name="ffn",
        )(x2d, wup, wdn, g, b)
    kw, vw = MLA_HEADS * HEAD_PAD, MLA_HEADS * VT_ROWS
    table = pl.BlockSpec((TM_FFN, LANES), lambda i: (i, 0))
    in_specs += [table, table] + [_const_spec(w.shape) for w in kv_args[2:]]
    return pl.pallas_call(
        _ffn_kv_kernel,
        out_shape=(jax.ShapeDtypeStruct(x2d.shape, F32), jax.ShapeDtypeStruct((t, kw), BF16),
                   jax.ShapeDtypeStruct((t // TK_B, vw, TK_B), BF16)),
        grid=(t // TM_FFN,), in_specs=in_specs,
        out_specs=(tile, pl.BlockSpec((TM_FFN, kw), lambda i: (i, 0)),
                   pl.BlockSpec((TM_FFN // TK_B, vw, TK_B), lambda i: (i, 0, 0))),
        compiler_params=params, name="ffn_kv",
    )(x2d, wup, wdn, g, b, *kv_args)


def _mixer_b_kernel(x_ref, win_ref, gq_ref, wuq_ref, wuqs_ref, c_ref, s_ref, k_ref, vt_ref,
                    mkv_ref, wout_ref, g_ref, b_ref, o_ref, q_sc, sa_sc, sb_sc, ma_sc, mb_sc, m_sc,
                    acc_sc, ot_sc, cat_ref):
    tq = TQ_B
    step = pl.program_id(1)
    q_scale = (MLA_NOPE + MLA_ROPE) ** -0.5 * LOG2_E
    key_chunk = lax.broadcasted_iota(jnp.int32, (TK_B, tq), 0) // MLA_CHUNK
    qry_chunk = lax.broadcasted_iota(jnp.int32, (TK_B, tq), 1) // MLA_CHUNK
    allowed = key_chunk <= qry_chunk

    def in_proj(rows):
        return _dot(x_ref[rows, :].astype(BF16), win_ref[...])

    def queries(rows, proj):
        cq = proj[:, :Q_LORA]
        cq = cq * lax.rsqrt(jnp.mean(cq * cq, axis=-1, keepdims=True) + RMS_EPS) * gq_ref[...]
        cq = cq.astype(BF16)
        q_lin = _dot(cq, wuq_ref[...])
        q_swp = _dot(cq, wuqs_ref[...])
        ctab = c_ref[rows, :]
        stab = s_ref[rows, :]
        for h in range(MLA_HEADS):
            sl = slice(h * HEAD_PAD, (h + 1) * HEAD_PAD)
            q_sc[rows, sl] = ((q_lin[:, sl] * ctab + q_swp[:, sl] * stab) * q_scale).astype(BF16)

    def attention(rows, idx):
        qi = MIXB_TILES * step + idx
        odd = idx % 2

        def put_scores(dst, j, h):
            kb = k_ref[pl.ds(pl.multiple_of(j * TK_B, TK_B), TK_B), h * HEAD_PAD:(h + 1) * HEAD_PAD]
            s = _dot_nt(kb, q_sc[rows, h * HEAD_PAD:(h + 1) * HEAD_PAD])
            dst[0][h] = s
            dst[1][h] = jnp.max(s, axis=0, keepdims=True)

        def softmax_pv(j, h, src, masked):
            s = src[0][h]
            if masked:
                s = jnp.where(allowed, s, -jnp.inf)
                tile_max = jnp.max(s, axis=0, keepdims=True)
            else:
                tile_max = src[1][h]
            m_old = m_sc[h]
            m_new = jnp.maximum(m_old, tile_max)
            corr = jnp.exp2(m_old - m_new)
            p = jnp.exp2(s - m_new).astype(BF16)
            acc_sc[h] = corr * acc_sc[h] + _dot(vt_ref[j, h * VT_ROWS:(h + 1) * VT_ROWS, :], p)
            m_sc[h] = m_new

        def key_tile(j, src, dst, masked):
            if dst is not None:
                for h in range(QK_AHEAD):
                    put_scores(dst, j + 1, h)
            for h in range(MLA_HEADS):
                softmax_pv(j, h, src, masked)
                if dst is not None and h + QK_AHEAD < MLA_HEADS:
                    put_scores(dst, j + 1, h + QK_AHEAD)

        buf_a, buf_b = (sa_sc, ma_sc), (sb_sc, mb_sc)
        m_sc[...] = jnp.full(m_sc.shape, -jnp.inf, F32)
        acc_sc[...] = jnp.zeros_like(acc_sc)
        for h in range(MLA_HEADS):
            put_scores(buf_a, 0, h)

        def tile_pair(k, carry):
            key_tile(2 * k, buf_a, buf_b, False)
            key_tile(2 * k + 1, buf_b, buf_a, False)
            return carry

        lax.fori_loop(0, qi // 2, tile_pair, 0)
        if odd:
            key_tile(qi - 1, buf_a, buf_b, False)
            key_tile(qi, buf_b, None, True)
        else:
            key_tile(qi, buf_a, None, True)
        for h in range(MLA_HEADS):
            acc = acc_sc[h]
            ot_sc[h * MLA_V:(h + 1) * MLA_V, :] = acc[:MLA_V] / acc[MLA_V:MLA_V + 1]
        cat_ref[rows, :MLA_HEADS * MLA_V] = ot_sc[...].T.astype(BF16)

    tiles = [slice(i * tq, (i + 1) * tq) for i in range(MIXB_TILES)]
    proj = in_proj(tiles[0])
    queries(tiles[0], proj)
    for i, rows in enumerate(tiles):
        attention(rows, i)
        q_mem = proj[:, Q_LORA:]
        mem_scores = [_memory_scores(q_mem, mkv_ref, h) for h in range(MEM_HEADS)]
        if i + 1 < MIXB_TILES:
            proj = in_proj(tiles[i + 1])
        for h in range(MEM_HEADS):
            _memory_output(mem_scores[h], mkv_ref, cat_ref.at[rows], MLA_HEADS * MLA_V, h)
        if i + 1 < MIXB_TILES:
            queries(tiles[i + 1], proj)
        _out_proj_norm(x_ref.at[rows], cat_ref.at[rows], wout_ref, g_ref, b_ref, o_ref.at[rows])


def _mixer_b(x2d, win, gq, wuq, wuqs, ctab, stab, k_all, vt_all, memkv, wout, g, b, batch, seq):
    assert TQ_B == TK_B
    assert MIXB_TILES % 2 == 0
    ts = MIXB_TILES * TQ_B
    ns = seq // ts
    width = MLA_HEADS * MLA_V + MEM_HEADS * MEM_DIM
    tile = lambda bi, si: (bi * ns + si, 0)
    return pl.pallas_call(
        _mixer_b_kernel,
        out_shape=jax.ShapeDtypeStruct(x2d.shape, F32),
        grid=(batch, ns),
        in_specs=[pl.BlockSpec((ts, D_MODEL), tile),
                  _const_spec(win.shape), _const_spec(gq.shape),
                  _const_spec(wuq.shape), _const_spec(wuqs.shape),
                  pl.BlockSpec((ts, LANES), tile), pl.BlockSpec((ts, LANES), tile),
                  pl.BlockSpec((seq, k_all.shape[1]), lambda bi, si: (bi, 0)),
                  pl.BlockSpec((seq // TK_B,) + vt_all.shape[1:], lambda bi, si: (bi, 0, 0)),
                  pl.BlockSpec((N_MEM, 2 * MEM_HEADS * MEM_DIM), lambda bi, si: (bi, 1)),
                  _const_spec(wout.shape), _const_spec(g.shape), _const_spec(b.shape)],
        out_specs=pl.BlockSpec((ts, D_MODEL), tile),
        scratch_shapes=[pltpu.VMEM((ts, MLA_HEADS * HEAD_PAD), BF16),
                        pltpu.VMEM((MLA_HEADS, TK_B, TQ_B), F32),
                        pltpu.VMEM((MLA_HEADS, TK_B, TQ_B), F32),
                        pltpu.VMEM((MLA_HEADS, 1, TQ_B), F32),
                        pltpu.VMEM((MLA_HEADS, 1, TQ_B), F32),
                        pltpu.VMEM((MLA_HEADS, 1, TQ_B), F32),
                        pltpu.VMEM((MLA_HEADS, VT_ROWS, TQ_B), F32),
                        pltpu.VMEM((MLA_HEADS * MLA_V, TQ_B), F32),
                        pltpu.VMEM((ts, width), BF16)],
        compiler_params=pltpu.CompilerParams(dimension_semantics=("parallel", "arbitrary"),
                                             vmem_limit_bytes=VMEM_LIMIT),
        name="mixer_b",
    )(x2d, win, gq, wuq, wuqs, ctab, stab, k_all, vt_all, memkv, wout, g, b)


def _pad_heads(w, heads, dim):
    r = w.shape[0]
    w = w.reshape(r, heads, dim)
    w = jnp.pad(w, ((0, 0), (0, 0), (0, HEAD_PAD - dim)))
    return w.reshape(r, heads * HEAD_PAD)


def _swap_rope_halves(w, heads):
    r = w.shape[0]
    w = w.reshape(r, heads, MLA_NOPE + MLA_ROPE)
    x1 = w[..., MLA_NOPE:MLA_NOPE + ROPE_HALF]
    x2 = w[..., MLA_NOPE + ROPE_HALF:]
    return jnp.concatenate([jnp.zeros_like(w[..., :MLA_NOPE]), x2, x1], axis=-1).reshape(r, -1)


def kernel(x, mem, positions, a_w_in, a_b_igate, a_b_fgate, a_w_mem_kv, a_w_out, kv_w_down, kv_norm_g, kv_w_uk, kv_w_uv, b_w_in, b_q_norm_g, b_w_uq, b_w_mem_kv, b_w_out, ln1_g, ln1_b, ffn_w_up, ffn_w_down, ln2_g, ln2_b):
    batch, seq, _ = x.shape
    t = batch * seq
    x2d = x.reshape(t, D_MODEL)
    row = lambda v: v.reshape(1, -1).astype(F32)

    inv_freq = ROPE_THETA ** (-jnp.arange(0, MLA_ROPE, 2, dtype=F32) / MLA_ROPE)
    ctab, stab = _rope_tables(positions.reshape(1, t).astype(F32), inv_freq.reshape(ROPE_HALF, 1))

    memkv = _mem_kv(mem.reshape(batch * N_MEM, D_MODEL),
                    jnp.concatenate([a_w_mem_kv[0], b_w_mem_kv[0]], axis=1).astype(BF16))

    hq = MLSTM_HEADS * MLSTM_QK
    hv = MLSTM_HEADS * MLSTM_V
    g0 = 2 * hq + 2 * hv
    w_in = a_w_in[0]
    wmain = jnp.concatenate([w_in[:, :g0], w_in[:, g0 + 2 * MLSTM_HEADS:]], axis=1).astype(BF16)
    wgr = w_in[:, g0:g0 + 2 * MLSTM_HEADS].T.astype(BF16)
    bgr = jnp.concatenate([a_b_igate[0], a_b_fgate[0]]).astype(F32).reshape(2 * MLSTM_HEADS, 1)
    x2d = _mixer_a(x2d, wmain, wgr, bgr, memkv, a_w_out[0].astype(BF16),
                   row(ln1_g[0]), row(ln1_b[0]), batch, seq)

    wd = jnp.zeros((D_MODEL, KV_LORA + 2 * LANES), F32)
    wd = wd.at[:, :KV_LORA].set(kv_w_down[:, :KV_LORA])
    r0 = KV_LORA + ROPE_LO
    wd = wd.at[:, r0:r0 + MLA_ROPE].set(kv_w_down[:, KV_LORA:])
    r1 = KV_LORA + LANES + ROPE_LO
    wd = wd.at[:, r1:r1 + ROPE_HALF].set(kv_w_down[:, KV_LORA + ROPE_HALF:])
    wd = wd.at[:, r1 + ROPE_HALF:r1 + MLA_ROPE].set(kv_w_down[:, KV_LORA:KV_LORA + ROPE_HALF])
    kv_args = (ctab, stab, wd.astype(BF16), row(kv_norm_g),
               _pad_heads(kv_w_uk, MLA_HEADS, MLA_NOPE).astype(BF16), kv_w_uv.T.astype(BF16))
    x2d, k_all, vt_all = _ffn(x2d, ffn_w_up[0].astype(BF16), ffn_w_down[0].astype(BF16),
                              row(ln2_g[0]), row(ln2_b[0]), kv_args)

    wuq = _pad_heads(b_w_uq[0], MLA_HEADS, MLA_NOPE + MLA_ROPE).astype(BF16)
    wuqs = _pad_heads(_swap_rope_halves(b_w_uq[0], MLA_HEADS), MLA_HEADS,
                      MLA_NOPE + MLA_ROPE).astype(BF16)
    x2d = _mixer_b(x2d, b_w_in[0].astype(BF16), row(b_q_norm_g[0]), wuq, wuqs, ctab, stab,
                   k_all, vt_all, memkv, b_w_out[0].astype(BF16),
                   row(ln1_g[1]), row(ln1_b[1]), batch, seq)
    x2d = _ffn(x2d, ffn_w_up[1].astype(BF16), ffn_w_down[1].astype(BF16),
               row(ln2_g[1]), row(ln2_b[1]))
    return x2d.reshape(batch, seq, D_MODEL)
```

```python
import jax
import jax.numpy as jnp
from jax import lax
from jax.experimental import pallas as pl
from jax.experimental.pallas import tpu as pltpu

F32 = jnp.float32
BF16 = jnp.bfloat16

D_MODEL = 1024
DEPTH = 2
N_MEM = 256
MLSTM_HEADS = 4
MLSTM_QK = 64
MLSTM_V = 128
MEM_HEADS = 4
MEM_DIM = 128
MLA_HEADS = 8
MLA_NOPE = 64
MLA_ROPE = 32
MLA_V = 64
Q_LORA = 256
KV_LORA = 256
D_FF = 4 * D_MODEL
ROPE_THETA = 10000.0
LN_EPS = 1e-5
RMS_EPS = 1e-6
ALPHA = (2 * DEPTH) ** 0.25
MLA_CHUNK = 64
BF16_SUBLANES = 16
VT_ROWS = MLA_V + BF16_SUBLANES
QK_AHEAD = 2
LOG2_E = 1.4426950408889634

LANES = 128
HEAD_PAD = 128
ROPE_LO = MLA_NOPE
ROPE_HALF = MLA_ROPE // 2

MLSTM_CHUNK = 256
MLSTM_CHUNKS = 2
TS_A = MLSTM_CHUNK * MLSTM_CHUNKS
HEADS_LOCKSTEP = 2
GATE_ROWS = 2 * MLSTM_HEADS
COL_EXPO, COL_INTER, COL_EINV, COL_WGT = (i * GATE_ROWS for i in range(4))
TQ_B = 256
TK_B = 256
MIXB_TILES = 4
TM_FFN = 1024
TM_MEM = 512
FF_CHUNK = 1024
FFN_ROWS = TK_B
V7X_VMEM_BYTES = 64 * 1024 * 1024
VMEM_LIMIT = V7X_VMEM_BYTES - 8 * 1024 * 1024

NT_DIMS = (((1,), (1,)), ((), ()))
TN_DIMS = (((0,), (0,)), ((), ()))


def _dot(a, b):
    return jnp.dot(a, b, preferred_element_type=F32)


def _dot_nt(a, b):
    return lax.dot_general(a, b, NT_DIMS, preferred_element_type=F32)


def _layer_norm(y, g, b):
    mu = jnp.mean(y, axis=-1, keepdims=True)
    yc = y - mu
    var = jnp.mean(yc * yc, axis=-1, keepdims=True)
    return yc * lax.rsqrt(var + LN_EPS) * g + b


def _log_sigmoid(z):
    return jnp.minimum(z, 0.0) - jnp.log(1.0 + jnp.exp(-jnp.abs(z)))


def _out_proj_norm(x_ref, cat_ref, wout_ref, g_ref, b_ref, o_ref):
    mix = _dot(cat_ref[...], wout_ref[...])
    o_ref[...] = _layer_norm(ALPHA * x_ref[...] + mix, g_ref[...], b_ref[...])


def _const_spec(shape):
    nd = len(shape)
    return pl.BlockSpec(shape, lambda *_: (0,) * nd, pipeline_mode=pl.Buffered(1))


def _rope_rows(pos, invf):
    ang = invf * pos
    cos = jnp.cos(ang)
    sin = jnp.sin(ang)
    n = ang.shape[1]
    tail = LANES - ROPE_LO - MLA_ROPE
    ct = jnp.concatenate([jnp.ones((ROPE_LO, n), F32), cos, cos, jnp.ones((tail, n), F32)], axis=0)
    st = jnp.concatenate([jnp.zeros((ROPE_LO, n), F32), -sin, sin, jnp.zeros((tail, n), F32)], axis=0)
    return ct.T, st.T


def _mem_kv_kernel(mem_ref, w_ref, o_ref):
    o_ref[...] = _dot(mem_ref[...].astype(BF16), w_ref[...]).astype(BF16)


def _mem_kv(mem2d, w):
    r, n = mem2d.shape[0], w.shape[1]
    return pl.pallas_call(
        _mem_kv_kernel,
        out_shape=jax.ShapeDtypeStruct((r, n), BF16),
        grid=(r // TM_MEM,),
        in_specs=[pl.BlockSpec((TM_MEM, D_MODEL), lambda i: (i, 0)),
                  _const_spec(w.shape)],
        out_specs=pl.BlockSpec((TM_MEM, n), lambda i: (i, 0)),
        compiler_params=pltpu.CompilerParams(dimension_semantics=("parallel",),
                                             vmem_limit_bytes=VMEM_LIMIT),
        name="mem_kv",
    )(mem2d, w)


def _memory_scores(q_all, mkv_ref, h):
    q_scale = MEM_DIM ** -0.5 * LOG2_E
    lo = h * MEM_DIM
    qh = (q_all[:, lo:lo + MEM_DIM] * q_scale).astype(BF16)
    return _dot_nt(qh, mkv_ref[:, lo:lo + MEM_DIM])


def _memory_output(s, mkv_ref, cat_ref, col0, h):
    lo = h * MEM_DIM
    v0 = MEM_HEADS * MEM_DIM + lo
    p = jnp.exp2(s - jnp.max(s, axis=-1, keepdims=True)).astype(BF16)
    v_ext = jnp.concatenate([mkv_ref[:, v0:v0 + MEM_DIM], jnp.ones((N_MEM, LANES), BF16)], axis=1)
    o = _dot(p, v_ext)
    cat_ref[:, col0 + lo:col0 + lo + MEM_DIM] = (o[:, :MEM_DIM] / o[:, MEM_DIM:]).astype(BF16)


def _lane_scan(x, combine, fill):
    n = x.shape[1]
    lane = lax.broadcasted_iota(jnp.int32, x.shape, 1)
    d = 1
    while d < n:
        x = combine(x, jnp.where(lane >= d, pltpu.roll(x, d, axis=1), fill))
        d *= 2
    return x


def _mlstm_gates(gate_all, m_prev, n_chunks):
    length = gate_all.shape[1] // n_chunks
    rows = GATE_ROWS
    gate = jnp.concatenate([gate_all[:, c * length:(c + 1) * length] for c in range(n_chunks)],
                           axis=0)
    head_row = lax.broadcasted_iota(jnp.int32, gate.shape, 0) % rows < MLSTM_HEADS
    cum_f = _lane_scan(_log_sigmoid(gate), jnp.add, 0.0)
    a_all = jnp.where(head_row, gate, 0.0)
    b_all = jnp.where(head_row, pltpu.roll(cum_f, gate.shape[0] - MLSTM_HEADS, axis=0), 0.0)
    r_all = a_all - b_all
    mi_all = b_all + _lane_scan(r_all, jnp.maximum, -jnp.inf)
    out = []
    for c in range(n_chunks):
        b, r, m_intra = (t[c * rows:(c + 1) * rows] for t in (b_all, r_all, mi_all))
        g_tot = b[:, length - 1:length]
        m_inter = b + m_prev
        m_t = jnp.maximum(m_inter, m_intra)
        m_new = jnp.maximum(g_tot + m_prev, jnp.max(g_tot + r, axis=1, keepdims=True))
        decay = jnp.exp(g_tot + m_prev - m_new)
        stack = jnp.concatenate([(b - m_t) * LOG2_E, jnp.exp(m_inter - m_t), jnp.exp(-m_t),
                                 jnp.exp(g_tot + r - m_new),
                                 jnp.zeros((LANES - 4 * rows, length), F32)], axis=0)
        out.append((r * LOG2_E, stack.T, decay))
        m_prev = m_new
    return out, m_prev


def _mixer_a_kernel(x_ref, wmain_ref, wgr_ref, bgr_ref, mkv_ref, wout_ref, g_ref, b_ref, o_ref,
                    c_st, m_st, cat_ref):
    hq = MLSTM_HEADS * MLSTM_QK
    hv = MLSTM_HEADS * MLSTM_V
    cl = MLSTM_CHUNK

    @pl.when(pl.program_id(1) == 0)
    def _():
        c_st[...] = jnp.zeros_like(c_st)
        m_st[...] = jnp.zeros_like(m_st)

    causal = (lax.broadcasted_iota(jnp.int32, (cl, cl), 1)
              <= lax.broadcasted_iota(jnp.int32, (cl, cl), 0))
    ones_blk = jnp.ones((cl, LANES), BF16)
    lane_half = lax.broadcasted_iota(jnp.int32, (cl, LANES), 1) // MLSTM_QK

    xbs = [x_ref[c * cl:(c + 1) * cl, :].astype(BF16) for c in range(MLSTM_CHUNKS)]
    gate_all = jnp.concatenate([_dot_nt(wgr_ref[...], xb) for xb in xbs], axis=1) + bgr_ref[...]
    gates, m_st[...] = _mlstm_gates(gate_all, m_st[...], MLSTM_CHUNKS)

    def head_scores(c, proj, h):
        blk = slice((h // 2) * LANES, (h // 2 + 1) * LANES)
        mine = lane_half == h % 2
        q = jnp.where(mine, proj[:, blk], 0.0).astype(BF16)
        k = proj[:, hq:2 * hq][:, blk] * (MLSTM_QK ** -0.5)
        v = proj[:, 2 * hq + h * MLSTM_V:2 * hq + (h + 1) * MLSTM_V].astype(BF16)
        v_ext = jnp.concatenate([v, ones_blk], axis=1)
        return q, k, v_ext, mine, _dot_nt(q, k.astype(BF16))

    def head_pv(c, h, parts):
        r2, cols, _ = gates[c]
        _, _, v_ext, _, qk = parts
        expo = jnp.where(causal, cols[:, COL_EXPO + h:COL_EXPO + h + 1] + r2[h:h + 1, :], -jnp.inf)
        p = qk * jnp.exp2(expo)
        return _dot(p.astype(BF16), v_ext)

    def head_finish(c, proj, h, parts, num):
        _, cols, decay = gates[c]
        q, k, v_ext, mine, _ = parts
        rows = slice(c * cl, (c + 1) * cl)
        o_pre = proj[:, 2 * hq + hv + h * MLSTM_V:2 * hq + hv + (h + 1) * MLSTM_V]
        c_prev = c_st[h]
        inter_b = jnp.broadcast_to(cols[:, COL_INTER + h:COL_INTER + h + 1], (cl, MLSTM_V))
        einv_b = jnp.broadcast_to(cols[:, COL_EINV + h:COL_EINV + h + 1], (cl, MLSTM_V))
        qc = _dot(q, c_prev.astype(BF16))
        nq = num[:, MLSTM_V:] + inter_b * qc[:, MLSTM_V:]
        hh = (num[:, :MLSTM_V] + inter_b * qc[:, :MLSTM_V]) / jnp.maximum(jnp.abs(nq), einv_b)
        hh = hh * jax.nn.sigmoid(o_pre)
        cat_ref[rows, h * MLSTM_V:(h + 1) * MLSTM_V] = hh.astype(BF16)
        kw = jnp.where(mine, k * cols[:, COL_WGT + h:COL_WGT + h + 1], 0.0).astype(BF16)
        c_st[h] = decay[h:h + 1, :] * c_prev + lax.dot_general(kw, v_ext, TN_DIMS,
                                                               preferred_element_type=F32)

    chunks = range(MLSTM_CHUNKS)
    projs = [_dot(xbs[c], wmain_ref[...]) for c in chunks]
    for h0 in range(0, MLSTM_HEADS, HEADS_LOCKSTEP):
        group = [(c, h) for h in range(h0, h0 + HEADS_LOCKSTEP) for c in chunks]
        parts = [head_scores(c, projs[c], h) for c, h in group]
        nums = [head_pv(c, h, parts[i]) for i, (c, h) in enumerate(group)]
        for i, (c, h) in enumerate(group):
            head_finish(c, projs[c], h, parts[i], nums[i])
    for c in chunks:
        rows = slice(c * cl, (c + 1) * cl)
        q_mem = projs[c][:, 2 * hq + 2 * hv:]
        for h in range(MEM_HEADS):
            _memory_output(_memory_scores(q_mem, mkv_ref, h), mkv_ref, cat_ref.at[rows], hv, h)
        _out_proj_norm(x_ref.at[rows], cat_ref.at[rows], wout_ref, g_ref, b_ref, o_ref.at[rows])


def _mixer_a(x2d, wmain, wgr, bgr, memkv, wout, g, b, batch, seq):
    ns = seq // TS_A
    width = MLSTM_HEADS * MLSTM_V + MEM_HEADS * MEM_DIM
    return pl.pallas_call(
        _mixer_a_kernel,
        out_shape=jax.ShapeDtypeStruct(x2d.shape, F32),
        grid=(batch, ns),
        in_specs=[pl.BlockSpec((TS_A, D_MODEL), lambda bi, si: (bi * ns + si, 0)),
                  _const_spec(wmain.shape), _const_spec(wgr.shape), _const_spec(bgr.shape),
                  pl.BlockSpec((N_MEM, 2 * MEM_HEADS * MEM_DIM), lambda bi, si: (bi, 0)),
                  _const_spec(wout.shape), _const_spec(g.shape), _const_spec(b.shape)],
        out_specs=pl.BlockSpec((TS_A, D_MODEL), lambda bi, si: (bi * ns + si, 0)),
        scratch_shapes=[pltpu.VMEM((MLSTM_HEADS, LANES, 2 * MLSTM_V), F32),
                        pltpu.VMEM((GATE_ROWS, 1), F32),
                        pltpu.VMEM((TS_A, width), BF16)],
        compiler_params=pltpu.CompilerParams(dimension_semantics=("parallel", "arbitrary"),
                                             vmem_limit_bytes=VMEM_LIMIT),
        name="mixer_a",
    )(x2d, wmain, wgr, bgr, memkv, wout, g, b)


def _shared_kv_rows(y, ctab, stab, wd_ref, gk_ref, wuk_ref, wuvt_ref):
    d = _dot(y.astype(BF16), wd_ref[...])
    ckv = d[:, :KV_LORA]
    ckv = ckv * lax.rsqrt(jnp.mean(ckv * ckv, axis=-1, keepdims=True) + RMS_EPS) * gk_ref[...]
    ckv = ckv.astype(BF16)
    k_rope = (d[:, KV_LORA:KV_LORA + LANES] * ctab
              + d[:, KV_LORA + LANES:KV_LORA + 2 * LANES] * stab)
    k_nope = _dot(ckv, wuk_ref[...])
    k = jnp.concatenate([(k_nope[:, h * HEAD_PAD:(h + 1) * HEAD_PAD] + k_rope).astype(BF16)
                         for h in range(MLA_HEADS)], axis=1)
    vt = _dot_nt(wuvt_ref[...], ckv).astype(BF16)
    ones = jnp.ones((BF16_SUBLANES, vt.shape[1]), BF16)
    pieces = []
    for h in range(MLA_HEADS):
        pieces += [vt[h * MLA_V:(h + 1) * MLA_V], ones]
    return k, jnp.concatenate(pieces, axis=0)


def _ffn_rows(x, wup_ref, wdn_ref, g_ref, b_ref):
    xb = x.astype(BF16)
    acc = jnp.zeros(x.shape, F32)
    for j in range(D_FF // FF_CHUNK):
        hid = _dot(xb, wup_ref[:, j * FF_CHUNK:(j + 1) * FF_CHUNK])
        hid = jnp.square(jnp.maximum(hid, 0.0)).astype(BF16)
        acc = acc + _dot(hid, wdn_ref[j * FF_CHUNK:(j + 1) * FF_CHUNK, :])
    return _layer_norm(ALPHA * x + acc, g_ref[...], b_ref[...])


def _ffn_kernel(x_ref, wup_ref, wdn_ref, g_ref, b_ref, o_ref):
    for r in range(x_ref.shape[0] // FFN_ROWS):
        rows = slice(r * FFN_ROWS, (r + 1) * FFN_ROWS)
        o_ref[rows, :] = _ffn_rows(x_ref[rows, :], wup_ref, wdn_ref, g_ref, b_ref)


def _ffn_kv_kernel(x_ref, wup_ref, wdn_ref, g_ref, b_ref, pos_ref, invf_ref, wd_ref, gk_ref,
                   wuk_ref, wuvt_ref, o_ref, k_ref, vt_ref, c_ref, s_ref):
    n_sub = x_ref.shape[0] // FFN_ROWS
    ys = []
    for r in range(n_sub + 1):
        if r < n_sub:
            rows = slice(r * FFN_ROWS, (r + 1) * FFN_ROWS)
            c_ref[rows, :], s_ref[rows, :] = _rope_rows(pos_ref[:, rows], invf_ref[...])
            ys.append(_ffn_rows(x_ref[rows, :], wup_ref, wdn_ref, g_ref, b_ref))
            o_ref[rows, :] = ys[r]
        if r > 0:
            prev = slice((r - 1) * FFN_ROWS, r * FFN_ROWS)
            k_ref[prev, :], vt_ref[r - 1] = _shared_kv_rows(
                ys[r - 1], c_ref[prev, :], s_ref[prev, :], wd_ref, gk_ref, wuk_ref, wuvt_ref)


def _ffn(x2d, wup, wdn, g, b, kv_args=None):
    t = x2d.shape[0]
    tile = pl.BlockSpec((TM_FFN, D_MODEL), lambda i: (i, 0))
    in_specs = [tile, _const_spec(wup.shape), _const_spec(wdn.shape),
                _const_spec(g.shape), _const_spec(b.shape)]
    params = pltpu.CompilerParams(dimension_semantics=("parallel",), vmem_limit_bytes=VMEM_LIMIT)
    if kv_args is None:
        return pl.pallas_call(
            _ffn_kernel, out_shape=jax.ShapeDtypeStruct(x2d.shape, F32), grid=(t // TM_FFN,),
            in_specs=in_specs, out_specs=tile, compiler_params=params, name="ffn",
        )(x2d, wup, wdn, g, b)
    kw, vw = MLA_HEADS * HEAD_PAD, MLA_HEADS * VT_ROWS
    table = pl.BlockSpec((TM_FFN, LANES), lambda i: (i, 0))
    in_specs += [pl.BlockSpec((1, TM_FFN), lambda i: (0, i))]
    in_specs += [_const_spec(w.shape) for w in kv_args[1:]]
    return pl.pallas_call(
        _ffn_kv_kernel,
        out_shape=(jax.ShapeDtypeStruct(x2d.shape, F32), jax.ShapeDtypeStruct((t, kw), BF16),
                   jax.ShapeDtypeStruct((t // TK_B, vw, TK_B), BF16),
                   jax.ShapeDtypeStruct((t, LANES), F32), jax.ShapeDtypeStruct((t, LANES), F32)),
        grid=(t // TM_FFN,), in_specs=in_specs,
        out_specs=(tile, pl.BlockSpec((TM_FFN, kw), lambda i: (i, 0)),
                   pl.BlockSpec((TM_FFN // TK_B, vw, TK_B), lambda i: (i, 0, 0)), table, table),
        compiler_params=params, name="ffn_kv",
    )(x2d, wup, wdn, g, b, *kv_args)


def _mixer_b_kernel(x_ref, win_ref, gq_ref, wuq_ref, wuqs_ref, c_ref, s_ref, k_ref, vt_ref,
                    mkv_ref, wout_ref, g_ref, b_ref, o_ref, q_sc, sa_sc, sb_sc, m_sc, acc_sc,
                    ot_sc, cat_ref):
    tq = TQ_B
    step = pl.program_id(1)
    q_scale = (MLA_NOPE + MLA_ROPE) ** -0.5 * LOG2_E
    key_chunk = lax.broadcasted_iota(jnp.int32, (TK_B, tq), 0) // MLA_CHUNK
    qry_chunk = lax.broadcasted_iota(jnp.int32, (TK_B, tq), 1) // MLA_CHUNK
    allowed = key_chunk <= qry_chunk

    def in_proj(rows):
        return _dot(x_ref[rows, :].astype(BF16), win_ref[...])

    def queries(rows, proj):
        cq = proj[:, :Q_LORA]
        cq = cq * lax.rsqrt(jnp.mean(cq * cq, axis=-1, keepdims=True) + RMS_EPS) * gq_ref[...]
        cq = cq.astype(BF16)
        q_lin = _dot(cq, wuq_ref[...])
        q_swp = _dot(cq, wuqs_ref[...])
        ctab = c_ref[rows, :]
        stab = s_ref[rows, :]
        for h in range(MLA_HEADS):
            sl = slice(h * HEAD_PAD, (h + 1) * HEAD_PAD)
            q_sc[rows, sl] = ((q_lin[:, sl] * ctab + q_swp[:, sl] * stab) * q_scale).astype(BF16)

    def attention(rows, idx):
        qi = MIXB_TILES * step + idx
        odd = idx % 2

        def scores(j, h):
            kb = k_ref[pl.ds(pl.multiple_of(j * TK_B, TK_B), TK_B), h * HEAD_PAD:(h + 1) * HEAD_PAD]
            return _dot_nt(kb, q_sc[rows, h * HEAD_PAD:(h + 1) * HEAD_PAD])

        def softmax_pv(j, h, src, masked):
            s = src[h]
            if masked:
                s = jnp.where(allowed, s, -jnp.inf)
            m_old = m_sc[h]
            m_new = jnp.maximum(m_old, jnp.max(s, axis=0, keepdims=True))
            corr = jnp.exp2(m_old - m_new)
            p = jnp.exp2(s - m_new).astype(BF16)
            acc_sc[h] = corr * acc_sc[h] + _dot(vt_ref[j, h * VT_ROWS:(h + 1) * VT_ROWS, :], p)
            m_sc[h] = m_new

        def key_tile(j, src, dst, masked):
            if dst is not None:
                for h in range(QK_AHEAD):
                    dst[h] = scores(j + 1, h)
            for h in range(MLA_HEADS):
                softmax_pv(j, h, src, masked)
                if dst is not None and h + QK_AHEAD < MLA_HEADS:
                    dst[h + QK_AHEAD] = scores(j + 1, h + QK_AHEAD)

        m_sc[...] = jnp.full(m_sc.shape, -jnp.inf, F32)
        acc_sc[...] = jnp.zeros_like(acc_sc)
        for h in range(MLA_HEADS):
            sa_sc[h] = scores(0, h)

        def tile_pair(k, carry):
            key_tile(2 * k, sa_sc, sb_sc, False)
            key_tile(2 * k + 1, sb_sc, sa_sc, False)
            return carry

        lax.fori_loop(0, qi // 2, tile_pair, 0)
        if odd:
            key_tile(qi - 1, sa_sc, sb_sc, False)
            key_tile(qi, sb_sc, None, True)
        else:
            key_tile(qi, sa_sc, None, True)
        for h in range(MLA_HEADS):
            acc = acc_sc[h]
            ot_sc[h * MLA_V:(h + 1) * MLA_V, :] = acc[:MLA_V] / acc[MLA_V:MLA_V + 1]
        cat_ref[rows, :MLA_HEADS * MLA_V] = ot_sc[...].T.astype(BF16)

    tiles = [slice(i * tq, (i + 1) * tq) for i in range(MIXB_TILES)]
    proj = in_proj(tiles[0])
    queries(tiles[0], proj)
    for i, rows in enumerate(tiles):
        attention(rows, i)
        q_mem = proj[:, Q_LORA:]
        mem_scores = [_memory_scores(q_mem, mkv_ref, h) for h in range(MEM_HEADS)]
        if i + 1 < MIXB_TILES:
            proj = in_proj(tiles[i + 1])
        for h in range(MEM_HEADS):
            _memory_output(mem_scores[h], mkv_ref, cat_ref.at[rows], MLA_HEADS * MLA_V, h)
        if i + 1 < MIXB_TILES:
            queries(tiles[i + 1], proj)
        _out_proj_norm(x_ref.at[rows], cat_ref.at[rows], wout_ref, g_ref, b_ref, o_ref.at[rows])


def _mixer_b(x2d, win, gq, wuq, wuqs, ctab, stab, k_all, vt_all, memkv, wout, g, b, batch, seq):
    assert TQ_B == TK_B
    assert MIXB_TILES % 2 == 0
    ts = MIXB_TILES * TQ_B
    ns = seq // ts
    width = MLA_HEADS * MLA_V + MEM_HEADS * MEM_DIM
    tile = lambda bi, si: (bi * ns + si, 0)
    return pl.pallas_call(
        _mixer_b_kernel,
        out_shape=jax.ShapeDtypeStruct(x2d.shape, F32),
        grid=(batch, ns),
        in_specs=[pl.BlockSpec((ts, D_MODEL), tile),
                  _const_spec(win.shape), _const_spec(gq.shape),
                  _const_spec(wuq.shape), _const_spec(wuqs.shape),
                  pl.BlockSpec((ts, LANES), tile), pl.BlockSpec((ts, LANES), tile),
                  pl.BlockSpec((seq, k_all.shape[1]), lambda bi, si: (bi, 0)),
                  pl.BlockSpec((seq // TK_B,) + vt_all.shape[1:], lambda bi, si: (bi, 0, 0)),
                  pl.BlockSpec((N_MEM, 2 * MEM_HEADS * MEM_DIM), lambda bi, si: (bi, 1)),
                  _const_spec(wout.shape), _const_spec(g.shape), _const_spec(b.shape)],
        out_specs=pl.BlockSpec((ts, D_MODEL), tile),
        scratch_shapes=[pltpu.VMEM((ts, MLA_HEADS * HEAD_PAD), BF16),
                        pltpu.VMEM((MLA_HEADS, TK_B, TQ_B), F32),
                        pltpu.VMEM((MLA_HEADS, TK_B, TQ_B), F32),
                        pltpu.VMEM((MLA_HEADS, 1, TQ_B), F32),
                        pltpu.VMEM((MLA_HEADS, VT_ROWS, TQ_B), F32),
                        pltpu.VMEM((MLA_HEADS * MLA_V, TQ_B), F32),
                        pltpu.VMEM((ts, width), BF16)],
        compiler_params=pltpu.CompilerParams(dimension_semantics=("parallel", "arbitrary"),
                                             vmem_limit_bytes=VMEM_LIMIT),
        name="mixer_b",
    )(x2d, win, gq, wuq, wuqs, ctab, stab, k_all, vt_all, memkv, wout, g, b)


def _pad_heads(w, heads, dim):
    r = w.shape[0]
    w = w.reshape(r, heads, dim)
    w = jnp.pad(w, ((0, 0), (0, 0), (0, HEAD_PAD - dim)))
    return w.reshape(r, heads * HEAD_PAD)


def _swap_rope_halves(w, heads):
    r = w.shape[0]
    w = w.reshape(r, heads, MLA_NOPE + MLA_ROPE)
    x1 = w[..., MLA_NOPE:MLA_NOPE + ROPE_HALF]
    x2 = w[..., MLA_NOPE + ROPE_HALF:]
    return jnp.concatenate([jnp.zeros_like(w[..., :MLA_NOPE]), x2, x1], axis=-1).reshape(r, -1)


def kernel(x, mem, positions, a_w_in, a_b_igate, a_b_fgate, a_w_mem_kv, a_w_out, kv_w_down, kv_norm_g, kv_w_uk, kv_w_uv, b_w_in, b_q_norm_g, b_w_uq, b_w_mem_kv, b_w_out, ln1_g, ln1_b, ffn_w_up, ffn_w_down, ln2_g, ln2_b):
    batch, seq, _ = x.shape
    t = batch * seq
    x2d = x.reshape(t, D_MODEL)
    row = lambda v: v.reshape(1, -1).astype(F32)

    memkv = _mem_kv(mem.reshape(batch * N_MEM, D_MODEL),
                    jnp.concatenate([a_w_mem_kv[0], b_w_mem_kv[0]], axis=1).astype(BF16))

    hq = MLSTM_HEADS * MLSTM_QK
    hv = MLSTM_HEADS * MLSTM_V
    g0 = 2 * hq + 2 * hv
    w_in = a_w_in[0]
    wmain = jnp.concatenate([w_in[:, :g0], w_in[:, g0 + GATE_ROWS:]], axis=1).astype(BF16)
    wgr = w_in[:, g0:g0 + GATE_ROWS].T.astype(BF16)
    bgr = jnp.concatenate([a_b_igate[0], a_b_fgate[0]]).astype(F32).reshape(GATE_ROWS, 1)
    x2d = _mixer_a(x2d, wmain, wgr, bgr, memkv, a_w_out[0].astype(BF16),
                   row(ln1_g[0]), row(ln1_b[0]), batch, seq)

    inv_freq = ROPE_THETA ** (-jnp.arange(0, MLA_ROPE, 2, dtype=F32) / MLA_ROPE)
    wd = jnp.zeros((D_MODEL, KV_LORA + 2 * LANES), F32)
    wd = wd.at[:, :KV_LORA].set(kv_w_down[:, :KV_LORA])
    r0 = KV_LORA + ROPE_LO
    wd = wd.at[:, r0:r0 + MLA_ROPE].set(kv_w_down[:, KV_LORA:])
    r1 = KV_LORA + LANES + ROPE_LO
    wd = wd.at[:, r1:r1 + ROPE_HALF].set(kv_w_down[:, KV_LORA + ROPE_HALF:])
    wd = wd.at[:, r1 + ROPE_HALF:r1 + MLA_ROPE].set(kv_w_down[:, KV_LORA:KV_LORA + ROPE_HALF])
    kv_args = (positions.reshape(1, t).astype(F32), inv_freq.reshape(ROPE_HALF, 1),
               wd.astype(BF16), row(kv_norm_g),
               _pad_heads(kv_w_uk, MLA_HEADS, MLA_NOPE).astype(BF16), kv_w_uv.T.astype(BF16))
    x2d, k_all, vt_all, ctab, stab = _ffn(x2d, ffn_w_up[0].astype(BF16),
                                          ffn_w_down[0].astype(BF16),
                                          row(ln2_g[0]), row(ln2_b[0]), kv_args)

    wuq = _pad_heads(b_w_uq[0], MLA_HEADS, MLA_NOPE + MLA_ROPE).astype(BF16)
    wuqs = _pad_heads(_swap_rope_halves(b_w_uq[0], MLA_HEADS), MLA_HEADS,
                      MLA_NOPE + MLA_ROPE).astype(BF16)
    x2d = _mixer_b(x2d, b_w_in[0].astype(BF16), row(b_q_norm_g[0]), wuq, wuqs, ctab, stab,
                   k_all, vt_all, memkv, b_w_out[0].astype(BF16),
                   row(ln1_g[1]), row(ln1_b[1]), batch, seq)
    x2d = _ffn(x2d, ffn_w_up[1].astype(BF16), ffn_w_down[1].astype(BF16),
               row(ln2_g[1]), row(ln2_b[1]))
    return x2d.reshape(batch, seq, D_MODEL)
```

```python
import jax
import jax.numpy as jnp
from jax import lax
from jax.experimental import pallas as pl
from jax.experimental.pallas import tpu as pltpu

F32 = jnp.float32
BF16 = jnp.bfloat16

D_MODEL = 1024
DEPTH = 2
N_MEM = 256
MLSTM_HEADS = 4
MLSTM_QK = 64
MLSTM_V = 128
MEM_HEADS = 4
MEM_DIM = 128
MLA_HEADS = 8
MLA_NOPE = 64
MLA_ROPE = 32
MLA_V = 64
Q_LORA = 256
KV_LORA = 256
D_FF = 4 * D_MODEL
ROPE_THETA = 10000.0
LN_EPS = 1e-5
RMS_EPS = 1e-6
ALPHA = (2 * DEPTH) ** 0.25
MLA_CHUNK = 64
BF16_SUBLANES = 16
VT_ROWS = MLA_V + BF16_SUBLANES
QK_AHEAD = 2
LOG2_E = 1.4426950408889634

LANES = 128
HEAD_PAD = 128
ROPE_LO = MLA_NOPE
ROPE_HALF = MLA_ROPE // 2

MLSTM_CHUNK = 256
MLSTM_CHUNKS = 2
TS_A = MLSTM_CHUNK * MLSTM_CHUNKS
HEADS_LOCKSTEP = 2
GATE_ROWS = 2 * MLSTM_HEADS
COL_EXPO, COL_INTER, COL_EINV, COL_WGT = (i * GATE_ROWS for i in range(4))
TQ_B = 256
TK_B = 256
MIXB_TILES = 4
TM_FFN = 1024
TM_MEM = 512
FF_CHUNK = 1024
FFN_ROWS = TK_B
V7X_VMEM_BYTES = 64 * 1024 * 1024
VMEM_LIMIT = V7X_VMEM_BYTES - 8 * 1024 * 1024

NT_DIMS = (((1,), (1,)), ((), ()))
TN_DIMS = (((0,), (0,)), ((), ()))


def _dot(a, b):
    return jnp.dot(a, b, preferred_element_type=F32)


def _dot_nt(a, b):
    return lax.dot_general(a, b, NT_DIMS, preferred_element_type=F32)


def _layer_norm(y, g, b):
    mu = jnp.mean(y, axis=-1, keepdims=True)
    yc = y - mu
    var = jnp.mean(yc * yc, axis=-1, keepdims=True)
    return yc * lax.rsqrt(var + LN_EPS) * g + b


def _log_sigmoid(z):
    return jnp.minimum(z, 0.0) - jnp.log(1.0 + jnp.exp(-jnp.abs(z)))


def _out_proj_norm(x_ref, cat_ref, wout_ref, g_ref, b_ref, o_ref):
    mix = _dot(cat_ref[...], wout_ref[...])
    o_ref[...] = _layer_norm(ALPHA * x_ref[...] + mix, g_ref[...], b_ref[...])


def _const_spec(shape):
    nd = len(shape)
    return pl.BlockSpec(shape, lambda *_: (0,) * nd, pipeline_mode=pl.Buffered(1))


def _rope_rows(pos, invf):
    ang = invf * pos
    cos = jnp.cos(ang)
    sin = jnp.sin(ang)
    n = ang.shape[1]
    tail = LANES - ROPE_LO - MLA_ROPE
    ct = jnp.concatenate([jnp.ones((ROPE_LO, n), F32), cos, cos, jnp.ones((tail, n), F32)], axis=0)
    st = jnp.concatenate([jnp.zeros((ROPE_LO, n), F32), -sin, sin, jnp.zeros((tail, n), F32)], axis=0)
    return ct.T, st.T


def _mem_kv_kernel(mem_ref, w_ref, o_ref):
    o_ref[...] = _dot(mem_ref[...].astype(BF16), w_ref[...]).astype(BF16)


def _mem_kv(mem2d, w):
    r, n = mem2d.shape[0], w.shape[1]
    return pl.pallas_call(
        _mem_kv_kernel,
        out_shape=jax.ShapeDtypeStruct((r, n), BF16),
        grid=(r // TM_MEM,),
        in_specs=[pl.BlockSpec((TM_MEM, D_MODEL), lambda i: (i, 0)),
                  _const_spec(w.shape)],
        out_specs=pl.BlockSpec((TM_MEM, n), lambda i: (i, 0)),
        compiler_params=pltpu.CompilerParams(dimension_semantics=("parallel",),
                                             vmem_limit_bytes=VMEM_LIMIT),
        name="mem_kv",
    )(mem2d, w)


def _memory_scores(q_all, mkv_ref, h):
    q_scale = MEM_DIM ** -0.5 * LOG2_E
    lo = h * MEM_DIM
    qh = (q_all[:, lo:lo + MEM_DIM] * q_scale).astype(BF16)
    return _dot_nt(qh, mkv_ref[:, lo:lo + MEM_DIM])


def _memory_output(s, mkv_ref, cat_ref, col0, h):
    lo = h * MEM_DIM
    v0 = MEM_HEADS * MEM_DIM + lo
    p = jnp.exp2(s - jnp.max(s, axis=-1, keepdims=True)).astype(BF16)
    v_ext = jnp.concatenate([mkv_ref[:, v0:v0 + MEM_DIM], jnp.ones((N_MEM, LANES), BF16)], axis=1)
    o = _dot(p, v_ext)
    cat_ref[:, col0 + lo:col0 + lo + MEM_DIM] = (o[:, :MEM_DIM] / o[:, MEM_DIM:]).astype(BF16)


def _lane_scan(x, combine, fill):
    n = x.shape[1]
    lane = lax.broadcasted_iota(jnp.int32, x.shape, 1)
    d = 1
    while d < n:
        x = combine(x, jnp.where(lane >= d, pltpu.roll(x, d, axis=1), fill))
        d *= 2
    return x


def _mlstm_gates(gate_all, m_prev, n_chunks):
    length = gate_all.shape[1] // n_chunks
    rows = GATE_ROWS
    gate = jnp.concatenate([gate_all[:, c * length:(c + 1) * length] for c in range(n_chunks)],
                           axis=0)
    head_row = lax.broadcasted_iota(jnp.int32, gate.shape, 0) % rows < MLSTM_HEADS
    cum_f = _lane_scan(_log_sigmoid(gate), jnp.add, 0.0)
    a_all = jnp.where(head_row, gate, 0.0)
    b_all = jnp.where(head_row, pltpu.roll(cum_f, gate.shape[0] - MLSTM_HEADS, axis=0), 0.0)
    r_all = a_all - b_all
    mi_all = b_all + _lane_scan(r_all, jnp.maximum, -jnp.inf)
    out = []
    for c in range(n_chunks):
        b, r, m_intra = (t[c * rows:(c + 1) * rows] for t in (b_all, r_all, mi_all))
        g_tot = b[:, length - 1:length]
        m_inter = b + m_prev
        m_t = jnp.maximum(m_inter, m_intra)
        m_new = jnp.maximum(g_tot + m_prev, jnp.max(g_tot + r, axis=1, keepdims=True))
        decay = jnp.exp(g_tot + m_prev - m_new)
        stack = jnp.concatenate([(b - m_t) * LOG2_E, jnp.exp(m_inter - m_t), jnp.exp(-m_t),
                                 jnp.exp(g_tot + r - m_new),
                                 jnp.zeros((LANES - 4 * rows, length), F32)], axis=0)
        out.append((r * LOG2_E, stack.T, decay))
        m_prev = m_new
    return out, m_prev


def _mixer_a_kernel(x_ref, wmain_ref, wgr_ref, bgr_ref, mkv_ref, wout_ref, g_ref, b_ref, o_ref,
                    c_st, m_st, cat_ref):
    hq = MLSTM_HEADS * MLSTM_QK
    hv = MLSTM_HEADS * MLSTM_V
    cl = MLSTM_CHUNK

    @pl.when(pl.program_id(1) == 0)
    def _():
        c_st[...] = jnp.zeros_like(c_st)
        m_st[...] = jnp.zeros_like(m_st)

    causal = (lax.broadcasted_iota(jnp.int32, (cl, cl), 1)
              <= lax.broadcasted_iota(jnp.int32, (cl, cl), 0))
    ones_blk = jnp.ones((cl, LANES), BF16)
    lane_half = lax.broadcasted_iota(jnp.int32, (cl, LANES), 1) // MLSTM_QK

    xbs = [x_ref[c * cl:(c + 1) * cl, :].astype(BF16) for c in range(MLSTM_CHUNKS)]
    gate_all = jnp.concatenate([_dot_nt(wgr_ref[...], xb) for xb in xbs], axis=1) + bgr_ref[...]
    gates, m_st[...] = _mlstm_gates(gate_all, m_st[...], MLSTM_CHUNKS)

    def head_scores(c, proj, h):
        blk = slice((h // 2) * LANES, (h // 2 + 1) * LANES)
        mine = lane_half == h % 2
        q = jnp.where(mine, proj[:, blk], 0.0).astype(BF16)
        k = proj[:, hq:2 * hq][:, blk] * (MLSTM_QK ** -0.5)
        v = proj[:, 2 * hq + h * MLSTM_V:2 * hq + (h + 1) * MLSTM_V].astype(BF16)
        v_ext = jnp.concatenate([v, ones_blk], axis=1)
        return q, k, v_ext, mine, _dot_nt(q, k.astype(BF16))

    def head_pv(c, h, parts):
        r2, cols, _ = gates[c]
        _, _, v_ext, _, qk = parts
        expo = jnp.where(causal, cols[:, COL_EXPO + h:COL_EXPO + h + 1] + r2[h:h + 1, :], -jnp.inf)
        p = qk * jnp.exp2(expo)
        return _dot(p.astype(BF16), v_ext)

    def head_finish(c, proj, h, parts, num):
        _, cols, decay = gates[c]
        q, k, v_ext, mine, _ = parts
        rows = slice(c * cl, (c + 1) * cl)
        o_pre = proj[:, 2 * hq + hv + h * MLSTM_V:2 * hq + hv + (h + 1) * MLSTM_V]
        c_prev = c_st[h]
        inter_b = jnp.broadcast_to(cols[:, COL_INTER + h:COL_INTER + h + 1], (cl, MLSTM_V))
        einv_b = jnp.broadcast_to(cols[:, COL_EINV + h:COL_EINV + h + 1], (cl, MLSTM_V))
        qc = _dot(q, c_prev.astype(BF16))
        nq = num[:, MLSTM_V:] + inter_b * qc[:, MLSTM_V:]
        hh = (num[:, :MLSTM_V] + inter_b * qc[:, :MLSTM_V]) / jnp.maximum(jnp.abs(nq), einv_b)
        hh = hh * jax.nn.sigmoid(o_pre)
        cat_ref[rows, h * MLSTM_V:(h + 1) * MLSTM_V] = hh.astype(BF16)
        kw = jnp.where(mine, k * cols[:, COL_WGT + h:COL_WGT + h + 1], 0.0).astype(BF16)
        c_st[h] = decay[h:h + 1, :] * c_prev + lax.dot_general(kw, v_ext, TN_DIMS,
                                                               preferred_element_type=F32)

    chunks = range(MLSTM_CHUNKS)
    projs = [_dot(xbs[c], wmain_ref[...]) for c in chunks]
    for h0 in range(0, MLSTM_HEADS, HEADS_LOCKSTEP):
        group = [(c, h) for h in range(h0, h0 + HEADS_LOCKSTEP) for c in chunks]
        parts = [head_scores(c, projs[c], h) for c, h in group]
        nums = [head_pv(c, h, parts[i]) for i, (c, h) in enumerate(group)]
        for i, (c, h) in enumerate(group):
            head_finish(c, projs[c], h, parts[i], nums[i])
    for c in chunks:
        rows = slice(c * cl, (c + 1) * cl)
        q_mem = projs[c][:, 2 * hq + 2 * hv:]
        for h in range(MEM_HEADS):
            _memory_output(_memory_scores(q_mem, mkv_ref, h), mkv_ref, cat_ref.at[rows], hv, h)
        _out_proj_norm(x_ref.at[rows], cat_ref.at[rows], wout_ref, g_ref, b_ref, o_ref.at[rows])


def _mixer_a(x2d, wmain, wgr, bgr, memkv, wout, g, b, batch, seq):
    ns = seq // TS_A
    width = MLSTM_HEADS * MLSTM_V + MEM_HEADS * MEM_DIM
    return pl.pallas_call(
        _mixer_a_kernel,
        out_shape=jax.ShapeDtypeStruct(x2d.shape, F32),
        grid=(batch, ns),
        in_specs=[pl.BlockSpec((TS_A, D_MODEL), lambda bi, si: (bi * ns + si, 0)),
                  _const_spec(wmain.shape), _const_spec(wgr.shape), _const_spec(bgr.shape),
                  pl.BlockSpec((N_MEM, 2 * MEM_HEADS * MEM_DIM), lambda bi, si: (bi, 0)),
                  _const_spec(wout.shape), _const_spec(g.shape), _const_spec(b.shape)],
        out_specs=pl.BlockSpec((TS_A, D_MODEL), lambda bi, si: (bi * ns + si, 0)),
        scratch_shapes=[pltpu.VMEM((MLSTM_HEADS, LANES, 2 * MLSTM_V), F32),
                        pltpu.VMEM((GATE_ROWS, 1), F32),
                        pltpu.VMEM((TS_A, width), BF16)],
        compiler_params=pltpu.CompilerParams(dimension_semantics=("parallel", "arbitrary"),
                                             vmem_limit_bytes=VMEM_LIMIT),
        name="mixer_a",
    )(x2d, wmain, wgr, bgr, memkv, wout, g, b)


def _shared_kv_down(y, wd_ref):
    return _dot(y.astype(BF16), wd_ref[...])


def _shared_kv_rows(d, ctab, stab, gk_ref, wuk_ref, wuvt_ref):
    ckv = d[:, :KV_LORA]
    ckv = ckv * lax.rsqrt(jnp.mean(ckv * ckv, axis=-1, keepdims=True) + RMS_EPS) * gk_ref[...]
    ckv = ckv.astype(BF16)
    k_rope = (d[:, KV_LORA:KV_LORA + LANES] * ctab
              + d[:, KV_LORA + LANES:KV_LORA + 2 * LANES] * stab)
    k_nope = _dot(ckv, wuk_ref[...])
    k = jnp.concatenate([(k_nope[:, h * HEAD_PAD:(h + 1) * HEAD_PAD] + k_rope).astype(BF16)
                         for h in range(MLA_HEADS)], axis=1)
    vt = _dot_nt(wuvt_ref[...], ckv).astype(BF16)
    ones = jnp.ones((BF16_SUBLANES, vt.shape[1]), BF16)
    pieces = []
    for h in range(MLA_HEADS):
        pieces += [vt[h * MLA_V:(h + 1) * MLA_V], ones]
    return k, jnp.concatenate(pieces, axis=0)


def _ffn_rows(x, wup_ref, wdn_ref, g_ref, b_ref):
    xb = x.astype(BF16)
    acc = jnp.zeros(x.shape, F32)
    for j in range(D_FF // FF_CHUNK):
        hid = _dot(xb, wup_ref[:, j * FF_CHUNK:(j + 1) * FF_CHUNK])
        hid = jnp.square(jnp.maximum(hid, 0.0)).astype(BF16)
        acc = acc + _dot(hid, wdn_ref[j * FF_CHUNK:(j + 1) * FF_CHUNK, :])
    return _layer_norm(ALPHA * x + acc, g_ref[...], b_ref[...])


def _ffn_kernel(x_ref, wup_ref, wdn_ref, g_ref, b_ref, o_ref):
    for r in range(x_ref.shape[0] // FFN_ROWS):
        rows = slice(r * FFN_ROWS, (r + 1) * FFN_ROWS)
        o_ref[rows, :] = _ffn_rows(x_ref[rows, :], wup_ref, wdn_ref, g_ref, b_ref)


def _ffn_kv_kernel(x_ref, wup_ref, wdn_ref, g_ref, b_ref, pos_ref, invf_ref, wd_ref, gk_ref,
                   wuk_ref, wuvt_ref, o_ref, k_ref, vt_ref, c_ref, s_ref):
    n_sub = x_ref.shape[0] // FFN_ROWS
    ys, downs = [], []
    for r in range(n_sub + 2):
        if r < n_sub:
            rows = slice(r * FFN_ROWS, (r + 1) * FFN_ROWS)
            c_ref[rows, :], s_ref[rows, :] = _rope_rows(pos_ref[:, rows], invf_ref[...])
            ys.append(_ffn_rows(x_ref[rows, :], wup_ref, wdn_ref, g_ref, b_ref))
            o_ref[rows, :] = ys[r]
        if 1 <= r <= n_sub:
            downs.append(_shared_kv_down(ys[r - 1], wd_ref))
        if r >= 2:
            prev = slice((r - 2) * FFN_ROWS, (r - 1) * FFN_ROWS)
            k_ref[prev, :], vt_ref[r - 2] = _shared_kv_rows(
                downs[r - 2], c_ref[prev, :], s_ref[prev, :], gk_ref, wuk_ref, wuvt_ref)


def _ffn(x2d, wup, wdn, g, b, kv_args=None):
    t = x2d.shape[0]
    tile = pl.BlockSpec((TM_FFN, D_MODEL), lambda i: (i, 0))
    in_specs = [tile, _const_spec(wup.shape), _const_spec(wdn.shape),
                _const_spec(g.shape), _const_spec(b.shape)]
    params = pltpu.CompilerParams(dimension_semantics=("parallel",), vmem_limit_bytes=VMEM_LIMIT)
    if kv_args is None:
        return pl.pallas_call(
            _ffn_kernel, out_shape=jax.ShapeDtypeStruct(x2d.shape, F32), grid=(t // TM_FFN,),
            in_specs=in_specs, out_specs=tile, compiler_params=params, name="ffn",
        )(x2d, wup, wdn, g, b)
    kw, vw = MLA_HEADS * HEAD_PAD, MLA_HEADS * VT_ROWS
    table = pl.BlockSpec((TM_FFN, LANES), lambda i: (i, 0))
    in_specs += [pl.BlockSpec((1, TM_FFN), lambda i: (0, i))]
    in_specs += [_const_spec(w.shape) for w in kv_args[1:]]
    return pl.pallas_call(
        _ffn_kv_kernel,
        out_shape=(jax.ShapeDtypeStruct(x2d.shape, F32), jax.ShapeDtypeStruct((t, kw), BF16),
                   jax.ShapeDtypeStruct((t // TK_B, vw, TK_B), BF16),
                   jax.ShapeDtypeStruct((t, LANES), F32), jax.ShapeDtypeStruct((t, LANES), F32)),
        grid=(t // TM_FFN,), in_specs=in_specs,
        out_specs=(tile, pl.BlockSpec((TM_FFN, kw), lambda i: (i, 0)),
                   pl.BlockSpec((TM_FFN // TK_B, vw, TK_B), lambda i: (i, 0, 0)), table, table),
        compiler_params=params, name="ffn_kv",
    )(x2d, wup, wdn, g, b, *kv_args)


def _mixer_b_kernel(x_ref, win_ref, gq_ref, wuq_ref, wuqs_ref, c_ref, s_ref, k_ref, vt_ref,
                    mkv_ref, wout_ref, g_ref, b_ref, o_ref, q_sc, sa_sc, sb_sc, m_sc, acc_sc,
                    ot_sc, cat_ref):
    tq = TQ_B
    step = pl.program_id(1)
    q_scale = (MLA_NOPE + MLA_ROPE) ** -0.5 * LOG2_E
    key_chunk = lax.broadcasted_iota(jnp.int32, (TK_B, tq), 0) // MLA_CHUNK
    qry_chunk = lax.broadcasted_iota(jnp.int32, (TK_B, tq), 1) // MLA_CHUNK
    allowed = key_chunk <= qry_chunk

    def in_proj(rows):
        return _dot(x_ref[rows, :].astype(BF16), win_ref[...])

    def queries(rows, proj):
        cq = proj[:, :Q_LORA]
        cq = cq * lax.rsqrt(jnp.mean(cq * cq, axis=-1, keepdims=True) + RMS_EPS) * gq_ref[...]
        cq = cq.astype(BF16)
        q_lin = _dot(cq, wuq_ref[...])
        q_swp = _dot(cq, wuqs_ref[...])
        ctab = c_ref[rows, :]
        stab = s_ref[rows, :]
        for h in range(MLA_HEADS):
            sl = slice(h * HEAD_PAD, (h + 1) * HEAD_PAD)
            q_sc[rows, sl] = ((q_lin[:, sl] * ctab + q_swp[:, sl] * stab) * q_scale).astype(BF16)

    def attention(rows, idx):
        qi = MIXB_TILES * step + idx
        odd = idx % 2

        def scores(j, h):
            kb = k_ref[pl.ds(pl.multiple_of(j * TK_B, TK_B), TK_B), h * HEAD_PAD:(h + 1) * HEAD_PAD]
            return _dot_nt(kb, q_sc[rows, h * HEAD_PAD:(h + 1) * HEAD_PAD])

        def softmax_pv(j, h, src, masked):
            s = src[h]
            if masked:
                s = jnp.where(allowed, s, -jnp.inf)
            m_old = m_sc[h]
            m_new = jnp.maximum(m_old, jnp.max(s, axis=0, keepdims=True))
            corr = jnp.exp2(m_old - m_new)
            p = jnp.exp2(s - m_new).astype(BF16)
            acc_sc[h] = corr * acc_sc[h] + _dot(vt_ref[j, h * VT_ROWS:(h + 1) * VT_ROWS, :], p)
            m_sc[h] = m_new

        def key_tile(j, src, dst, masked):
            if dst is not None:
                for h in range(QK_AHEAD):
                    dst[h] = scores(j + 1, h)
            for h in range(MLA_HEADS):
                softmax_pv(j, h, src, masked)
                if dst is not None and h + QK_AHEAD < MLA_HEADS:
                    dst[h + QK_AHEAD] = scores(j + 1, h + QK_AHEAD)

        m_sc[...] = jnp.full(m_sc.shape, -jnp.inf, F32)
        acc_sc[...] = jnp.zeros_like(acc_sc)
        for h in range(MLA_HEADS):
            sa_sc[h] = scores(0, h)

        def tile_pair(k, carry):
            key_tile(2 * k, sa_sc, sb_sc, False)
            key_tile(2 * k + 1, sb_sc, sa_sc, False)
            return carry

        lax.fori_loop(0, qi // 2, tile_pair, 0)
        if odd:
            key_tile(qi - 1, sa_sc, sb_sc, False)
            key_tile(qi, sb_sc, None, True)
        else:
            key_tile(qi, sa_sc, None, True)
        for h in range(MLA_HEADS):
            acc = acc_sc[h]
            ot_sc[h * MLA_V:(h + 1) * MLA_V, :] = acc[:MLA_V] / acc[MLA_V:MLA_V + 1]
        cat_ref[rows, :MLA_HEADS * MLA_V] = ot_sc[...].T.astype(BF16)

    tiles = [slice(i * tq, (i + 1) * tq) for i in range(MIXB_TILES)]
    proj = in_proj(tiles[0])
    queries(tiles[0], proj)
    for i, rows in enumerate(tiles):
        attention(rows, i)
        q_mem = proj[:, Q_LORA:]
        mem_scores = [_memory_scores(q_mem, mkv_ref, h) for h in range(MEM_HEADS)]
        if i + 1 < MIXB_TILES:
            proj = in_proj(tiles[i + 1])
        for h in range(MEM_HEADS):
            _memory_output(mem_scores[h], mkv_ref, cat_ref.at[rows], MLA_HEADS * MLA_V, h)
        if i + 1 < MIXB_TILES:
            queries(tiles[i + 1], proj)
        _out_proj_norm(x_ref.at[rows], cat_ref.at[rows], wout_ref, g_ref, b_ref, o_ref.at[rows])


def _mixer_b(x2d, win, gq, wuq, wuqs, ctab, stab, k_all, vt_all, memkv, wout, g, b, batch, seq):
    assert TQ_B == TK_B
    assert MIXB_TILES % 2 == 0
    ts = MIXB_TILES * TQ_B
    ns = seq // ts
    width = MLA_HEADS * MLA_V + MEM_HEADS * MEM_DIM
    tile = lambda bi, si: (bi * ns + si, 0)
    return pl.pallas_call(
        _mixer_b_kernel,
        out_shape=jax.ShapeDtypeStruct(x2d.shape, F32),
        grid=(batch, ns),
        in_specs=[pl.BlockSpec((ts, D_MODEL), tile),
                  _const_spec(win.shape), _const_spec(gq.shape),
                  _const_spec(wuq.shape), _const_spec(wuqs.shape),
                  pl.BlockSpec((ts, LANES), tile), pl.BlockSpec((ts, LANES), tile),
                  pl.BlockSpec((seq, k_all.shape[1]), lambda bi, si: (bi, 0)),
                  pl.BlockSpec((seq // TK_B,) + vt_all.shape[1:], lambda bi, si: (bi, 0, 0)),
                  pl.BlockSpec((N_MEM, 2 * MEM_HEADS * MEM_DIM), lambda bi, si: (bi, 1)),
                  _const_spec(wout.shape), _const_spec(g.shape), _const_spec(b.shape)],
        out_specs=pl.BlockSpec((ts, D_MODEL), tile),
        scratch_shapes=[pltpu.VMEM((ts, MLA_HEADS * HEAD_PAD), BF16),
                        pltpu.VMEM((MLA_HEADS, TK_B, TQ_B), F32),
                        pltpu.VMEM((MLA_HEADS, TK_B, TQ_B), F32),
                        pltpu.VMEM((MLA_HEADS, 1, TQ_B), F32),
                        pltpu.VMEM((MLA_HEADS, VT_ROWS, TQ_B), F32),
                        pltpu.VMEM((MLA_HEADS * MLA_V, TQ_B), F32),
                        pltpu.VMEM((ts, width), BF16)],
        compiler_params=pltpu.CompilerParams(dimension_semantics=("parallel", "arbitrary"),
                                             vmem_limit_bytes=VMEM_LIMIT),
        name="mixer_b",
    )(x2d, win, gq, wuq, wuqs, ctab, stab, k_all, vt_all, memkv, wout, g, b)


def _pad_heads(w, heads, dim):
    r = w.shape[0]
    w = w.reshape(r, heads, dim)
    w = jnp.pad(w, ((0, 0), (0, 0), (0, HEAD_PAD - dim)))
    return w.reshape(r, heads * HEAD_PAD)


def _swap_rope_halves(w, heads):
    r = w.shape[0]
    w = w.reshape(r, heads, MLA_NOPE + MLA_ROPE)
    x1 = w[..., MLA_NOPE:MLA_NOPE + ROPE_HALF]
    x2 = w[..., MLA_NOPE + ROPE_HALF:]
    return jnp.concatenate([jnp.zeros_like(w[..., :MLA_NOPE]), x2, x1], axis=-1).reshape(r, -1)


def kernel(x, mem, positions, a_w_in, a_b_igate, a_b_fgate, a_w_mem_kv, a_w_out, kv_w_down, kv_norm_g, kv_w_uk, kv_w_uv, b_w_in, b_q_norm_g, b_w_uq, b_w_mem_kv, b_w_out, ln1_g, ln1_b, ffn_w_up, ffn_w_down, ln2_g, ln2_b):
    batch, seq, _ = x.shape
    t = batch * seq
    x2d = x.reshape(t, D_MODEL)
    row = lambda v: v.reshape(1, -1).astype(F32)

    memkv = _mem_kv(mem.reshape(batch * N_MEM, D_MODEL),
                    jnp.concatenate([a_w_mem_kv[0], b_w_mem_kv[0]], axis=1).astype(BF16))

    hq = MLSTM_HEADS * MLSTM_QK
    hv = MLSTM_HEADS * MLSTM_V
    g0 = 2 * hq + 2 * hv
    w_in = a_w_in[0]
    wmain = jnp.concatenate([w_in[:, :g0], w_in[:, g0 + GATE_ROWS:]], axis=1).astype(BF16)
    wgr = w_in[:, g0:g0 + GATE_ROWS].T.astype(BF16)
    bgr = jnp.concatenate([a_b_igate[0], a_b_fgate[0]]).astype(F32).reshape(GATE_ROWS, 1)
    x2d = _mixer_a(x2d, wmain, wgr, bgr, memkv, a_w_out[0].astype(BF16),
                   row(ln1_g[0]), row(ln1_b[0]), batch, seq)

    inv_freq = ROPE_THETA ** (-jnp.arange(0, MLA_ROPE, 2, dtype=F32) / MLA_ROPE)
    wd = jnp.zeros((D_MODEL, KV_LORA + 2 * LANES), F32)
    wd = wd.at[:, :KV_LORA].set(kv_w_down[:, :KV_LORA])
    r0 = KV_LORA + ROPE_LO
    wd = wd.at[:, r0:r0 + MLA_ROPE].set(kv_w_down[:, KV_LORA:])
    r1 = KV_LORA + LANES + ROPE_LO
    wd = wd.at[:, r1:r1 + ROPE_HALF].set(kv_w_down[:, KV_LORA + ROPE_HALF:])
    wd = wd.at[:, r1 + ROPE_HALF:r1 + MLA_ROPE].set(kv_w_down[:, KV_LORA:KV_LORA + ROPE_HALF])
    kv_args = (positions.reshape(1, t).astype(F32), inv_freq.reshape(ROPE_HALF, 1),
               wd.astype(BF16), row(kv_norm_g),
               _pad_heads(kv_w_uk, MLA_HEADS, MLA_NOPE).astype(BF16), kv_w_uv.T.astype(BF16))
    x2d, k_all, vt_all, ctab, stab = _ffn(x2d, ffn_w_up[0].astype(BF16),
                                          ffn_w_down[0].astype(BF16),
                                          row(ln2_g[0]), row(ln2_b[0]), kv_args)

    wuq = _pad_heads(b_w_uq[0], MLA_HEADS, MLA_NOPE + MLA_ROPE).astype(BF16)
    wuqs = _pad_heads(_swap_rope_halves(b_w_uq[0], MLA_HEADS), MLA_HEADS,
                      MLA_NOPE + MLA_ROPE).astype(BF16)
    x2d = _mixer_b(x2d, b_w_in[0].astype(BF16), row(b_q_norm_g[0]), wuq, wuqs, ctab, stab,
                   k_all, vt_all, memkv, b_w_out[0].astype(BF16),
                   row(ln1_g[1]), row(ln1_b[1]), batch, seq)
    x2d = _ffn(x2d, ffn_w_up[1].astype(BF16), ffn_w_down[1].astype(BF16),
               row(ln2_g[1]), row(ln2_b[1]))
    return x2d.reshape(batch, seq, D_MODEL)
```

```python
import jax
import jax.numpy as jnp
from jax import lax
from jax.experimental import pallas as pl
from jax.experimental.pallas import tpu as pltpu

F32 = jnp.float32
BF16 = jnp.bfloat16

D_MODEL = 1024
DEPTH = 2
N_MEM = 256
MLSTM_HEADS = 4
MLSTM_QK = 64
MLSTM_V = 128
MEM_HEADS = 4
MEM_DIM = 128
MLA_HEADS = 8
MLA_NOPE = 64
MLA_ROPE = 32
MLA_V = 64
Q_LORA = 256
KV_LORA = 256
D_FF = 4 * D_MODEL
ROPE_THETA = 10000.0
LN_EPS = 1e-5
RMS_EPS = 1e-6
ALPHA = (2 * DEPTH) ** 0.25
MLA_CHUNK = 64
BF16_SUBLANES = 16
VT_ROWS = MLA_V + BF16_SUBLANES
QK_AHEAD = 2
LOG2_E = 1.4426950408889634

LANES = 128
HEAD_PAD = 128
ROPE_LO = MLA_NOPE
ROPE_HALF = MLA_ROPE // 2

MLSTM_CHUNK = 256
MLSTM_CHUNKS = 2
TS_A = MLSTM_CHUNK * MLSTM_CHUNKS
HEADS_LOCKSTEP = 2
GATE_ROWS = 2 * MLSTM_HEADS
COL_EXPO, COL_INTER, COL_EINV, COL_WGT = (i * GATE_ROWS for i in range(4))
TQ_B = 256
TK_B = 256
MIXB_TILES = 4
TM_FFN = 1024
TM_MEM = 512
FF_CHUNK = 1024
FFN_ROWS = TK_B
V7X_VMEM_BYTES = 64 * 1024 * 1024
VMEM_LIMIT = V7X_VMEM_BYTES - 8 * 1024 * 1024

NT_DIMS = (((1,), (1,)), ((), ()))
TN_DIMS = (((0,), (0,)), ((), ()))


def _dot(a, b):
    return jnp.dot(a, b, preferred_element_type=F32)


def _dot_nt(a, b):
    return lax.dot_general(a, b, NT_DIMS, preferred_element_type=F32)


def _layer_norm(y, g, b):
    mu = jnp.mean(y, axis=-1, keepdims=True)
    yc = y - mu
    var = jnp.mean(yc * yc, axis=-1, keepdims=True)
    return yc * lax.rsqrt(var + LN_EPS) * g + b


def _log_sigmoid(z):
    return jnp.minimum(z, 0.0) - jnp.log(1.0 + jnp.exp(-jnp.abs(z)))


def _out_proj_norm(x_ref, cat_ref, wout_ref, g_ref, b_ref, o_ref):
    mix = _dot(cat_ref[...], wout_ref[...])
    o_ref[...] = _layer_norm(ALPHA * x_ref[...] + mix, g_ref[...], b_ref[...])


def _const_spec(shape):
    nd = len(shape)
    return pl.BlockSpec(shape, lambda *_: (0,) * nd, pipeline_mode=pl.Buffered(1))


def _rope_rows(pos, invf):
    ang = invf * pos
    cos = jnp.cos(ang)
    sin = jnp.sin(ang)
    n = ang.shape[1]
    tail = LANES - ROPE_LO - MLA_ROPE
    ct = jnp.concatenate([jnp.ones((ROPE_LO, n), F32), cos, cos, jnp.ones((tail, n), F32)], axis=0)
    st = jnp.concatenate([jnp.zeros((ROPE_LO, n), F32), -sin, sin, jnp.zeros((tail, n), F32)], axis=0)
    return ct.T, st.T


def _mem_kv_kernel(mem_ref, w_ref, o_ref):
    o_ref[...] = _dot(mem_ref[...].astype(BF16), w_ref[...]).astype(BF16)


def _mem_kv(mem2d, w):
    r, n = mem2d.shape[0], w.shape[1]
    return pl.pallas_call(
        _mem_kv_kernel,
        out_shape=jax.ShapeDtypeStruct((r, n), BF16),
        grid=(r // TM_MEM,),
        in_specs=[pl.BlockSpec((TM_MEM, D_MODEL), lambda i: (i, 0)),
                  _const_spec(w.shape)],
        out_specs=pl.BlockSpec((TM_MEM, n), lambda i: (i, 0)),
        compiler_params=pltpu.CompilerParams(dimension_semantics=("parallel",),
                                             vmem_limit_bytes=VMEM_LIMIT),
        name="mem_kv",
    )(mem2d, w)


def _memory_scores(q_all, mkv_ref, h):
    q_scale = MEM_DIM ** -0.5 * LOG2_E
    lo = h * MEM_DIM
    qh = (q_all[:, lo:lo + MEM_DIM] * q_scale).astype(BF16)
    return _dot_nt(qh, mkv_ref[:, lo:lo + MEM_DIM])


def _memory_output(s, mkv_ref, cat_ref, col0, h):
    lo = h * MEM_DIM
    v0 = MEM_HEADS * MEM_DIM + lo
    p = jnp.exp2(s - jnp.max(s, axis=-1, keepdims=True)).astype(BF16)
    v_ext = jnp.concatenate([mkv_ref[:, v0:v0 + MEM_DIM], jnp.ones((N_MEM, LANES), BF16)], axis=1)
    o = _dot(p, v_ext)
    cat_ref[:, col0 + lo:col0 + lo + MEM_DIM] = (o[:, :MEM_DIM] / o[:, MEM_DIM:]).astype(BF16)


def _lane_scan(x, combine, fill):
    n = x.shape[1]
    lane = lax.broadcasted_iota(jnp.int32, x.shape, 1)
    d = 1
    while d < n:
        x = combine(x, jnp.where(lane >= d, pltpu.roll(x, d, axis=1), fill))
        d *= 2
    return x


def _mlstm_gates(gate_all, m_prev, n_chunks):
    length = gate_all.shape[1] // n_chunks
    rows = GATE_ROWS
    gate = jnp.concatenate([gate_all[:, c * length:(c + 1) * length] for c in range(n_chunks)],
                           axis=0)
    head_row = lax.broadcasted_iota(jnp.int32, gate.shape, 0) % rows < MLSTM_HEADS
    cum_f = _lane_scan(_log_sigmoid(gate), jnp.add, 0.0)
    a_all = jnp.where(head_row, gate, 0.0)
    b_all = jnp.where(head_row, pltpu.roll(cum_f, gate.shape[0] - MLSTM_HEADS, axis=0), 0.0)
    r_all = a_all - b_all
    mi_all = b_all + _lane_scan(r_all, jnp.maximum, -jnp.inf)
    out = []
    for c in range(n_chunks):
        b, r, m_intra = (t[c * rows:(c + 1) * rows] for t in (b_all, r_all, mi_all))
        g_tot = b[:, length - 1:length]
        m_inter = b + m_prev
        m_t = jnp.maximum(m_inter, m_intra)
        m_new = jnp.maximum(g_tot + m_prev, jnp.max(g_tot + r, axis=1, keepdims=True))
        decay = jnp.exp(g_tot + m_prev - m_new)
        stack = jnp.concatenate([(b - m_t) * LOG2_E, jnp.exp(m_inter - m_t), jnp.exp(-m_t),
                                 jnp.exp(g_tot + r - m_new),
                                 jnp.zeros((LANES - 4 * rows, length), F32)], axis=0)
        out.append((r * LOG2_E, stack.T, decay))
        m_prev = m_new
    return out, m_prev


def _mixer_a_kernel(x_ref, wmain_ref, wgr_ref, bgr_ref, mkv_ref, wout_ref, g_ref, b_ref, o_ref,
                    c_st, m_st, cat_ref):
    hq = MLSTM_HEADS * MLSTM_QK
    hv = MLSTM_HEADS * MLSTM_V
    cl = MLSTM_CHUNK

    @pl.when(pl.program_id(1) == 0)
    def _():
        c_st[...] = jnp.zeros_like(c_st)
        m_st[...] = jnp.zeros_like(m_st)

    causal = (lax.broadcasted_iota(jnp.int32, (cl, cl), 1)
              <= lax.broadcasted_iota(jnp.int32, (cl, cl), 0))
    ones_blk = jnp.ones((cl, LANES), BF16)
    lane_half = lax.broadcasted_iota(jnp.int32, (cl, LANES), 1) // MLSTM_QK

    xbs = [x_ref[c * cl:(c + 1) * cl, :].astype(BF16) for c in range(MLSTM_CHUNKS)]
    gate_all = jnp.concatenate([_dot_nt(wgr_ref[...], xb) for xb in xbs], axis=1) + bgr_ref[...]
    gates, m_st[...] = _mlstm_gates(gate_all, m_st[...], MLSTM_CHUNKS)

    def head_scores(c, proj, h):
        blk = slice((h // 2) * LANES, (h // 2 + 1) * LANES)
        mine = lane_half == h % 2
        q = jnp.where(mine, proj[:, blk], 0.0).astype(BF16)
        k = proj[:, hq:2 * hq][:, blk] * (MLSTM_QK ** -0.5)
        v = proj[:, 2 * hq + h * MLSTM_V:2 * hq + (h + 1) * MLSTM_V].astype(BF16)
        v_ext = jnp.concatenate([v, ones_blk], axis=1)
        return q, k, v_ext, mine, _dot_nt(q, k.astype(BF16))

    def head_pv(c, h, parts):
        r2, cols, _ = gates[c]
        _, _, v_ext, _, qk = parts
        expo = jnp.where(causal, cols[:, COL_EXPO + h:COL_EXPO + h + 1] + r2[h:h + 1, :], -jnp.inf)
        p = qk * jnp.exp2(expo)
        return _dot(p.astype(BF16), v_ext)

    def head_finish(c, proj, h, parts, num):
        _, cols, decay = gates[c]
        q, k, v_ext, mine, _ = parts
        rows = slice(c * cl, (c + 1) * cl)
        o_pre = proj[:, 2 * hq + hv + h * MLSTM_V:2 * hq + hv + (h + 1) * MLSTM_V]
        c_prev = c_st[h]
        inter_b = jnp.broadcast_to(cols[:, COL_INTER + h:COL_INTER + h + 1], (cl, MLSTM_V))
        einv_b = jnp.broadcast_to(cols[:, COL_EINV + h:COL_EINV + h + 1], (cl, MLSTM_V))
        qc = _dot(q, c_prev.astype(BF16))
        nq = num[:, MLSTM_V:] + inter_b * qc[:, MLSTM_V:]
        hh = (num[:, :MLSTM_V] + inter_b * qc[:, :MLSTM_V]) / jnp.maximum(jnp.abs(nq), einv_b)
        hh = hh * jax.nn.sigmoid(o_pre)
        cat_ref[rows, h * MLSTM_V:(h + 1) * MLSTM_V] = hh.astype(BF16)
        kw = jnp.where(mine, k * cols[:, COL_WGT + h:COL_WGT + h + 1], 0.0).astype(BF16)
        c_st[h] = decay[h:h + 1, :] * c_prev + lax.dot_general(kw, v_ext, TN_DIMS,
                                                               preferred_element_type=F32)

    chunks = range(MLSTM_CHUNKS)
    projs = [_dot(xbs[c], wmain_ref[...]) for c in chunks]
    for h0 in range(0, MLSTM_HEADS, HEADS_LOCKSTEP):
        group = [(c, h) for h in range(h0, h0 + HEADS_LOCKSTEP) for c in chunks]
        parts = [head_scores(c, projs[c], h) for c, h in group]
        nums = [head_pv(c, h, parts[i]) for i, (c, h) in enumerate(group)]
        for i, (c, h) in enumerate(group):
            head_finish(c, projs[c], h, parts[i], nums[i])
    row_slices = [slice(c * cl, (c + 1) * cl) for c in chunks]
    for c in chunks:
        q_mem = projs[c][:, 2 * hq + 2 * hv:]
        for h in range(MEM_HEADS):
            _memory_output(_memory_scores(q_mem, mkv_ref, h), mkv_ref, cat_ref.at[row_slices[c]],
                           hv, h)
    for rows in row_slices:
        _out_proj_norm(x_ref.at[rows], cat_ref.at[rows], wout_ref, g_ref, b_ref, o_ref.at[rows])


def _mixer_a(x2d, wmain, wgr, bgr, memkv, wout, g, b, batch, seq):
    ns = seq // TS_A
    width = MLSTM_HEADS * MLSTM_V + MEM_HEADS * MEM_DIM
    return pl.pallas_call(
        _mixer_a_kernel,
        out_shape=jax.ShapeDtypeStruct(x2d.shape, F32),
        grid=(batch, ns),
        in_specs=[pl.BlockSpec((TS_A, D_MODEL), lambda bi, si: (bi * ns + si, 0)),
                  _const_spec(wmain.shape), _const_spec(wgr.shape), _const_spec(bgr.shape),
                  pl.BlockSpec((N_MEM, 2 * MEM_HEADS * MEM_DIM), lambda bi, si: (bi, 0)),
                  _const_spec(wout.shape), _const_spec(g.shape), _const_spec(b.shape)],
        out_specs=pl.BlockSpec((TS_A, D_MODEL), lambda bi, si: (bi * ns + si, 0)),
        scratch_shapes=[pltpu.VMEM((MLSTM_HEADS, LANES, 2 * MLSTM_V), F32),
                        pltpu.VMEM((GATE_ROWS, 1), F32),
                        pltpu.VMEM((TS_A, width), BF16)],
        compiler_params=pltpu.CompilerParams(dimension_semantics=("parallel", "arbitrary"),
                                             vmem_limit_bytes=VMEM_LIMIT),
        name="mixer_a",
    )(x2d, wmain, wgr, bgr, memkv, wout, g, b)


def _shared_kv_down(y, wd_ref):
    return _dot(y.astype(BF16), wd_ref[...])


def _shared_kv_rows(d, ctab, stab, gk_ref, wuk_ref, wuvt_ref):
    ckv = d[:, :KV_LORA]
    ckv = ckv * lax.rsqrt(jnp.mean(ckv * ckv, axis=-1, keepdims=True) + RMS_EPS) * gk_ref[...]
    ckv = ckv.astype(BF16)
    k_rope = (d[:, KV_LORA:KV_LORA + LANES] * ctab
              + d[:, KV_LORA + LANES:KV_LORA + 2 * LANES] * stab)
    k_nope = _dot(ckv, wuk_ref[...])
    k = jnp.concatenate([(k_nope[:, h * HEAD_PAD:(h + 1) * HEAD_PAD] + k_rope).astype(BF16)
                         for h in range(MLA_HEADS)], axis=1)
    vt = _dot_nt(wuvt_ref[...], ckv).astype(BF16)
    ones = jnp.ones((BF16_SUBLANES, vt.shape[1]), BF16)
    pieces = []
    for h in range(MLA_HEADS):
        pieces += [vt[h * MLA_V:(h + 1) * MLA_V], ones]
    return k, jnp.concatenate(pieces, axis=0)


def _ffn_rows(x, wup_ref, wdn_ref, g_ref, b_ref):
    xb = x.astype(BF16)
    acc = jnp.zeros(x.shape, F32)
    for j in range(D_FF // FF_CHUNK):
        hid = _dot(xb, wup_ref[:, j * FF_CHUNK:(j + 1) * FF_CHUNK])
        hid = jnp.square(jnp.maximum(hid, 0.0)).astype(BF16)
        acc = acc + _dot(hid, wdn_ref[j * FF_CHUNK:(j + 1) * FF_CHUNK, :])
    return _layer_norm(ALPHA * x + acc, g_ref[...], b_ref[...])


def _ffn_kernel(x_ref, wup_ref, wdn_ref, g_ref, b_ref, o_ref):
    for r in range(x_ref.shape[0] // FFN_ROWS):
        rows = slice(r * FFN_ROWS, (r + 1) * FFN_ROWS)
        o_ref[rows, :] = _ffn_rows(x_ref[rows, :], wup_ref, wdn_ref, g_ref, b_ref)


def _ffn_kv_kernel(x_ref, wup_ref, wdn_ref, g_ref, b_ref, pos_ref, invf_ref, wd_ref, gk_ref,
                   wuk_ref, wuvt_ref, o_ref, k_ref, vt_ref, c_ref, s_ref):
    n_sub = x_ref.shape[0] // FFN_ROWS
    ys, downs = [], []
    for r in range(n_sub + 2):
        if r < n_sub:
            rows = slice(r * FFN_ROWS, (r + 1) * FFN_ROWS)
            c_ref[rows, :], s_ref[rows, :] = _rope_rows(pos_ref[:, rows], invf_ref[...])
            ys.append(_ffn_rows(x_ref[rows, :], wup_ref, wdn_ref, g_ref, b_ref))
            o_ref[rows, :] = ys[r]
        if 1 <= r <= n_sub:
            downs.append(_shared_kv_down(ys[r - 1], wd_ref))
        if r >= 2:
            prev = slice((r - 2) * FFN_ROWS, (r - 1) * FFN_ROWS)
            k_ref[prev, :], vt_ref[r - 2] = _shared_kv_rows(
                downs[r - 2], c_ref[prev, :], s_ref[prev, :], gk_ref, wuk_ref, wuvt_ref)


def _ffn(x2d, wup, wdn, g, b, kv_args=None):
    t = x2d.shape[0]
    tile = pl.BlockSpec((TM_FFN, D_MODEL), lambda i: (i, 0))
    in_specs = [tile, _const_spec(wup.shape), _const_spec(wdn.shape),
                _const_spec(g.shape), _const_spec(b.shape)]
    params = pltpu.CompilerParams(dimension_semantics=("parallel",), vmem_limit_bytes=VMEM_LIMIT)
    if kv_args is None:
        return pl.pallas_call(
            _ffn_kernel, out_shape=jax.ShapeDtypeStruct(x2d.shape, F32), grid=(t // TM_FFN,),
            in_specs=in_specs, out_specs=tile, compiler_params=params, name="ffn",
        )(x2d, wup, wdn, g, b)
    kw, vw = MLA_HEADS * HEAD_PAD, MLA_HEADS * VT_ROWS
    table = pl.BlockSpec((TM_FFN, LANES), lambda i: (i, 0))
    in_specs += [pl.BlockSpec((1, TM_FFN), lambda i: (0, i))]
    in_specs += [_const_spec(w.shape) for w in kv_args[1:]]
    return pl.pallas_call(
        _ffn_kv_kernel,
        out_shape=(jax.ShapeDtypeStruct(x2d.shape, F32), jax.ShapeDtypeStruct((t, kw), BF16),
                   jax.ShapeDtypeStruct((t // TK_B, vw, TK_B), BF16),
                   jax.ShapeDtypeStruct((t, LANES), F32), jax.ShapeDtypeStruct((t, LANES), F32)),
        grid=(t // TM_FFN,), in_specs=in_specs,
        out_specs=(tile, pl.BlockSpec((TM_FFN, kw), lambda i: (i, 0)),
                   pl.BlockSpec((TM_FFN // TK_B, vw, TK_B), lambda i: (i, 0, 0)), table, table),
        compiler_params=params, name="ffn_kv",
    )(x2d, wup, wdn, g, b, *kv_args)


def _mixer_b_kernel(x_ref, win_ref, gq_ref, wuq_ref, wuqs_ref, c_ref, s_ref, k_ref, vt_ref,
                    mkv_ref, wout_ref, g_ref, b_ref, o_ref, q_sc, sa_sc, sb_sc, m_sc, acc_sc,
                    ot_sc, cat_ref):
    tq = TQ_B
    step = pl.program_id(1)
    q_scale = (MLA_NOPE + MLA_ROPE) ** -0.5 * LOG2_E
    key_chunk = lax.broadcasted_iota(jnp.int32, (TK_B, tq), 0) // MLA_CHUNK
    qry_chunk = lax.broadcasted_iota(jnp.int32, (TK_B, tq), 1) // MLA_CHUNK
    allowed = key_chunk <= qry_chunk

    def in_proj(rows):
        return _dot(x_ref[rows, :].astype(BF16), win_ref[...])

    def queries(rows, proj):
        cq = proj[:, :Q_LORA]
        cq = cq * lax.rsqrt(jnp.mean(cq * cq, axis=-1, keepdims=True) + RMS_EPS) * gq_ref[...]
        cq = cq.astype(BF16)
        q_lin = _dot(cq, wuq_ref[...])
        q_swp = _dot(cq, wuqs_ref[...])
        ctab = c_ref[rows, :]
        stab = s_ref[rows, :]
        for h in range(MLA_HEADS):
            sl = slice(h * HEAD_PAD, (h + 1) * HEAD_PAD)
            q_sc[rows, sl] = ((q_lin[:, sl] * ctab + q_swp[:, sl] * stab) * q_scale).astype(BF16)

    def attention(rows, idx):
        qi = MIXB_TILES * step + idx
        odd = idx % 2

        def scores(j, h):
            kb = k_ref[pl.ds(pl.multiple_of(j * TK_B, TK_B), TK_B), h * HEAD_PAD:(h + 1) * HEAD_PAD]
            return _dot_nt(kb, q_sc[rows, h * HEAD_PAD:(h + 1) * HEAD_PAD])

        def softmax_pv(j, h, src, masked):
            s = src[h]
            if masked:
                s = jnp.where(allowed, s, -jnp.inf)
            m_old = m_sc[h]
            m_new = jnp.maximum(m_old, jnp.max(s, axis=0, keepdims=True))
            corr = jnp.exp2(m_old - m_new)
            p = jnp.exp2(s - m_new).astype(BF16)
            acc_sc[h] = corr * acc_sc[h] + _dot(vt_ref[j, h * VT_ROWS:(h + 1) * VT_ROWS, :], p)
            m_sc[h] = m_new

        def key_tile(j, src, dst, masked):
            if dst is not None:
                for h in range(QK_AHEAD):
                    dst[h] = scores(j + 1, h)
            for h in range(MLA_HEADS):
                softmax_pv(j, h, src, masked)
                if dst is not None and h + QK_AHEAD < MLA_HEADS:
                    dst[h + QK_AHEAD] = scores(j + 1, h + QK_AHEAD)

        m_sc[...] = jnp.full(m_sc.shape, -jnp.inf, F32)
        acc_sc[...] = jnp.zeros_like(acc_sc)
        for h in range(MLA_HEADS):
            sa_sc[h] = scores(0, h)

        def tile_pair(k, carry):
            key_tile(2 * k, sa_sc, sb_sc, False)
            key_tile(2 * k + 1, sb_sc, sa_sc, False)
            return carry

        lax.fori_loop(0, qi // 2, tile_pair, 0)
        if odd:
            key_tile(qi - 1, sa_sc, sb_sc, False)
            key_tile(qi, sb_sc, None, True)
        else:
            key_tile(qi, sa_sc, None, True)
        for h in range(MLA_HEADS):
            acc = acc_sc[h]
            ot_sc[h * MLA_V:(h + 1) * MLA_V, :] = acc[:MLA_V] / acc[MLA_V:MLA_V + 1]
        cat_ref[rows, :MLA_HEADS * MLA_V] = ot_sc[...].T.astype(BF16)

    tiles = [slice(i * tq, (i + 1) * tq) for i in range(MIXB_TILES)]
    proj = in_proj(tiles[0])
    queries(tiles[0], proj)
    for i, rows in enumerate(tiles):
        attention(rows, i)
        q_mem = proj[:, Q_LORA:]
        mem_scores = [_memory_scores(q_mem, mkv_ref, h) for h in range(MEM_HEADS)]
        if i + 1 < MIXB_TILES:
            proj = in_proj(tiles[i + 1])
        for h in range(MEM_HEADS):
            _memory_output(mem_scores[h], mkv_ref, cat_ref.at[rows], MLA_HEADS * MLA_V, h)
        if i + 1 < MIXB_TILES:
            queries(tiles[i + 1], proj)
        _out_proj_norm(x_ref.at[rows], cat_ref.at[rows], wout_ref, g_ref, b_ref, o_ref.at[rows])


def _mixer_b(x2d, win, gq, wuq, wuqs, ctab, stab, k_all, vt_all, memkv, wout, g, b, batch, seq):
    assert TQ_B == TK_B
    assert MIXB_TILES % 2 == 0
    ts = MIXB_TILES * TQ_B
    ns = seq // ts
    width = MLA_HEADS * MLA_V + MEM_HEADS * MEM_DIM
    tile = lambda bi, si: (bi * ns + si, 0)
    return pl.pallas_call(
        _mixer_b_kernel,
        out_shape=jax.ShapeDtypeStruct(x2d.shape, F32),
        grid=(batch, ns),
        in_specs=[pl.BlockSpec((ts, D_MODEL), tile),
                  _const_spec(win.shape), _const_spec(gq.shape),
                  _const_spec(wuq.shape), _const_spec(wuqs.shape),
                  pl.BlockSpec((ts, LANES), tile), pl.BlockSpec((ts, LANES), tile),
                  pl.BlockSpec((seq, k_all.shape[1]), lambda bi, si: (bi, 0)),
                  pl.BlockSpec((seq // TK_B,) + vt_all.shape[1:], lambda bi, si: (bi, 0, 0)),
                  pl.BlockSpec((N_MEM, 2 * MEM_HEADS * MEM_DIM), lambda bi, si: (bi, 1)),
                  _const_spec(wout.shape), _const_spec(g.shape), _const_spec(b.shape)],
        out_specs=pl.BlockSpec((ts, D_MODEL), tile),
        scratch_shapes=[pltpu.VMEM((ts, MLA_HEADS * HEAD_PAD), BF16),
                        pltpu.VMEM((MLA_HEADS, TK_B, TQ_B), F32),
                        pltpu.VMEM((MLA_HEADS, TK_B, TQ_B), F32),
                        pltpu.VMEM((MLA_HEADS, 1, TQ_B), F32),
                        pltpu.VMEM((MLA_HEADS, VT_ROWS, TQ_B), F32),
                        pltpu.VMEM((MLA_HEADS * MLA_V, TQ_B), F32),
                        pltpu.VMEM((ts, width), BF16)],
        compiler_params=pltpu.CompilerParams(dimension_semantics=("parallel", "arbitrary"),
                                             vmem_limit_bytes=VMEM_LIMIT),
        name="mixer_b",
    )(x2d, win, gq, wuq, wuqs, ctab, stab, k_all, vt_all, memkv, wout, g, b)


def _pad_heads(w, heads, dim):
    r = w.shape[0]
    w = w.reshape(r, heads, dim)
    w = jnp.pad(w, ((0, 0), (0, 0), (0, HEAD_PAD - dim)))
    return w.reshape(r, heads * HEAD_PAD)


def _swap_rope_halves(w, heads):
    r = w.shape[0]
    w = w.reshape(r, heads, MLA_NOPE + MLA_ROPE)
    x1 = w[..., MLA_NOPE:MLA_NOPE + ROPE_HALF]
    x2 = w[..., MLA_NOPE + ROPE_HALF:]
    return jnp.concatenate([jnp.zeros_like(w[..., :MLA_NOPE]), x2, x1], axis=-1).reshape(r, -1)


def kernel(x, mem, positions, a_w_in, a_b_igate, a_b_fgate, a_w_mem_kv, a_w_out, kv_w_down, kv_norm_g, kv_w_uk, kv_w_uv, b_w_in, b_q_norm_g, b_w_uq, b_w_mem_kv, b_w_out, ln1_g, ln1_b, ffn_w_up, ffn_w_down, ln2_g, ln2_b):
    batch, seq, _ = x.shape
    t = batch * seq
    x2d = x.reshape(t, D_MODEL)
    row = lambda v: v.reshape(1, -1).astype(F32)

    memkv = _mem_kv(mem.reshape(batch * N_MEM, D_MODEL),
                    jnp.concatenate([a_w_mem_kv[0], b_w_mem_kv[0]], axis=1).astype(BF16))

    hq = MLSTM_HEADS * MLSTM_QK
    hv = MLSTM_HEADS * MLSTM_V
    g0 = 2 * hq + 2 * hv
    w_in = a_w_in[0]
    wmain = jnp.concatenate([w_in[:, :g0], w_in[:, g0 + GATE_ROWS:]], axis=1).astype(BF16)
    wgr = w_in[:, g0:g0 + GATE_ROWS].T.astype(BF16)
    bgr = jnp.concatenate([a_b_igate[0], a_b_fgate[0]]).astype(F32).reshape(GATE_ROWS, 1)
    x2d = _mixer_a(x2d, wmain, wgr, bgr, memkv, a_w_out[0].astype(BF16),
                   row(ln1_g[0]), row(ln1_b[0]), batch, seq)

    inv_freq = ROPE_THETA ** (-jnp.arange(0, MLA_ROPE, 2, dtype=F32) / MLA_ROPE)
    wd = jnp.zeros((D_MODEL, KV_LORA + 2 * LANES), F32)
    wd = wd.at[:, :KV_LORA].set(kv_w_down[:, :KV_LORA])
    r0 = KV_LORA + ROPE_LO
    wd = wd.at[:, r0:r0 + MLA_ROPE].set(kv_w_down[:, KV_LORA:])
    r1 = KV_LORA + LANES + ROPE_LO
    wd = wd.at[:, r1:r1 + ROPE_HALF].set(kv_w_down[:, KV_LORA + ROPE_HALF:])
    wd = wd.at[:, r1 + ROPE_HALF:r1 + MLA_ROPE].set(kv_w_down[:, KV_LORA:KV_LORA + ROPE_HALF])
    kv_args = (positions.reshape(1, t).astype(F32), inv_freq.reshape(ROPE_HALF, 1),
               wd.astype(BF16), row(kv_norm_g),
               _pad_heads(kv_w_uk, MLA_HEADS, MLA_NOPE).astype(BF16), kv_w_uv.T.astype(BF16))
    x2d, k_all, vt_all, ctab, stab = _ffn(x2d, ffn_w_up[0].astype(BF16),
                                          ffn_w_down[0].astype(BF16),
                                          row(ln2_g[0]), row(ln2_b[0]), kv_args)

    wuq = _pad_heads(b_w_uq[0], MLA_HEADS, MLA_NOPE + MLA_ROPE).astype(BF16)
    wuqs = _pad_heads(_swap_rope_halves(b_w_uq[0], MLA_HEADS), MLA_HEADS,
                      MLA_NOPE + MLA_ROPE).astype(BF16)
    x2d = _mixer_b(x2d, b_w_in[0].astype(BF16), row(b_q_norm_g[0]), wuq, wuqs, ctab, stab,
                   k_all, vt_all, memkv, b_w_out[0].astype(BF16),
                   row(ln1_g[1]), row(ln1_b[1]), batch, seq)
    x2d = _ffn(x2d, ffn_w_up[1].astype(BF16), ffn_w_down[1].astype(BF16),
               row(ln2_g[1]), row(ln2_b[1]))
    return x2d.reshape(batch, seq, D_MODEL)
```

```python
import jax
import jax.numpy as jnp
from jax import lax
from jax.experimental import pallas as pl
from jax.experimental.pallas import tpu as pltpu

F32 = jnp.float32
BF16 = jnp.bfloat16

D_MODEL = 1024
DEPTH = 2
N_MEM = 256
MLSTM_HEADS = 4
MLSTM_QK = 64
MLSTM_V = 128
MEM_HEADS = 4
MEM_DIM = 128
MLA_HEADS = 8
MLA_NOPE = 64
MLA_ROPE = 32
MLA_V = 64
Q_LORA = 256
KV_LORA = 256
D_FF = 4 * D_MODEL
ROPE_THETA = 10000.0
LN_EPS = 1e-5
RMS_EPS = 1e-6
ALPHA = (2 * DEPTH) ** 0.25
MLA_CHUNK = 64
BF16_SUBLANES = 16
VT_ROWS = MLA_V + BF16_SUBLANES
QK_AHEAD = 2
LOG2_E = 1.4426950408889634

LANES = 128
HEAD_PAD = 128
ROPE_LO = MLA_NOPE
ROPE_HALF = MLA_ROPE // 2

MLSTM_CHUNK = 256
MLSTM_CHUNKS = 2
TS_A = MLSTM_CHUNK * MLSTM_CHUNKS
HEADS_LOCKSTEP = 2
GATE_ROWS = 2 * MLSTM_HEADS
COL_EXPO, COL_INTER, COL_EINV, COL_WGT = (i * GATE_ROWS for i in range(4))
TQ_B = 256
TK_B = 256
MIXB_TILES = 4
TM_FFN = 1024
TM_MEM = 512
FF_CHUNK = 2048
FFN_ROWS = TK_B
V7X_VMEM_BYTES = 64 * 1024 * 1024
VMEM_LIMIT = V7X_VMEM_BYTES - 8 * 1024 * 1024

NT_DIMS = (((1,), (1,)), ((), ()))
TN_DIMS = (((0,), (0,)), ((), ()))


def _dot(a, b):
    return jnp.dot(a, b, preferred_element_type=F32)


def _dot_nt(a, b):
    return lax.dot_general(a, b, NT_DIMS, preferred_element_type=F32)


def _layer_norm(y, g, b):
    mu = jnp.mean(y, axis=-1, keepdims=True)
    yc = y - mu
    var = jnp.mean(yc * yc, axis=-1, keepdims=True)
    return yc * lax.rsqrt(var + LN_EPS) * g + b


def _log_sigmoid(z):
    return jnp.minimum(z, 0.0) - jnp.log(1.0 + jnp.exp(-jnp.abs(z)))


def _out_proj_norm(x_ref, cat_ref, wout_ref, g_ref, b_ref, o_ref):
    mix = _dot(cat_ref[...], wout_ref[...])
    o_ref[...] = _layer_norm(ALPHA * x_ref[...] + mix, g_ref[...], b_ref[...])


def _const_spec(shape):
    nd = len(shape)
    return pl.BlockSpec(shape, lambda *_: (0,) * nd, pipeline_mode=pl.Buffered(1))


def _rope_rows(pos, invf):
    ang = invf * pos
    cos = jnp.cos(ang)
    sin = jnp.sin(ang)
    n = ang.shape[1]
    tail = LANES - ROPE_LO - MLA_ROPE
    ct = jnp.concatenate([jnp.ones((ROPE_LO, n), F32), cos, cos, jnp.ones((tail, n), F32)], axis=0)
    st = jnp.concatenate([jnp.zeros((ROPE_LO, n), F32), -sin, sin, jnp.zeros((tail, n), F32)], axis=0)
    return ct.T, st.T


def _mem_kv_kernel(mem_ref, w_ref, o_ref):
    o_ref[...] = _dot(mem_ref[...].astype(BF16), w_ref[...]).astype(BF16)


def _mem_kv(mem2d, w):
    r, n = mem2d.shape[0], w.shape[1]
    return pl.pallas_call(
        _mem_kv_kernel,
        out_shape=jax.ShapeDtypeStruct((r, n), BF16),
        grid=(r // TM_MEM,),
        in_specs=[pl.BlockSpec((TM_MEM, D_MODEL), lambda i: (i, 0)),
                  _const_spec(w.shape)],
        out_specs=pl.BlockSpec((TM_MEM, n), lambda i: (i, 0)),
        compiler_params=pltpu.CompilerParams(dimension_semantics=("parallel",),
                                             vmem_limit_bytes=VMEM_LIMIT),
        name="mem_kv",
    )(mem2d, w)


def _memory_scores(q_all, mkv_ref, h):
    q_scale = MEM_DIM ** -0.5 * LOG2_E
    lo = h * MEM_DIM
    qh = (q_all[:, lo:lo + MEM_DIM] * q_scale).astype(BF16)
    return _dot_nt(qh, mkv_ref[:, lo:lo + MEM_DIM])


def _memory_output(s, mkv_ref, cat_ref, col0, h):
    lo = h * MEM_DIM
    v0 = MEM_HEADS * MEM_DIM + lo
    p = jnp.exp2(s - jnp.max(s, axis=-1, keepdims=True)).astype(BF16)
    v_ext = jnp.concatenate([mkv_ref[:, v0:v0 + MEM_DIM], jnp.ones((N_MEM, LANES), BF16)], axis=1)
    o = _dot(p, v_ext)
    cat_ref[:, col0 + lo:col0 + lo + MEM_DIM] = (o[:, :MEM_DIM] / o[:, MEM_DIM:]).astype(BF16)


def _lane_scan(x, combine, fill):
    n = x.shape[1]
    lane = lax.broadcasted_iota(jnp.int32, x.shape, 1)
    d = 1
    while d < n:
        x = combine(x, jnp.where(lane >= d, pltpu.roll(x, d, axis=1), fill))
        d *= 2
    return x


def _mlstm_gates(gate_all, m_prev, n_chunks):
    length = gate_all.shape[1] // n_chunks
    rows = GATE_ROWS
    gate = jnp.concatenate([gate_all[:, c * length:(c + 1) * length] for c in range(n_chunks)],
                           axis=0)
    head_row = lax.broadcasted_iota(jnp.int32, gate.shape, 0) % rows < MLSTM_HEADS
    cum_f = _lane_scan(_log_sigmoid(gate), jnp.add, 0.0)
    a_all = jnp.where(head_row, gate, 0.0)
    b_all = jnp.where(head_row, pltpu.roll(cum_f, gate.shape[0] - MLSTM_HEADS, axis=0), 0.0)
    r_all = a_all - b_all
    mi_all = b_all + _lane_scan(r_all, jnp.maximum, -jnp.inf)
    out = []
    for c in range(n_chunks):
        b, r, m_intra = (t[c * rows:(c + 1) * rows] for t in (b_all, r_all, mi_all))
        g_tot = b[:, length - 1:length]
        m_inter = b + m_prev
        m_t = jnp.maximum(m_inter, m_intra)
        m_new = jnp.maximum(g_tot + m_prev, jnp.max(g_tot + r, axis=1, keepdims=True))
        decay = jnp.exp(g_tot + m_prev - m_new)
        stack = jnp.concatenate([(b - m_t) * LOG2_E, jnp.exp(m_inter - m_t), jnp.exp(-m_t),
                                 jnp.exp(g_tot + r - m_new),
                                 jnp.zeros((LANES - 4 * rows, length), F32)], axis=0)
        out.append((r * LOG2_E, stack.T, decay))
        m_prev = m_new
    return out, m_prev


def _mixer_a_kernel(x_ref, wmain_ref, wgr_ref, bgr_ref, mkv_ref, wout_ref, g_ref, b_ref, o_ref,
                    c_st, m_st, cat_ref):
    hq = MLSTM_HEADS * MLSTM_QK
    hv = MLSTM_HEADS * MLSTM_V
    cl = MLSTM_CHUNK

    @pl.when(pl.program_id(1) == 0)
    def _():
        c_st[...] = jnp.zeros_like(c_st)
        m_st[...] = jnp.zeros_like(m_st)

    causal = (lax.broadcasted_iota(jnp.int32, (cl, cl), 1)
              <= lax.broadcasted_iota(jnp.int32, (cl, cl), 0))
    ones_blk = jnp.ones((cl, LANES), BF16)
    lane_half = lax.broadcasted_iota(jnp.int32, (cl, LANES), 1) // MLSTM_QK

    xbs = [x_ref[c * cl:(c + 1) * cl, :].astype(BF16) for c in range(MLSTM_CHUNKS)]
    gate_all = jnp.concatenate([_dot_nt(wgr_ref[...], xb) for xb in xbs], axis=1) + bgr_ref[...]
    gates, m_st[...] = _mlstm_gates(gate_all, m_st[...], MLSTM_CHUNKS)

    def head_scores(c, proj, h):
        blk = slice((h // 2) * LANES, (h // 2 + 1) * LANES)
        mine = lane_half == h % 2
        q = jnp.where(mine, proj[:, blk], 0.0).astype(BF16)
        k = proj[:, hq:2 * hq][:, blk] * (MLSTM_QK ** -0.5)
        v = proj[:, 2 * hq + h * MLSTM_V:2 * hq + (h + 1) * MLSTM_V].astype(BF16)
        v_ext = jnp.concatenate([v, ones_blk], axis=1)
        return q, k, v_ext, mine, _dot_nt(q, k.astype(BF16))

    def head_pv(c, h, parts):
        r2, cols, _ = gates[c]
        _, _, v_ext, _, qk = parts
        expo = jnp.where(causal, cols[:, COL_EXPO + h:COL_EXPO + h + 1] + r2[h:h + 1, :], -jnp.inf)
        p = qk * jnp.exp2(expo)
        return _dot(p.astype(BF16), v_ext)

    def head_finish(c, proj, h, parts, num):
        _, cols, decay = gates[c]
        q, k, v_ext, mine, _ = parts
        rows = slice(c * cl, (c + 1) * cl)
        o_pre = proj[:, 2 * hq + hv + h * MLSTM_V:2 * hq + hv + (h + 1) * MLSTM_V]
        c_prev = c_st[h]
        inter_b = jnp.broadcast_to(cols[:, COL_INTER + h:COL_INTER + h + 1], (cl, MLSTM_V))
        einv_b = jnp.broadcast_to(cols[:, COL_EINV + h:COL_EINV + h + 1], (cl, MLSTM_V))
        qc = _dot(q, c_prev.astype(BF16))
        nq = num[:, MLSTM_V:] + inter_b * qc[:, MLSTM_V:]
        hh = (num[:, :MLSTM_V] + inter_b * qc[:, :MLSTM_V]) / jnp.maximum(jnp.abs(nq), einv_b)
        hh = hh * jax.nn.sigmoid(o_pre)
        cat_ref[rows, h * MLSTM_V:(h + 1) * MLSTM_V] = hh.astype(BF16)
        kw = jnp.where(mine, k * cols[:, COL_WGT + h:COL_WGT + h + 1], 0.0).astype(BF16)
        c_st[h] = decay[h:h + 1, :] * c_prev + lax.dot_general(kw, v_ext, TN_DIMS,
                                                               preferred_element_type=F32)

    chunks = range(MLSTM_CHUNKS)
    projs = [_dot(xbs[c], wmain_ref[...]) for c in chunks]
    for h0 in range(0, MLSTM_HEADS, HEADS_LOCKSTEP):
        group = [(c, h) for h in range(h0, h0 + HEADS_LOCKSTEP) for c in chunks]
        parts = [head_scores(c, projs[c], h) for c, h in group]
        nums = [head_pv(c, h, parts[i]) for i, (c, h) in enumerate(group)]
        for i, (c, h) in enumerate(group):
            head_finish(c, projs[c], h, parts[i], nums[i])
    row_slices = [slice(c * cl, (c + 1) * cl) for c in chunks]
    for c in chunks:
        q_mem = projs[c][:, 2 * hq + 2 * hv:]
        for h in range(MEM_HEADS):
            _memory_output(_memory_scores(q_mem, mkv_ref, h), mkv_ref, cat_ref.at[row_slices[c]],
                           hv, h)
    for rows in row_slices:
        _out_proj_norm(x_ref.at[rows], cat_ref.at[rows], wout_ref, g_ref, b_ref, o_ref.at[rows])


def _mixer_a(x2d, wmain, wgr, bgr, memkv, wout, g, b, batch, seq):
    ns = seq // TS_A
    width = MLSTM_HEADS * MLSTM_V + MEM_HEADS * MEM_DIM
    return pl.pallas_call(
        _mixer_a_kernel,
        out_shape=jax.ShapeDtypeStruct(x2d.shape, F32),
        grid=(batch, ns),
        in_specs=[pl.BlockSpec((TS_A, D_MODEL), lambda bi, si: (bi * ns + si, 0)),
                  _const_spec(wmain.shape), _const_spec(wgr.shape), _const_spec(bgr.shape),
                  pl.BlockSpec((N_MEM, 2 * MEM_HEADS * MEM_DIM), lambda bi, si: (bi, 0)),
                  _const_spec(wout.shape), _const_spec(g.shape), _const_spec(b.shape)],
        out_specs=pl.BlockSpec((TS_A, D_MODEL), lambda bi, si: (bi * ns + si, 0)),
        scratch_shapes=[pltpu.VMEM((MLSTM_HEADS, LANES, 2 * MLSTM_V), F32),
                        pltpu.VMEM((GATE_ROWS, 1), F32),
                        pltpu.VMEM((TS_A, width), BF16)],
        compiler_params=pltpu.CompilerParams(dimension_semantics=("parallel", "arbitrary"),
                                             vmem_limit_bytes=VMEM_LIMIT),
        name="mixer_a",
    )(x2d, wmain, wgr, bgr, memkv, wout, g, b)


def _shared_kv_down(y, wd_ref):
    return _dot(y.astype(BF16), wd_ref[...])


def _shared_kv_rows(d, ctab, stab, gk_ref, wuk_ref, wuvt_ref):
    ckv = d[:, :KV_LORA]
    ckv = ckv * lax.rsqrt(jnp.mean(ckv * ckv, axis=-1, keepdims=True) + RMS_EPS) * gk_ref[...]
    ckv = ckv.astype(BF16)
    k_rope = (d[:, KV_LORA:KV_LORA + LANES] * ctab
              + d[:, KV_LORA + LANES:KV_LORA + 2 * LANES] * stab)
    k_nope = _dot(ckv, wuk_ref[...])
    k = jnp.concatenate([(k_nope[:, h * HEAD_PAD:(h + 1) * HEAD_PAD] + k_rope).astype(BF16)
                         for h in range(MLA_HEADS)], axis=1)
    vt = _dot_nt(wuvt_ref[...], ckv).astype(BF16)
    ones = jnp.ones((BF16_SUBLANES, vt.shape[1]), BF16)
    pieces = []
    for h in range(MLA_HEADS):
        pieces += [vt[h * MLA_V:(h + 1) * MLA_V], ones]
    return k, jnp.concatenate(pieces, axis=0)


def _ffn_rows(x, wup_ref, wdn_ref, g_ref, b_ref):
    xb = x.astype(BF16)
    acc = jnp.zeros(x.shape, F32)
    for j in range(D_FF // FF_CHUNK):
        hid = _dot(xb, wup_ref[:, j * FF_CHUNK:(j + 1) * FF_CHUNK])
        hid = jnp.square(jnp.maximum(hid, 0.0)).astype(BF16)
        acc = acc + _dot(hid, wdn_ref[j * FF_CHUNK:(j + 1) * FF_CHUNK, :])
    return _layer_norm(ALPHA * x + acc, g_ref[...], b_ref[...])


def _ffn_kernel(x_ref, wup_ref, wdn_ref, g_ref, b_ref, o_ref):
    for r in range(x_ref.shape[0] // FFN_ROWS):
        rows = slice(r * FFN_ROWS, (r + 1) * FFN_ROWS)
        o_ref[rows, :] = _ffn_rows(x_ref[rows, :], wup_ref, wdn_ref, g_ref, b_ref)


def _ffn_kv_kernel(x_ref, wup_ref, wdn_ref, g_ref, b_ref, pos_ref, invf_ref, wd_ref, gk_ref,
                   wuk_ref, wuvt_ref, o_ref, k_ref, vt_ref, c_ref, s_ref):
    n_sub = x_ref.shape[0] // FFN_ROWS
    ys, downs = [], []
    for r in range(n_sub + 2):
        if r < n_sub:
            rows = slice(r * FFN_ROWS, (r + 1) * FFN_ROWS)
            c_ref[rows, :], s_ref[rows, :] = _rope_rows(pos_ref[:, rows], invf_ref[...])
            ys.append(_ffn_rows(x_ref[rows, :], wup_ref, wdn_ref, g_ref, b_ref))
            o_ref[rows, :] = ys[r]
        if 1 <= r <= n_sub:
            downs.append(_shared_kv_down(ys[r - 1], wd_ref))
        if r >= 2:
            prev = slice((r - 2) * FFN_ROWS, (r - 1) * FFN_ROWS)
            k_ref[prev, :], vt_ref[r - 2] = _shared_kv_rows(
                downs[r - 2], c_ref[prev, :], s_ref[prev, :], gk_ref, wuk_ref, wuvt_ref)


def _ffn(x2d, wup, wdn, g, b, kv_args=None):
    t = x2d.shape[0]
    tile = pl.BlockSpec((TM_FFN, D_MODEL), lambda i: (i, 0))
    in_specs = [tile, _const_spec(wup.shape), _const_spec(wdn.shape),
                _const_spec(g.shape), _const_spec(b.shape)]
    params = pltpu.CompilerParams(dimension_semantics=("parallel",), vmem_limit_bytes=VMEM_LIMIT)
    if kv_args is None:
        return pl.pallas_call(
            _ffn_kernel, out_shape=jax.ShapeDtypeStruct(x2d.shape, F32), grid=(t // TM_FFN,),
            in_specs=in_specs, out_specs=tile, compiler_params=params, name="ffn",
        )(x2d, wup, wdn, g, b)
    kw, vw = MLA_HEADS * HEAD_PAD, MLA_HEADS * VT_ROWS
    table = pl.BlockSpec((TM_FFN, LANES), lambda i: (i, 0))
    in_specs += [pl.BlockSpec((1, TM_FFN), lambda i: (0, i))]
    in_specs += [_const_spec(w.shape) for w in kv_args[1:]]
    return pl.pallas_call(
        _ffn_kv_kernel,
        out_shape=(jax.ShapeDtypeStruct(x2d.shape, F32), jax.ShapeDtypeStruct((t, kw), BF16),
                   jax.ShapeDtypeStruct((t // TK_B, vw, TK_B), BF16),
                   jax.ShapeDtypeStruct((t, LANES), F32), jax.ShapeDtypeStruct((t, LANES), F32)),
        grid=(t // TM_FFN,), in_specs=in_specs,
        out_specs=(tile, pl.BlockSpec((TM_FFN, kw), lambda i: (i, 0)),
                   pl.BlockSpec((TM_FFN // TK_B, vw, TK_B), lambda i: (i, 0, 0)), table, table),
        compiler_params=params, name="ffn_kv",
    )(x2d, wup, wdn, g, b, *kv_args)


def _mixer_b_kernel(x_ref, win_ref, gq_ref, wuq_ref, wuqs_ref, c_ref, s_ref, k_ref, vt_ref,
                    mkv_ref, wout_ref, g_ref, b_ref, o_ref, q_sc, sa_sc, sb_sc, m_sc, acc_sc,
                    ot_sc, cat_ref):
    tq = TQ_B
    step = pl.program_id(1)
    q_scale = (MLA_NOPE + MLA_ROPE) ** -0.5 * LOG2_E
    key_chunk = lax.broadcasted_iota(jnp.int32, (TK_B, tq), 0) // MLA_CHUNK
    qry_chunk = lax.broadcasted_iota(jnp.int32, (TK_B, tq), 1) // MLA_CHUNK
    allowed = key_chunk <= qry_chunk

    def in_proj(rows):
        return _dot(x_ref[rows, :].astype(BF16), win_ref[...])

    def queries(rows, proj):
        cq = proj[:, :Q_LORA]
        cq = cq * lax.rsqrt(jnp.mean(cq * cq, axis=-1, keepdims=True) + RMS_EPS) * gq_ref[...]
        cq = cq.astype(BF16)
        ctab = c_ref[rows, :]
        stab = s_ref[rows, :]
        for pair in range(MLA_HEADS // 2):
            cols = slice(2 * pair * HEAD_PAD, 2 * (pair + 1) * HEAD_PAD)
            q_lin = _dot(cq, wuq_ref[:, cols])
            q_swp = _dot(cq, wuqs_ref[:, cols])
            for h in range(2):
                sl = slice(h * HEAD_PAD, (h + 1) * HEAD_PAD)
                dst = slice((2 * pair + h) * HEAD_PAD, (2 * pair + h + 1) * HEAD_PAD)
                q_sc[rows, dst] = ((q_lin[:, sl] * ctab + q_swp[:, sl] * stab)
                                   * q_scale).astype(BF16)

    def attention(rows, idx):
        qi = MIXB_TILES * step + idx
        odd = idx % 2

        def scores(j, h):
            kb = k_ref[pl.ds(pl.multiple_of(j * TK_B, TK_B), TK_B), h * HEAD_PAD:(h + 1) * HEAD_PAD]
            return _dot_nt(kb, q_sc[rows, h * HEAD_PAD:(h + 1) * HEAD_PAD])

        def softmax_pv(j, h, src, masked):
            s = src[h]
            if masked:
                s = jnp.where(allowed, s, -jnp.inf)
            m_old = m_sc[h]
            m_new = jnp.maximum(m_old, jnp.max(s, axis=0, keepdims=True))
            corr = jnp.exp2(m_old - m_new)
            p = jnp.exp2(s - m_new).astype(BF16)
            acc_sc[h] = corr * acc_sc[h] + _dot(vt_ref[j, h * VT_ROWS:(h + 1) * VT_ROWS, :], p)
            m_sc[h] = m_new

        def key_tile(j, src, dst, masked):
            if dst is not None:
                for h in range(QK_AHEAD):
                    dst[h] = scores(j + 1, h)
            for h in range(MLA_HEADS):
                softmax_pv(j, h, src, masked)
                if dst is not None and h + QK_AHEAD < MLA_HEADS:
                    dst[h + QK_AHEAD] = scores(j + 1, h + QK_AHEAD)

        m_sc[...] = jnp.full(m_sc.shape, -jnp.inf, F32)
        acc_sc[...] = jnp.zeros_like(acc_sc)
        for h in range(MLA_HEADS):
            sa_sc[h] = scores(0, h)

        def tile_pair(k, carry):
            key_tile(2 * k, sa_sc, sb_sc, False)
            key_tile(2 * k + 1, sb_sc, sa_sc, False)
            return carry

        lax.fori_loop(0, qi // 2, tile_pair, 0)
        if odd:
            key_tile(qi - 1, sa_sc, sb_sc, False)
            key_tile(qi, sb_sc, None, True)
        else:
            key_tile(qi, sa_sc, None, True)
        for h in range(MLA_HEADS):
            acc = acc_sc[h]
            ot_sc[h * MLA_V:(h + 1) * MLA_V, :] = acc[:MLA_V] / acc[MLA_V:MLA_V + 1]
        cat_ref[rows, :MLA_HEADS * MLA_V] = ot_sc[...].T.astype(BF16)

    tiles = [slice(i * tq, (i + 1) * tq) for i in range(MIXB_TILES)]
    proj = in_proj(tiles[0])
    queries(tiles[0], proj)
    for i, rows in enumerate(tiles):
        attention(rows, i)
        q_mem = proj[:, Q_LORA:]
        mem_scores = [_memory_scores(q_mem, mkv_ref, h) for h in range(MEM_HEADS)]
        if i + 1 < MIXB_TILES:
            proj = in_proj(tiles[i + 1])
        for h in range(MEM_HEADS):
            _memory_output(mem_scores[h], mkv_ref, cat_ref.at[rows], MLA_HEADS * MLA_V, h)
        if i + 1 < MIXB_TILES:
            queries(tiles[i + 1], proj)
        _out_proj_norm(x_ref.at[rows], cat_ref.at[rows], wout_ref, g_ref, b_ref, o_ref.at[rows])


def _mixer_b(x2d, win, gq, wuq, wuqs, ctab, stab, k_all, vt_all, memkv, wout, g, b, batch, seq):
    assert TQ_B == TK_B
    assert MIXB_TILES % 2 == 0
    ts = MIXB_TILES * TQ_B
    ns = seq // ts
    width = MLA_HEADS * MLA_V + MEM_HEADS * MEM_DIM
    tile = lambda bi, si: (bi * ns + si, 0)
    return pl.pallas_call(
        _mixer_b_kernel,
        out_shape=jax.ShapeDtypeStruct(x2d.shape, F32),
        grid=(batch, ns),
        in_specs=[pl.BlockSpec((ts, D_MODEL), tile),
                  _const_spec(win.shape), _const_spec(gq.shape),
                  _const_spec(wuq.shape), _const_spec(wuqs.shape),
                  pl.BlockSpec((ts, LANES), tile), pl.BlockSpec((ts, LANES), tile),
                  pl.BlockSpec((seq, k_all.shape[1]), lambda bi, si: (bi, 0)),
                  pl.BlockSpec((seq // TK_B,) + vt_all.shape[1:], lambda bi, si: (bi, 0, 0)),
                  pl.BlockSpec((N_MEM, 2 * MEM_HEADS * MEM_DIM), lambda bi, si: (bi, 1)),
                  _const_spec(wout.shape), _const_spec(g.shape), _const_spec(b.shape)],
        out_specs=pl.BlockSpec((ts, D_MODEL), tile),
        scratch_shapes=[pltpu.VMEM((ts, MLA_HEADS * HEAD_PAD), BF16),
                        pltpu.VMEM((MLA_HEADS, TK_B, TQ_B), F32),
                        pltpu.VMEM((MLA_HEADS, TK_B, TQ_B), F32),
                        pltpu.VMEM((MLA_HEADS, 1, TQ_B), F32),
                        pltpu.VMEM((MLA_HEADS, VT_ROWS, TQ_B), F32),
                        pltpu.VMEM((MLA_HEADS * MLA_V, TQ_B), F32),
                        pltpu.VMEM((ts, width), BF16)],
        compiler_params=pltpu.CompilerParams(dimension_semantics=("parallel", "arbitrary"),
                                             vmem_limit_bytes=VMEM_LIMIT),
        name="mixer_b",
    )(x2d, win, gq, wuq, wuqs, ctab, stab, k_all, vt_all, memkv, wout, g, b)


def _pad_heads(w, heads, dim):
    r = w.shape[0]
    w = w.reshape(r, heads, dim)
    w = jnp.pad(w, ((0, 0), (0, 0), (0, HEAD_PAD - dim)))
    return w.reshape(r, heads * HEAD_PAD)


def _swap_rope_halves(w, heads):
    r = w.shape[0]
    w = w.reshape(r, heads, MLA_NOPE + MLA_ROPE)
    x1 = w[..., MLA_NOPE:MLA_NOPE + ROPE_HALF]
    x2 = w[..., MLA_NOPE + ROPE_HALF:]
    return jnp.concatenate([jnp.zeros_like(w[..., :MLA_NOPE]), x2, x1], axis=-1).reshape(r, -1)


def kernel(x, mem, positions, a_w_in, a_b_igate, a_b_fgate, a_w_mem_kv, a_w_out, kv_w_down, kv_norm_g, kv_w_uk, kv_w_uv, b_w_in, b_q_norm_g, b_w_uq, b_w_mem_kv, b_w_out, ln1_g, ln1_b, ffn_w_up, ffn_w_down, ln2_g, ln2_b):
    batch, seq, _ = x.shape
    t = batch * seq
    x2d = x.reshape(t, D_MODEL)
    row = lambda v: v.reshape(1, -1).astype(F32)

    memkv = _mem_kv(mem.reshape(batch * N_MEM, D_MODEL),
                    jnp.concatenate([a_w_mem_kv[0], b_w_mem_kv[0]], axis=1).astype(BF16))

    hq = MLSTM_HEADS * MLSTM_QK
    hv = MLSTM_HEADS * MLSTM_V
    g0 = 2 * hq + 2 * hv
    w_in = a_w_in[0]
    wmain = jnp.concatenate([w_in[:, :g0], w_in[:, g0 + GATE_ROWS:]], axis=1).astype(BF16)
    wgr = w_in[:, g0:g0 + GATE_ROWS].T.astype(BF16)
    bgr = jnp.concatenate([a_b_igate[0], a_b_fgate[0]]).astype(F32).reshape(GATE_ROWS, 1)
    x2d = _mixer_a(x2d, wmain, wgr, bgr, memkv, a_w_out[0].astype(BF16),
                   row(ln1_g[0]), row(ln1_b[0]), batch, seq)

    inv_freq = ROPE_THETA ** (-jnp.arange(0, MLA_ROPE, 2, dtype=F32) / MLA_ROPE)
    wd = jnp.zeros((D_MODEL, KV_LORA + 2 * LANES), F32)
    wd = wd.at[:, :KV_LORA].set(kv_w_down[:, :KV_LORA])
    r0 = KV_LORA + ROPE_LO
    wd = wd.at[:, r0:r0 + MLA_ROPE].set(kv_w_down[:, KV_LORA:])
    r1 = KV_LORA + LANES + ROPE_LO
    wd = wd.at[:, r1:r1 + ROPE_HALF].set(kv_w_down[:, KV_LORA + ROPE_HALF:])
    wd = wd.at[:, r1 + ROPE_HALF:r1 + MLA_ROPE].set(kv_w_down[:, KV_LORA:KV_LORA + ROPE_HALF])
    kv_args = (positions.reshape(1, t).astype(F32), inv_freq.reshape(ROPE_HALF, 1),
               wd.astype(BF16), row(kv_norm_g),
               _pad_heads(kv_w_uk, MLA_HEADS, MLA_NOPE).astype(BF16), kv_w_uv.T.astype(BF16))
    x2d, k_all, vt_all, ctab, stab = _ffn(x2d, ffn_w_up[0].astype(BF16),
                                          ffn_w_down[0].astype(BF16),
                                          row(ln2_g[0]), row(ln2_b[0]), kv_args)

    wuq = _pad_heads(b_w_uq[0], MLA_HEADS, MLA_NOPE + MLA_ROPE).astype(BF16)
    wuqs = _pad_heads(_swap_rope_halves(b_w_uq[0], MLA_HEADS), MLA_HEADS,
                      MLA_NOPE + MLA_ROPE).astype(BF16)
    x2d = _mixer_b(x2d, b_w_in[0].astype(BF16), row(b_q_norm_g[0]), wuq, wuqs, ctab, stab,
                   k_all, vt_all, memkv, b_w_out[0].astype(BF16),
                   row(ln1_g[1]), row(ln1_b[1]), batch, seq)
    x2d = _ffn(x2d, ffn_w_up[1].astype(BF16), ffn_w_down[1].astype(BF16),
               row(ln2_g[1]), row(ln2_b[1]))
    return x2d.reshape(batch, seq, D_MODEL)
```

```python
import jax
import jax.numpy as jnp
from jax import lax
from jax.experimental import pallas as pl
from jax.experimental.pallas import tpu as pltpu

F32 = jnp.float32
BF16 = jnp.bfloat16

D_MODEL = 1024
DEPTH = 2
N_MEM = 256
MLSTM_HEADS = 4
MLSTM_QK = 64
MLSTM_V = 128
MEM_HEADS = 4
MEM_DIM = 128
MLA_HEADS = 8
MLA_NOPE = 64
MLA_ROPE = 32
MLA_V = 64
Q_LORA = 256
KV_LORA = 256
D_FF = 4 * D_MODEL
ROPE_THETA = 10000.0
LN_EPS = 1e-5
RMS_EPS = 1e-6
ALPHA = (2 * DEPTH) ** 0.25
MLA_CHUNK = 64
BF16_SUBLANES = 16
VT_ROWS = MLA_V + BF16_SUBLANES
QK_AHEAD = 2
LOG2_E = 1.4426950408889634

LANES = 128
HEAD_PAD = 128
ROPE_LO = MLA_NOPE
ROPE_HALF = MLA_ROPE // 2

MLSTM_CHUNK = 256
MLSTM_CHUNKS = 2
TS_A = MLSTM_CHUNK * MLSTM_CHUNKS
HEADS_LOCKSTEP = 2
GATE_ROWS = 2 * MLSTM_HEADS
COL_EXPO, COL_INTER, COL_EINV, COL_WGT = (i * GATE_ROWS for i in range(4))
TQ_B = 256
TK_B = 256
MIXB_TILES = 4
TM_FFN = 1024
TM_MEM = 512
FF_CHUNK = 2048
FFN_ROWS = TK_B
V7X_VMEM_BYTES = 64 * 1024 * 1024
VMEM_LIMIT = V7X_VMEM_BYTES - 8 * 1024 * 1024

NT_DIMS = (((1,), (1,)), ((), ()))
TN_DIMS = (((0,), (0,)), ((), ()))


def _dot(a, b):
    return jnp.dot(a, b, preferred_element_type=F32)


def _dot_nt(a, b):
    return lax.dot_general(a, b, NT_DIMS, preferred_element_type=F32)


def _layer_norm(y, g, b):
    mu = jnp.mean(y, axis=-1, keepdims=True)
    yc = y - mu
    var = jnp.mean(yc * yc, axis=-1, keepdims=True)
    return yc * lax.rsqrt(var + LN_EPS) * g + b


def _log_sigmoid(z):
    return jnp.minimum(z, 0.0) - jnp.log(1.0 + jnp.exp(-jnp.abs(z)))


def _out_proj_norm(x_ref, cat_ref, wout_ref, g_ref, b_ref, o_ref):
    mix = _dot(cat_ref[...], wout_ref[...])
    o_ref[...] = _layer_norm(ALPHA * x_ref[...] + mix, g_ref[...], b_ref[...])


def _const_spec(shape):
    nd = len(shape)
    return pl.BlockSpec(shape, lambda *_: (0,) * nd, pipeline_mode=pl.Buffered(1))


def _rope_rows(pos, invf):
    ang = invf * pos
    cos = jnp.cos(ang)
    sin = jnp.sin(ang)
    n = ang.shape[1]
    tail = LANES - ROPE_LO - MLA_ROPE
    ct = jnp.concatenate([jnp.ones((ROPE_LO, n), F32), cos, cos, jnp.ones((tail, n), F32)], axis=0)
    st = jnp.concatenate([jnp.zeros((ROPE_LO, n), F32), -sin, sin, jnp.zeros((tail, n), F32)], axis=0)
    return ct.T, st.T


def _mem_kv_kernel(mem_ref, w_ref, o_ref):
    o_ref[...] = _dot(mem_ref[...].astype(BF16), w_ref[...]).astype(BF16)


def _mem_kv(mem2d, w):
    r, n = mem2d.shape[0], w.shape[1]
    return pl.pallas_call(
        _mem_kv_kernel,
        out_shape=jax.ShapeDtypeStruct((r, n), BF16),
        grid=(r // TM_MEM,),
        in_specs=[pl.BlockSpec((TM_MEM, D_MODEL), lambda i: (i, 0)),
                  _const_spec(w.shape)],
        out_specs=pl.BlockSpec((TM_MEM, n), lambda i: (i, 0)),
        compiler_params=pltpu.CompilerParams(dimension_semantics=("parallel",),
                                             vmem_limit_bytes=VMEM_LIMIT),
        name="mem_kv",
    )(mem2d, w)


def _memory_scores(q_all, mkv_ref, h):
    q_scale = MEM_DIM ** -0.5 * LOG2_E
    lo = h * MEM_DIM
    qh = (q_all[:, lo:lo + MEM_DIM] * q_scale).astype(BF16)
    return _dot_nt(qh, mkv_ref[:, lo:lo + MEM_DIM])


def _memory_output(s, mkv_ref, cat_ref, col0, h):
    lo = h * MEM_DIM
    v0 = MEM_HEADS * MEM_DIM + lo
    p = jnp.exp2(s - jnp.max(s, axis=-1, keepdims=True)).astype(BF16)
    v_ext = jnp.concatenate([mkv_ref[:, v0:v0 + MEM_DIM], jnp.ones((N_MEM, LANES), BF16)], axis=1)
    o = _dot(p, v_ext)
    cat_ref[:, col0 + lo:col0 + lo + MEM_DIM] = (o[:, :MEM_DIM] / o[:, MEM_DIM:]).astype(BF16)


def _lane_scan(x, combine, fill):
    n = x.shape[1]
    lane = lax.broadcasted_iota(jnp.int32, x.shape, 1)
    d = 1
    while d < n:
        x = combine(x, jnp.where(lane >= d, pltpu.roll(x, d, axis=1), fill))
        d *= 2
    return x


def _mlstm_gates(gate_all, m_prev, n_chunks):
    length = gate_all.shape[1] // n_chunks
    rows = GATE_ROWS
    gate = jnp.concatenate([gate_all[:, c * length:(c + 1) * length] for c in range(n_chunks)],
                           axis=0)
    head_row = lax.broadcasted_iota(jnp.int32, gate.shape, 0) % rows < MLSTM_HEADS
    cum_f = _lane_scan(_log_sigmoid(gate), jnp.add, 0.0)
    a_all = jnp.where(head_row, gate, 0.0)
    b_all = jnp.where(head_row, pltpu.roll(cum_f, gate.shape[0] - MLSTM_HEADS, axis=0), 0.0)
    r_all = a_all - b_all
    mi_all = b_all + _lane_scan(r_all, jnp.maximum, -jnp.inf)
    out = []
    for c in range(n_chunks):
        b, r, m_intra = (t[c * rows:(c + 1) * rows] for t in (b_all, r_all, mi_all))
        g_tot = b[:, length - 1:length]
        m_inter = b + m_prev
        m_t = jnp.maximum(m_inter, m_intra)
        m_new = jnp.maximum(g_tot + m_prev, jnp.max(g_tot + r, axis=1, keepdims=True))
        decay = jnp.exp(g_tot + m_prev - m_new)
        stack = jnp.concatenate([(b - m_t) * LOG2_E, jnp.exp(m_inter - m_t), jnp.exp(-m_t),
                                 jnp.exp(g_tot + r - m_new),
                                 jnp.zeros((LANES - 4 * rows, length), F32)], axis=0)
        out.append((r * LOG2_E, stack.T, decay))
        m_prev = m_new
    return out, m_prev


def _mixer_a_kernel(x_ref, wmain_ref, wgr_ref, bgr_ref, mkv_ref, wout_ref, g_ref, b_ref, o_ref,
                    c_st, m_st, cat_ref):
    hq = MLSTM_HEADS * MLSTM_QK
    hv = MLSTM_HEADS * MLSTM_V
    cl = MLSTM_CHUNK

    @pl.when(pl.program_id(1) == 0)
    def _():
        c_st[...] = jnp.zeros_like(c_st)
        m_st[...] = jnp.zeros_like(m_st)

    causal = (lax.broadcasted_iota(jnp.int32, (cl, cl), 1)
              <= lax.broadcasted_iota(jnp.int32, (cl, cl), 0))
    ones_blk = jnp.ones((cl, LANES), BF16)
    lane_half = lax.broadcasted_iota(jnp.int32, (cl, LANES), 1) // MLSTM_QK

    xbs = [x_ref[c * cl:(c + 1) * cl, :].astype(BF16) for c in range(MLSTM_CHUNKS)]
    gate_all = jnp.concatenate([_dot_nt(wgr_ref[...], xb) for xb in xbs], axis=1) + bgr_ref[...]
    gates, m_st[...] = _mlstm_gates(gate_all, m_st[...], MLSTM_CHUNKS)

    def head_scores(c, proj, h):
        blk = slice((h // 2) * LANES, (h // 2 + 1) * LANES)
        mine = lane_half == h % 2
        q = jnp.where(mine, proj[:, blk], 0.0).astype(BF16)
        k = proj[:, hq:2 * hq][:, blk] * (MLSTM_QK ** -0.5)
        v = proj[:, 2 * hq + h * MLSTM_V:2 * hq + (h + 1) * MLSTM_V].astype(BF16)
        v_ext = jnp.concatenate([v, ones_blk], axis=1)
        return q, k, v_ext, mine, _dot_nt(q, k.astype(BF16))

    def head_pv(c, h, parts):
        r2, cols, _ = gates[c]
        _, _, v_ext, _, qk = parts
        expo = jnp.where(causal, cols[:, COL_EXPO + h:COL_EXPO + h + 1] + r2[h:h + 1, :], -jnp.inf)
        p = qk * jnp.exp2(expo)
        return _dot(p.astype(BF16), v_ext)

    def head_finish(c, proj, h, parts, num):
        _, cols, decay = gates[c]
        q, k, v_ext, mine, _ = parts
        rows = slice(c * cl, (c + 1) * cl)
        o_pre = proj[:, 2 * hq + hv + h * MLSTM_V:2 * hq + hv + (h + 1) * MLSTM_V]
        c_prev = c_st[h]
        inter_b = jnp.broadcast_to(cols[:, COL_INTER + h:COL_INTER + h + 1], (cl, MLSTM_V))
        einv_b = jnp.broadcast_to(cols[:, COL_EINV + h:COL_EINV + h + 1], (cl, MLSTM_V))
        qc = _dot(q, c_prev.astype(BF16))
        nq = num[:, MLSTM_V:] + inter_b * qc[:, MLSTM_V:]
        hh = (num[:, :MLSTM_V] + inter_b * qc[:, :MLSTM_V]) / jnp.maximum(jnp.abs(nq), einv_b)
        hh = hh * jax.nn.sigmoid(o_pre)
        cat_ref[rows, h * MLSTM_V:(h + 1) * MLSTM_V] = hh.astype(BF16)
        kw = jnp.where(mine, k * cols[:, COL_WGT + h:COL_WGT + h + 1], 0.0).astype(BF16)
        c_st[h] = decay[h:h + 1, :] * c_prev + lax.dot_general(kw, v_ext, TN_DIMS,
                                                               preferred_element_type=F32)

    chunks = range(MLSTM_CHUNKS)
    projs = [_dot(xbs[c], wmain_ref[...]) for c in chunks]
    for h0 in range(0, MLSTM_HEADS, HEADS_LOCKSTEP):
        group = [(c, h) for h in range(h0, h0 + HEADS_LOCKSTEP) for c in chunks]
        parts = [head_scores(c, projs[c], h) for c, h in group]
        nums = [head_pv(c, h, parts[i]) for i, (c, h) in enumerate(group)]
        for i, (c, h) in enumerate(group):
            head_finish(c, projs[c], h, parts[i], nums[i])
    row_slices = [slice(c * cl, (c + 1) * cl) for c in chunks]
    for c in chunks:
        q_mem = projs[c][:, 2 * hq + 2 * hv:]
        for h in range(MEM_HEADS):
            _memory_output(_memory_scores(q_mem, mkv_ref, h), mkv_ref, cat_ref.at[row_slices[c]],
                           hv, h)
    for rows in row_slices:
        _out_proj_norm(x_ref.at[rows], cat_ref.at[rows], wout_ref, g_ref, b_ref, o_ref.at[rows])


def _mixer_a(x2d, wmain, wgr, bgr, memkv, wout, g, b, batch, seq):
    ns = seq // TS_A
    width = MLSTM_HEADS * MLSTM_V + MEM_HEADS * MEM_DIM
    return pl.pallas_call(
        _mixer_a_kernel,
        out_shape=jax.ShapeDtypeStruct(x2d.shape, F32),
        grid=(batch, ns),
        in_specs=[pl.BlockSpec((TS_A, D_MODEL), lambda bi, si: (bi * ns + si, 0)),
                  _const_spec(wmain.shape), _const_spec(wgr.shape), _const_spec(bgr.shape),
                  pl.BlockSpec((N_MEM, 2 * MEM_HEADS * MEM_DIM), lambda bi, si: (bi, 0)),
                  _const_spec(wout.shape), _const_spec(g.shape), _const_spec(b.shape)],
        out_specs=pl.BlockSpec((TS_A, D_MODEL), lambda bi, si: (bi * ns + si, 0)),
        scratch_shapes=[pltpu.VMEM((MLSTM_HEADS, LANES, 2 * MLSTM_V), F32),
                        pltpu.VMEM((GATE_ROWS, 1), F32),
                        pltpu.VMEM((TS_A, width), BF16)],
        compiler_params=pltpu.CompilerParams(dimension_semantics=("parallel", "arbitrary"),
                                             vmem_limit_bytes=VMEM_LIMIT),
        name="mixer_a",
    )(x2d, wmain, wgr, bgr, memkv, wout, g, b)


def _shared_kv_down(y, wd_ref):
    return _dot(y.astype(BF16), wd_ref[...])


def _shared_kv_rows(d, ctab, stab, gk_ref, wuk_ref, wuvt_ref):
    ckv = d[:, :KV_LORA]
    ckv = ckv * lax.rsqrt(jnp.mean(ckv * ckv, axis=-1, keepdims=True) + RMS_EPS) * gk_ref[...]
    ckv = ckv.astype(BF16)
    k_rope = (d[:, KV_LORA:KV_LORA + LANES] * ctab
              + d[:, KV_LORA + LANES:KV_LORA + 2 * LANES] * stab)
    k_nope = _dot(ckv, wuk_ref[...])
    k = jnp.concatenate([(k_nope[:, h * HEAD_PAD:(h + 1) * HEAD_PAD] + k_rope).astype(BF16)
                         for h in range(MLA_HEADS)], axis=1)
    vt = _dot_nt(wuvt_ref[...], ckv).astype(BF16)
    ones = jnp.ones((BF16_SUBLANES, vt.shape[1]), BF16)
    pieces = []
    for h in range(MLA_HEADS):
        pieces += [vt[h * MLA_V:(h + 1) * MLA_V], ones]
    return k, jnp.concatenate(pieces, axis=0)


def _ffn_rows(x, wup_ref, wdn_ref, g_ref, b_ref):
    xb = x.astype(BF16)
    acc = jnp.zeros(x.shape, F32)
    for j in range(D_FF // FF_CHUNK):
        hid = _dot(xb, wup_ref[:, j * FF_CHUNK:(j + 1) * FF_CHUNK])
        hid = jnp.square(jnp.maximum(hid, 0.0)).astype(BF16)
        acc = acc + _dot(hid, wdn_ref[j * FF_CHUNK:(j + 1) * FF_CHUNK, :])
    return _layer_norm(ALPHA * x + acc, g_ref[...], b_ref[...])


def _ffn_kernel(x_ref, wup_ref, wdn_ref, g_ref, b_ref, o_ref):
    for r in range(x_ref.shape[0] // FFN_ROWS):
        rows = slice(r * FFN_ROWS, (r + 1) * FFN_ROWS)
        o_ref[rows, :] = _ffn_rows(x_ref[rows, :], wup_ref, wdn_ref, g_ref, b_ref)


def _ffn_kv_kernel(x_ref, wup_ref, wdn_ref, g_ref, b_ref, pos_ref, invf_ref, wd_ref, gk_ref,
                   wuk_ref, wuvt_ref, o_ref, k_ref, vt_ref, c_ref, s_ref):
    n_sub = x_ref.shape[0] // FFN_ROWS
    ys, downs = [], []
    for r in range(n_sub + 2):
        if r < n_sub:
            rows = slice(r * FFN_ROWS, (r + 1) * FFN_ROWS)
            c_ref[rows, :], s_ref[rows, :] = _rope_rows(pos_ref[:, rows], invf_ref[...])
            ys.append(_ffn_rows(x_ref[rows, :], wup_ref, wdn_ref, g_ref, b_ref))
            o_ref[rows, :] = ys[r]
        if 1 <= r <= n_sub:
            downs.append(_shared_kv_down(ys[r - 1], wd_ref))
        if r >= 2:
            prev = slice((r - 2) * FFN_ROWS, (r - 1) * FFN_ROWS)
            k_ref[prev, :], vt_ref[r - 2] = _shared_kv_rows(
                downs[r - 2], c_ref[prev, :], s_ref[prev, :], gk_ref, wuk_ref, wuvt_ref)


def _ffn(x2d, wup, wdn, g, b, kv_args=None):
    t = x2d.shape[0]
    tile = pl.BlockSpec((TM_FFN, D_MODEL), lambda i: (i, 0))
    in_specs = [tile, _const_spec(wup.shape), _const_spec(wdn.shape),
                _const_spec(g.shape), _const_spec(b.shape)]
    params = pltpu.CompilerParams(dimension_semantics=("parallel",), vmem_limit_bytes=VMEM_LIMIT)
    if kv_args is None:
        return pl.pallas_call(
            _ffn_kernel, out_shape=jax.ShapeDtypeStruct(x2d.shape, F32), grid=(t // TM_FFN,),
            in_specs=in_specs, out_specs=tile, compiler_params=params, name="ffn",
        )(x2d, wup, wdn, g, b)
    kw, vw = MLA_HEADS * HEAD_PAD, MLA_HEADS * VT_ROWS
    table = pl.BlockSpec((TM_FFN, LANES), lambda i: (i, 0))
    in_specs += [pl.BlockSpec((1, TM_FFN), lambda i: (0, i))]
    in_specs += [_const_spec(w.shape) for w in kv_args[1:]]
    return pl.pallas_call(
        _ffn_kv_kernel,
        out_shape=(jax.ShapeDtypeStruct(x2d.shape, F32), jax.ShapeDtypeStruct((t, kw), BF16),
                   jax.ShapeDtypeStruct((t // TK_B, vw, TK_B), BF16),
                   jax.ShapeDtypeStruct((t, LANES), F32), jax.ShapeDtypeStruct((t, LANES), F32)),
        grid=(t // TM_FFN,), in_specs=in_specs,
        out_specs=(tile, pl.BlockSpec((TM_FFN, kw), lambda i: (i, 0)),
                   pl.BlockSpec((TM_FFN // TK_B, vw, TK_B), lambda i: (i, 0, 0)), table, table),
        compiler_params=params, name="ffn_kv",
    )(x2d, wup, wdn, g, b, *kv_args)


def _mixer_b_kernel(x_ref, win_ref, gq_ref, wuq_ref, wuqs_ref, c_ref, s_ref, k_ref, vt_ref,
                    mkv_ref, wout_ref, g_ref, b_ref, o_ref, q_sc, sa_sc, sb_sc, ma_sc, mb_sc, m_sc,
                    acc_sc, ot_sc, cat_ref):
    tq = TQ_B
    step = pl.program_id(1)
    q_scale = (MLA_NOPE + MLA_ROPE) ** -0.5 * LOG2_E
    key_chunk = lax.broadcasted_iota(jnp.int32, (TK_B, tq), 0) // MLA_CHUNK
    qry_chunk = lax.broadcasted_iota(jnp.int32, (TK_B, tq), 1) // MLA_CHUNK
    allowed = key_chunk <= qry_chunk

    def in_proj(rows):
        return _dot(x_ref[rows, :].astype(BF16), win_ref[...])

    def queries(rows, proj):
        cq = proj[:, :Q_LORA]
        cq = cq * lax.rsqrt(jnp.mean(cq * cq, axis=-1, keepdims=True) + RMS_EPS) * gq_ref[...]
        cq = cq.astype(BF16)
        ctab = c_ref[rows, :]
        stab = s_ref[rows, :]
        for pair in range(MLA_HEADS // 2):
            cols = slice(2 * pair * HEAD_PAD, 2 * (pair + 1) * HEAD_PAD)
            q_lin = _dot(cq, wuq_ref[:, cols])
            q_swp = _dot(cq, wuqs_ref[:, cols])
            for h in range(2):
                sl = slice(h * HEAD_PAD, (h + 1) * HEAD_PAD)
                dst = slice((2 * pair + h) * HEAD_PAD, (2 * pair + h + 1) * HEAD_PAD)
                q_sc[rows, dst] = ((q_lin[:, sl] * ctab + q_swp[:, sl] * stab)
                                   * q_scale).astype(BF16)

    def attention(rows, idx):
        qi = MIXB_TILES * step + idx
        odd = idx % 2

        def put_scores(dst, j, h):
            kb = k_ref[pl.ds(pl.multiple_of(j * TK_B, TK_B), TK_B), h * HEAD_PAD:(h + 1) * HEAD_PAD]
            s = _dot_nt(kb, q_sc[rows, h * HEAD_PAD:(h + 1) * HEAD_PAD])
            dst[0][h] = s
            dst[1][h] = jnp.max(s, axis=0, keepdims=True)

        def softmax_pv(j, h, src, masked):
            s = src[0][h]
            if masked:
                s = jnp.where(allowed, s, -jnp.inf)
                tile_max = jnp.max(s, axis=0, keepdims=True)
            else:
                tile_max = src[1][h]
            m_old = m_sc[h]
            m_new = jnp.maximum(m_old, tile_max)
            corr = jnp.exp2(m_old - m_new)
            p = jnp.exp2(s - m_new).astype(BF16)
            acc_sc[h] = corr * acc_sc[h] + _dot(vt_ref[j, h * VT_ROWS:(h + 1) * VT_ROWS, :], p)
            m_sc[h] = m_new

        def key_tile(j, src, dst, masked):
            if dst is not None:
                for h in range(QK_AHEAD):
                    put_scores(dst, j + 1, h)
            for h in range(MLA_HEADS):
                softmax_pv(j, h, src, masked)
                if dst is not None and h + QK_AHEAD < MLA_HEADS:
                    put_scores(dst, j + 1, h + QK_AHEAD)

        buf_a, buf_b = (sa_sc, ma_sc), (sb_sc, mb_sc)
        m_sc[...] = jnp.full(m_sc.shape, -jnp.inf, F32)
        acc_sc[...] = jnp.zeros_like(acc_sc)
        for h in range(MLA_HEADS):
            put_scores(buf_a, 0, h)

        def tile_pair(k, carry):
            key_tile(2 * k, buf_a, buf_b, False)
            key_tile(2 * k + 1, buf_b, buf_a, False)
            return carry

        lax.fori_loop(0, qi // 2, tile_pair, 0)
        if odd:
            key_tile(qi - 1, buf_a, buf_b, False)
            key_tile(qi, buf_b, None, True)
        else:
            key_tile(qi, buf_a, None, True)
        for h in range(MLA_HEADS):
            acc = acc_sc[h]
            ot_sc[h * MLA_V:(h + 1) * MLA_V, :] = acc[:MLA_V] / acc[MLA_V:MLA_V + 1]
        cat_ref[rows, :MLA_HEADS * MLA_V] = ot_sc[...].T.astype(BF16)

    tiles = [slice(i * tq, (i + 1) * tq) for i in range(MIXB_TILES)]
    proj = in_proj(tiles[0])
    queries(tiles[0], proj)
    for i, rows in enumerate(tiles):
        attention(rows, i)
        q_mem = proj[:, Q_LORA:]
        mem_scores = [_memory_scores(q_mem, mkv_ref, h) for h in range(MEM_HEADS)]
        if i + 1 < MIXB_TILES:
            proj = in_proj(tiles[i + 1])
        for h in range(MEM_HEADS):
            _memory_output(mem_scores[h], mkv_ref, cat_ref.at[rows], MLA_HEADS * MLA_V, h)
        if i + 1 < MIXB_TILES:
            queries(tiles[i + 1], proj)
        _out_proj_norm(x_ref.at[rows], cat_ref.at[rows], wout_ref, g_ref, b_ref, o_ref.at[rows])


def _mixer_b(x2d, win, gq, wuq, wuqs, ctab, stab, k_all, vt_all, memkv, wout, g, b, batch, seq):
    assert TQ_B == TK_B
    assert MIXB_TILES % 2 == 0
    ts = MIXB_TILES * TQ_B
    ns = seq // ts
    width = MLA_HEADS * MLA_V + MEM_HEADS * MEM_DIM
    tile = lambda bi, si: (bi * ns + si, 0)
    return pl.pallas_call(
        _mixer_b_kernel,
        out_shape=jax.ShapeDtypeStruct(x2d.shape, F32),
        grid=(batch, ns),
        in_specs=[pl.BlockSpec((ts, D_MODEL), tile),
                  _const_spec(win.shape), _const_spec(gq.shape),
                  _const_spec(wuq.shape), _const_spec(wuqs.shape),
                  pl.BlockSpec((ts, LANES), tile), pl.BlockSpec((ts, LANES), tile),
                  pl.BlockSpec((seq, k_all.shape[1]), lambda bi, si: (bi, 0)),
                  pl.BlockSpec((seq // TK_B,) + vt_all.shape[1:], lambda bi, si: (bi, 0, 0)),
                  pl.BlockSpec((N_MEM, 2 * MEM_HEADS * MEM_DIM), lambda bi, si: (bi, 1)),
                  _const_spec(wout.shape), _const_spec(g.shape), _const_spec(b.shape)],
        out_specs=pl.BlockSpec((ts, D_MODEL), tile),
        scratch_shapes=[pltpu.VMEM((ts, MLA_HEADS * HEAD_PAD), BF16),
                        pltpu.VMEM((MLA_HEADS, TK_B, TQ_B), F32),
                        pltpu.VMEM((MLA_HEADS, TK_B, TQ_B), F32),
                        pltpu.VMEM((MLA_HEADS, 1, TQ_B), F32),
                        pltpu.VMEM((MLA_HEADS, 1, TQ_B), F32),
                        pltpu.VMEM((MLA_HEADS, 1, TQ_B), F32),
                        pltpu.VMEM((MLA_HEADS, VT_ROWS, TQ_B), F32),
                        pltpu.VMEM((MLA_HEADS * MLA_V, TQ_B), F32),
                        pltpu.VMEM((ts, width), BF16)],
        compiler_params=pltpu.CompilerParams(dimension_semantics=("parallel", "arbitrary"),
                                             vmem_limit_bytes=VMEM_LIMIT),
        name="mixer_b",
    )(x2d, win, gq, wuq, wuqs, ctab, stab, k_all, vt_all, memkv, wout, g, b)


def _pad_heads(w, heads, dim):
    r = w.shape[0]
    w = w.reshape(r, heads, dim)
    w = jnp.pad(w, ((0, 0), (0, 0), (0, HEAD_PAD - dim)))
    return w.reshape(r, heads * HEAD_PAD)


def _swap_rope_halves(w, heads):
    r = w.shape[0]
    w = w.reshape(r, heads, MLA_NOPE + MLA_ROPE)
    x1 = w[..., MLA_NOPE:MLA_NOPE + ROPE_HALF]
    x2 = w[..., MLA_NOPE + ROPE_HALF:]
    return jnp.concatenate([jnp.zeros_like(w[..., :MLA_NOPE]), x2, x1], axis=-1).reshape(r, -1)


def kernel(x, mem, positions, a_w_in, a_b_igate, a_b_fgate, a_w_mem_kv, a_w_out, kv_w_down, kv_norm_g, kv_w_uk, kv_w_uv, b_w_in, b_q_norm_g, b_w_uq, b_w_mem_kv, b_w_out, ln1_g, ln1_b, ffn_w_up, ffn_w_down, ln2_g, ln2_b):
    batch, seq, _ = x.shape
    t = batch * seq
    x2d = x.reshape(t, D_MODEL)
    row = lambda v: v.reshape(1, -1).astype(F32)

    memkv = _mem_kv(mem.reshape(batch * N_MEM, D_MODEL),
                    jnp.concatenate([a_w_mem_kv[0], b_w_mem_kv[0]], axis=1).astype(BF16))

    hq = MLSTM_HEADS * MLSTM_QK
    hv = MLSTM_HEADS * MLSTM_V
    g0 = 2 * hq + 2 * hv
    w_in = a_w_in[0]
    wmain = jnp.concatenate([w_in[:, :g0], w_in[:, g0 + GATE_ROWS:]], axis=1).astype(BF16)
    wgr = w_in[:, g0:g0 + GATE_ROWS].T.astype(BF16)
    bgr = jnp.concatenate([a_b_igate[0], a_b_fgate[0]]).astype(F32).reshape(GATE_ROWS, 1)
    x2d = _mixer_a(x2d, wmain, wgr, bgr, memkv, a_w_out[0].astype(BF16),
                   row(ln1_g[0]), row(ln1_b[0]), batch, seq)

    inv_freq = ROPE_THETA ** (-jnp.arange(0, MLA_ROPE, 2, dtype=F32) / MLA_ROPE)
    wd = jnp.zeros((D_MODEL, KV_LORA + 2 * LANES), F32)
    wd = wd.at[:, :KV_LORA].set(kv_w_down[:, :KV_LORA])
    r0 = KV_LORA + ROPE_LO
    wd = wd.at[:, r0:r0 + MLA_ROPE].set(kv_w_down[:, KV_LORA:])
    r1 = KV_LORA + LANES + ROPE_LO
    wd = wd.at[:, r1:r1 + ROPE_HALF].set(kv_w_down[:, KV_LORA + ROPE_HALF:])
    wd = wd.at[:, r1 + ROPE_HALF:r1 + MLA_ROPE].set(kv_w_down[:, KV_LORA:KV_LORA + ROPE_HALF])
    kv_args = (positions.reshape(1, t).astype(F32), inv_freq.reshape(ROPE_HALF, 1),
               wd.astype(BF16), row(kv_norm_g),
               _pad_heads(kv_w_uk, MLA_HEADS, MLA_NOPE).astype(BF16), kv_w_uv.T.astype(BF16))
    x2d, k_all, vt_all, ctab, stab = _ffn(x2d, ffn_w_up[0].astype(BF16),
                                          ffn_w_down[0].astype(BF16),
                                          row(ln2_g[0]), row(ln2_b[0]), kv_args)

    wuq = _pad_heads(b_w_uq[0], MLA_HEADS, MLA_NOPE + MLA_ROPE).astype(BF16)
    wuqs = _pad_heads(_swap_rope_halves(b_w_uq[0], MLA_HEADS), MLA_HEADS,
                      MLA_NOPE + MLA_ROPE).astype(BF16)
    x2d = _mixer_b(x2d, b_w_in[0].astype(BF16), row(b_q_norm_g[0]), wuq, wuqs, ctab, stab,
                   k_all, vt_all, memkv, b_w_out[0].astype(BF16),
                   row(ln1_g[1]), row(ln1_b[1]), batch, seq)
    x2d = _ffn(x2d, ffn_w_up[1].astype(BF16), ffn_w_down[1].astype(BF16),
               row(ln2_g[1]), row(ln2_b[1]))
    return x2d.reshape(batch, seq, D_MODEL)
```

```python
import jax
import jax.numpy as jnp
from jax import lax
from jax.experimental import pallas as pl
from jax.experimental.pallas import tpu as pltpu

F32 = jnp.float32
BF16 = jnp.bfloat16

D_MODEL = 1024
DEPTH = 2
N_MEM = 256
MLSTM_HEADS = 4
MLSTM_QK = 64
MLSTM_V = 128
MEM_HEADS = 4
MEM_DIM = 128
MLA_HEADS = 8
MLA_NOPE = 64
MLA_ROPE = 32
MLA_V = 64
Q_LORA = 256
KV_LORA = 256
D_FF = 4 * D_MODEL
ROPE_THETA = 10000.0
LN_EPS = 1e-5
RMS_EPS = 1e-6
ALPHA = (2 * DEPTH) ** 0.25
MLA_CHUNK = 64
BF16_SUBLANES = 16
VT_ROWS = MLA_V + BF16_SUBLANES
QK_AHEAD = 2
LOG2_E = 1.4426950408889634

LANES = 128
HEAD_PAD = 128
ROPE_LO = MLA_NOPE
ROPE_HALF = MLA_ROPE // 2

MLSTM_CHUNK = 256
MLSTM_CHUNKS = 2
TS_A = MLSTM_CHUNK * MLSTM_CHUNKS
HEADS_LOCKSTEP = 2
GATE_ROWS = 2 * MLSTM_HEADS
COL_EXPO, COL_INTER, COL_EINV, COL_WGT = (i * GATE_ROWS for i in range(4))
TQ_B = 256
TK_B = 256
MIXB_TILES = 4
TM_FFN = 1024
TM_MEM = 512
FF_CHUNK = 2048
FFN_ROWS = TK_B
V7X_VMEM_BYTES = 64 * 1024 * 1024
VMEM_LIMIT = V7X_VMEM_BYTES - 8 * 1024 * 1024

NT_DIMS = (((1,), (1,)), ((), ()))
TN_DIMS = (((0,), (0,)), ((), ()))


def _dot(a, b):
    return jnp.dot(a, b, preferred_element_type=F32)


def _dot_nt(a, b):
    return lax.dot_general(a, b, NT_DIMS, preferred_element_type=F32)


def _layer_norm(y, g, b):
    mu = jnp.mean(y, axis=-1, keepdims=True)
    yc = y - mu
    var = jnp.mean(yc * yc, axis=-1, keepdims=True)
    return yc * lax.rsqrt(var + LN_EPS) * g + b


def _log_sigmoid(z):
    return jnp.minimum(z, 0.0) - jnp.log(1.0 + jnp.exp(-jnp.abs(z)))


def _out_proj_norm(x_ref, cat_ref, wout_ref, g_ref, b_ref, o_ref):
    mix = _dot(cat_ref[...], wout_ref[...])
    o_ref[...] = _layer_norm(ALPHA * x_ref[...] + mix, g_ref[...], b_ref[...])


def _const_spec(shape):
    nd = len(shape)
    return pl.BlockSpec(shape, lambda *_: (0,) * nd, pipeline_mode=pl.Buffered(1))


def _rope_rows(pos, invf):
    ang = invf * pos
    cos = jnp.cos(ang)
    sin = jnp.sin(ang)
    n = ang.shape[1]
    tail = LANES - ROPE_LO - MLA_ROPE
    ct = jnp.concatenate([jnp.ones((ROPE_LO, n), F32), cos, cos, jnp.ones((tail, n), F32)], axis=0)
    st = jnp.concatenate([jnp.zeros((ROPE_LO, n), F32), -sin, sin, jnp.zeros((tail, n), F32)], axis=0)
    return ct.T, st.T


def _mem_kv_kernel(mem_ref, w_ref, o_ref):
    o_ref[...] = _dot(mem_ref[...].astype(BF16), w_ref[...]).astype(BF16)


def _mem_kv(mem2d, w):
    r, n = mem2d.shape[0], w.shape[1]
    return pl.pallas_call(
        _mem_kv_kernel,
        out_shape=jax.ShapeDtypeStruct((r, n), BF16),
        grid=(r // TM_MEM,),
        in_specs=[pl.BlockSpec((TM_MEM, D_MODEL), lambda i: (i, 0)),
                  _const_spec(w.shape)],
        out_specs=pl.BlockSpec((TM_MEM, n), lambda i: (i, 0)),
        compiler_params=pltpu.CompilerParams(dimension_semantics=("parallel",),
                                             vmem_limit_bytes=VMEM_LIMIT),
        name="mem_kv",
    )(mem2d, w)


def _memory_scores(q_all, mkv_ref, h):
    q_scale = MEM_DIM ** -0.5 * LOG2_E
    lo = h * MEM_DIM
    qh = (q_all[:, lo:lo + MEM_DIM] * q_scale).astype(BF16)
    return _dot_nt(qh, mkv_ref[:, lo:lo + MEM_DIM])


def _memory_output(s, mkv_ref, cat_ref, col0, h):
    lo = h * MEM_DIM
    v0 = MEM_HEADS * MEM_DIM + lo
    p = jnp.exp2(s - jnp.max(s, axis=-1, keepdims=True)).astype(BF16)
    v_ext = jnp.concatenate([mkv_ref[:, v0:v0 + MEM_DIM], jnp.ones((N_MEM, LANES), BF16)], axis=1)
    o = _dot(p, v_ext)
    cat_ref[:, col0 + lo:col0 + lo + MEM_DIM] = (o[:, :MEM_DIM] / o[:, MEM_DIM:]).astype(BF16)


def _lane_scan(x, combine, fill):
    n = x.shape[1]
    lane = lax.broadcasted_iota(jnp.int32, x.shape, 1)
    d = 1
    while d < n:
        x = combine(x, jnp.where(lane >= d, pltpu.roll(x, d, axis=1), fill))
        d *= 2
    return x


def _mlstm_gates(gate_all, m_prev, n_chunks):
    length = gate_all.shape[1] // n_chunks
    rows = GATE_ROWS
    gate = jnp.concatenate([gate_all[:, c * length:(c + 1) * length] for c in range(n_chunks)],
                           axis=0)
    head_row = lax.broadcasted_iota(jnp.int32, gate.shape, 0) % rows < MLSTM_HEADS
    cum_f = _lane_scan(_log_sigmoid(gate), jnp.add, 0.0)
    a_all = jnp.where(head_row, gate, 0.0)
    b_all = jnp.where(head_row, pltpu.roll(cum_f, gate.shape[0] - MLSTM_HEADS, axis=0), 0.0)
    r_all = a_all - b_all
    mi_all = b_all + _lane_scan(r_all, jnp.maximum, -jnp.inf)
    out = []
    for c in range(n_chunks):
        b, r, m_intra = (t[c * rows:(c + 1) * rows] for t in (b_all, r_all, mi_all))
        g_tot = b[:, length - 1:length]
        m_inter = b + m_prev
        m_t = jnp.maximum(m_inter, m_intra)
        m_new = jnp.maximum(g_tot + m_prev, jnp.max(g_tot + r, axis=1, keepdims=True))
        decay = jnp.exp(g_tot + m_prev - m_new)
        stack = jnp.concatenate([(b - m_t) * LOG2_E, jnp.exp(m_inter - m_t), jnp.exp(-m_t),
                                 jnp.exp(g_tot + r - m_new),
                                 jnp.zeros((LANES - 4 * rows, length), F32)], axis=0)
        out.append((r * LOG2_E, stack.T, decay))
        m_prev = m_new
    return out, m_prev


def _mixer_a_kernel(x_ref, wmain_ref, wgr_ref, bgr_ref, mkv_ref, wout_ref, g_ref, b_ref, o_ref,
                    c_st, m_st, cat_ref):
    hq = MLSTM_HEADS * MLSTM_QK
    hv = MLSTM_HEADS * MLSTM_V
    cl = MLSTM_CHUNK

    @pl.when(pl.program_id(1) == 0)
    def _():
        c_st[...] = jnp.zeros_like(c_st)
        m_st[...] = jnp.zeros_like(m_st)

    causal = (lax.broadcasted_iota(jnp.int32, (cl, cl), 1)
              <= lax.broadcasted_iota(jnp.int32, (cl, cl), 0))
    ones_blk = jnp.ones((cl, LANES), BF16)
    lane_half = lax.broadcasted_iota(jnp.int32, (cl, LANES), 1) // MLSTM_QK

    xbs = [x_ref[c * cl:(c + 1) * cl, :].astype(BF16) for c in range(MLSTM_CHUNKS)]
    gate_all = jnp.concatenate([_dot_nt(wgr_ref[...], xb) for xb in xbs], axis=1) + bgr_ref[...]
    gates, m_st[...] = _mlstm_gates(gate_all, m_st[...], MLSTM_CHUNKS)

    def head_scores(c, proj, h):
        blk = slice((h // 2) * LANES, (h // 2 + 1) * LANES)
        mine = lane_half == h % 2
        q = jnp.where(mine, proj[:, blk], 0.0).astype(BF16)
        k = proj[:, hq:2 * hq][:, blk] * (MLSTM_QK ** -0.5)
        v = proj[:, 2 * hq + h * MLSTM_V:2 * hq + (h + 1) * MLSTM_V].astype(BF16)
        v_ext = jnp.concatenate([v, ones_blk], axis=1)
        return q, k, v_ext, mine, _dot_nt(q, k.astype(BF16))

    def head_pv(c, h, parts):
        r2, cols, _ = gates[c]
        _, _, v_ext, _, qk = parts
        expo = jnp.where(causal, cols[:, COL_EXPO + h:COL_EXPO + h + 1] + r2[h:h + 1, :], -jnp.inf)
        p = qk * jnp.exp2(expo)
        return _dot(p.astype(BF16), v_ext)

    def head_finish(c, proj, h, parts, num):
        _, cols, decay = gates[c]
        q, k, v_ext, mine, _ = parts
        rows = slice(c * cl, (c + 1) * cl)
        o_pre = proj[:, 2 * hq + hv + h * MLSTM_V:2 * hq + hv + (h + 1) * MLSTM_V]
        c_prev = c_st[h]
        inter_b = jnp.broadcast_to(cols[:, COL_INTER + h:COL_INTER + h + 1], (cl, MLSTM_V))
        einv_b = jnp.broadcast_to(cols[:, COL_EINV + h:COL_EINV + h + 1], (cl, MLSTM_V))
        qc = _dot(q, c_prev.astype(BF16))
        nq = num[:, MLSTM_V:] + inter_b * qc[:, MLSTM_V:]
        hh = (num[:, :MLSTM_V] + inter_b * qc[:, :MLSTM_V]) / jnp.maximum(jnp.abs(nq), einv_b)
        hh = hh * jax.nn.sigmoid(o_pre)
        cat_ref[rows, h * MLSTM_V:(h + 1) * MLSTM_V] = hh.astype(BF16)
        kw = jnp.where(mine, k * cols[:, COL_WGT + h:COL_WGT + h + 1], 0.0).astype(BF16)
        c_st[h] = decay[h:h + 1, :] * c_prev + lax.dot_general(kw, v_ext, TN_DIMS,
                                                               preferred_element_type=F32)

    chunks = range(MLSTM_CHUNKS)
    projs = [_dot(xbs[c], wmain_ref[...]) for c in chunks]
    for h0 in range(0, MLSTM_HEADS, HEADS_LOCKSTEP):
        group = [(c, h) for h in range(h0, h0 + HEADS_LOCKSTEP) for c in chunks]
        parts = [head_scores(c, projs[c], h) for c, h in group]
        nums = [head_pv(c, h, parts[i]) for i, (c, h) in enumerate(group)]
        for i, (c, h) in enumerate(group):
            head_finish(c, projs[c], h, parts[i], nums[i])
    row_slices = [slice(c * cl, (c + 1) * cl) for c in chunks]
    for c in chunks:
        q_mem = projs[c][:, 2 * hq + 2 * hv:]
        for h in range(MEM_HEADS):
            _memory_output(_memory_scores(q_mem, mkv_ref, h), mkv_ref, cat_ref.at[row_slices[c]],
                           hv, h)
    for rows in row_slices:
        _out_proj_norm(x_ref.at[rows], cat_ref.at[rows], wout_ref, g_ref, b_ref, o_ref.at[rows])


def _mixer_a(x2d, wmain, wgr, bgr, memkv, wout, g, b, batch, seq):
    ns = seq // TS_A
    width = MLSTM_HEADS * MLSTM_V + MEM_HEADS * MEM_DIM
    return pl.pallas_call(
        _mixer_a_kernel,
        out_shape=jax.ShapeDtypeStruct(x2d.shape, F32),
        grid=(batch, ns),
        in_specs=[pl.BlockSpec((TS_A, D_MODEL), lambda bi, si: (bi * ns + si, 0)),
                  _const_spec(wmain.shape), _const_spec(wgr.shape), _const_spec(bgr.shape),
                  pl.BlockSpec((N_MEM, 2 * MEM_HEADS * MEM_DIM), lambda bi, si: (bi, 0)),
                  _const_spec(wout.shape), _const_spec(g.shape), _const_spec(b.shape)],
        out_specs=pl.BlockSpec((TS_A, D_MODEL), lambda bi, si: (bi * ns + si, 0)),
        scratch_shapes=[pltpu.VMEM((MLSTM_HEADS, LANES, 2 * MLSTM_V), F32),
                        pltpu.VMEM((GATE_ROWS, 1), F32),
                        pltpu.VMEM((TS_A, width), BF16)],
        compiler_params=pltpu.CompilerParams(dimension_semantics=("parallel", "arbitrary"),
                                             vmem_limit_bytes=VMEM_LIMIT),
        name="mixer_a",
    )(x2d, wmain, wgr, bgr, memkv, wout, g, b)


def _shared_kv_down(y, wd_ref):
    return _dot(y.astype(BF16), wd_ref[...])


def _shared_kv_rows(d, ctab, stab, gk_ref, wuk_ref, wuvt_ref):
    ckv = d[:, :KV_LORA]
    ckv = ckv * lax.rsqrt(jnp.mean(ckv * ckv, axis=-1, keepdims=True) + RMS_EPS) * gk_ref[...]
    ckv = ckv.astype(BF16)
    k_rope = (d[:, KV_LORA:KV_LORA + LANES] * ctab
              + d[:, KV_LORA + LANES:KV_LORA + 2 * LANES] * stab)
    k_nope = _dot(ckv, wuk_ref[...])
    k = jnp.concatenate([(k_nope[:, h * HEAD_PAD:(h + 1) * HEAD_PAD] + k_rope).astype(BF16)
                         for h in range(MLA_HEADS)], axis=1)
    vt = _dot_nt(wuvt_ref[...], ckv).astype(BF16)
    ones = jnp.ones((BF16_SUBLANES, vt.shape[1]), BF16)
    pieces = []
    for h in range(MLA_HEADS):
        pieces += [vt[h * MLA_V:(h + 1) * MLA_V], ones]
    return k, jnp.concatenate(pieces, axis=0)


def _ffn_rows(x, wup_ref, wdn_ref, g_ref, b_ref):
    xb = x.astype(BF16)
    acc = jnp.zeros(x.shape, F32)
    for j in range(D_FF // FF_CHUNK):
        hid = _dot(xb, wup_ref[:, j * FF_CHUNK:(j + 1) * FF_CHUNK])
        hid = jnp.square(jnp.maximum(hid, 0.0)).astype(BF16)
        acc = acc + _dot(hid, wdn_ref[j * FF_CHUNK:(j + 1) * FF_CHUNK, :])
    return _layer_norm(ALPHA * x + acc, g_ref[...], b_ref[...])


def _ffn_kernel(x_ref, wup_ref, wdn_ref, g_ref, b_ref, o_ref):
    for r in range(x_ref.shape[0] // FFN_ROWS):
        rows = slice(r * FFN_ROWS, (r + 1) * FFN_ROWS)
        o_ref[rows, :] = _ffn_rows(x_ref[rows, :], wup_ref, wdn_ref, g_ref, b_ref)


def _ffn_kv_kernel(x_ref, wup_ref, wdn_ref, g_ref, b_ref, pos_ref, invf_ref, wd_ref, gk_ref,
                   wuk_ref, wuvt_ref, o_ref, k_ref, vt_ref, c_ref, s_ref):
    n_sub = x_ref.shape[0] // FFN_ROWS
    ys, downs = [], []
    for r in range(n_sub + 2):
        if r < n_sub:
            rows = slice(r * FFN_ROWS, (r + 1) * FFN_ROWS)
            c_ref[rows, :], s_ref[rows, :] = _rope_rows(pos_ref[:, rows], invf_ref[...])
            ys.append(_ffn_rows(x_ref[rows, :], wup_ref, wdn_ref, g_ref, b_ref))
            o_ref[rows, :] = ys[r]
        if 1 <= r <= n_sub:
            downs.append(_shared_kv_down(ys[r - 1], wd_ref))
        if r >= 2:
            prev = slice((r - 2) * FFN_ROWS, (r - 1) * FFN_ROWS)
            k_ref[prev, :], vt_ref[r - 2] = _shared_kv_rows(
                downs[r - 2], c_ref[prev, :], s_ref[prev, :], gk_ref, wuk_ref, wuvt_ref)


def _layer_spec(w, layer):
    return pl.BlockSpec((None,) + w.shape[1:], lambda *_: (layer, 0, 0),
                        pipeline_mode=pl.Buffered(1))


def _ffn(x2d, wup, wdn, layer, g, b, kv_args=None):
    t = x2d.shape[0]
    tile = pl.BlockSpec((TM_FFN, D_MODEL), lambda i: (i, 0))
    in_specs = [tile, _layer_spec(wup, layer), _layer_spec(wdn, layer),
                _const_spec(g.shape), _const_spec(b.shape)]
    params = pltpu.CompilerParams(dimension_semantics=("parallel",), vmem_limit_bytes=VMEM_LIMIT)
    if kv_args is None:
        return pl.pallas_call(
            _ffn_kernel, out_shape=jax.ShapeDtypeStruct(x2d.shape, F32), grid=(t // TM_FFN,),
            in_specs=in_specs, out_specs=tile, compiler_params=params, name="ffn",
        )(x2d, wup, wdn, g, b)
    kw, vw = MLA_HEADS * HEAD_PAD, MLA_HEADS * VT_ROWS
    table = pl.BlockSpec((TM_FFN, LANES), lambda i: (i, 0))
    in_specs += [pl.BlockSpec((1, TM_FFN), lambda i: (0, i))]
    in_specs += [_const_spec(w.shape) for w in kv_args[1:]]
    return pl.pallas_call(
        _ffn_kv_kernel,
        out_shape=(jax.ShapeDtypeStruct(x2d.shape, F32), jax.ShapeDtypeStruct((t, kw), BF16),
                   jax.ShapeDtypeStruct((t // TK_B, vw, TK_B), BF16),
                   jax.ShapeDtypeStruct((t, LANES), F32), jax.ShapeDtypeStruct((t, LANES), F32)),
        grid=(t // TM_FFN,), in_specs=in_specs,
        out_specs=(tile, pl.BlockSpec((TM_FFN, kw), lambda i: (i, 0)),
                   pl.BlockSpec((TM_FFN // TK_B, vw, TK_B), lambda i: (i, 0, 0)), table, table),
        compiler_params=params, name="ffn_kv",
    )(x2d, wup, wdn, g, b, *kv_args)


def _mixer_b_kernel(x_ref, win_ref, gq_ref, wuq_ref, wuqs_ref, c_ref, s_ref, k_ref, vt_ref,
                    mkv_ref, wout_ref, g_ref, b_ref, o_ref, q_sc, sa_sc, sb_sc, ma_sc, mb_sc, m_sc,
                    acc_sc, ot_sc, cat_ref):
    tq = TQ_B
    step = pl.program_id(1)
    q_scale = (MLA_NOPE + MLA_ROPE) ** -0.5 * LOG2_E
    key_chunk = lax.broadcasted_iota(jnp.int32, (TK_B, tq), 0) // MLA_CHUNK
    qry_chunk = lax.broadcasted_iota(jnp.int32, (TK_B, tq), 1) // MLA_CHUNK
    allowed = key_chunk <= qry_chunk

    def in_proj(rows):
        return _dot(x_ref[rows, :].astype(BF16), win_ref[...])

    def queries(rows, proj):
        cq = proj[:, :Q_LORA]
        cq = cq * lax.rsqrt(jnp.mean(cq * cq, axis=-1, keepdims=True) + RMS_EPS) * gq_ref[...]
        cq = cq.astype(BF16)
        ctab = c_ref[rows, :]
        stab = s_ref[rows, :]
        for pair in range(MLA_HEADS // 2):
            cols = slice(2 * pair * HEAD_PAD, 2 * (pair + 1) * HEAD_PAD)
            q_lin = _dot(cq, wuq_ref[:, cols])
            q_swp = _dot(cq, wuqs_ref[:, cols])
            for h in range(2):
                sl = slice(h * HEAD_PAD, (h + 1) * HEAD_PAD)
                dst = slice((2 * pair + h) * HEAD_PAD, (2 * pair + h + 1) * HEAD_PAD)
                q_sc[rows, dst] = ((q_lin[:, sl] * ctab + q_swp[:, sl] * stab)
                                   * q_scale).astype(BF16)

    def attention(rows, idx):
        qi = MIXB_TILES * step + idx
        odd = idx % 2

        def put_scores(dst, j, h):
            kb = k_ref[pl.ds(pl.multiple_of(j * TK_B, TK_B), TK_B), h * HEAD_PAD:(h + 1) * HEAD_PAD]
            s = _dot_nt(kb, q_sc[rows, h * HEAD_PAD:(h + 1) * HEAD_PAD])
            dst[0][h] = s
            dst[1][h] = jnp.max(s, axis=0, keepdims=True)

        def softmax_pv(j, h, src, masked):
            s = src[0][h]
            if masked:
                s = jnp.where(allowed, s, -jnp.inf)
                tile_max = jnp.max(s, axis=0, keepdims=True)
            else:
                tile_max = src[1][h]
            m_old = m_sc[h]
            m_new = jnp.maximum(m_old, tile_max)
            corr = jnp.exp2(m_old - m_new)
            p = jnp.exp2(s - m_new).astype(BF16)
            acc_sc[h] = corr * acc_sc[h] + _dot(vt_ref[j, h * VT_ROWS:(h + 1) * VT_ROWS, :], p)
            m_sc[h] = m_new

        def key_tile(j, src, dst, masked):
            if dst is not None:
                for h in range(QK_AHEAD):
                    put_scores(dst, j + 1, h)
            for h in range(MLA_HEADS):
                softmax_pv(j, h, src, masked)
                if dst is not None and h + QK_AHEAD < MLA_HEADS:
                    put_scores(dst, j + 1, h + QK_AHEAD)

        buf_a, buf_b = (sa_sc, ma_sc), (sb_sc, mb_sc)
        m_sc[...] = jnp.full(m_sc.shape, -jnp.inf, F32)
        acc_sc[...] = jnp.zeros_like(acc_sc)
        for h in range(MLA_HEADS):
            put_scores(buf_a, 0, h)

        def tile_pair(k, carry):
            key_tile(2 * k, buf_a, buf_b, False)
            key_tile(2 * k + 1, buf_b, buf_a, False)
            return carry

        lax.fori_loop(0, qi // 2, tile_pair, 0)
        if odd:
            key_tile(qi - 1, buf_a, buf_b, False)
            key_tile(qi, buf_b, None, True)
        else:
            key_tile(qi, buf_a, None, True)
        for h in range(MLA_HEADS):
            acc = acc_sc[h]
            ot_sc[h * MLA_V:(h + 1) * MLA_V, :] = acc[:MLA_V] / acc[MLA_V:MLA_V + 1]
        cat_ref[rows, :MLA_HEADS * MLA_V] = ot_sc[...].T.astype(BF16)

    tiles = [slice(i * tq, (i + 1) * tq) for i in range(MIXB_TILES)]
    proj = in_proj(tiles[0])
    queries(tiles[0], proj)
    for i, rows in enumerate(tiles):
        attention(rows, i)
        q_mem = proj[:, Q_LORA:]
        mem_scores = [_memory_scores(q_mem, mkv_ref, h) for h in range(MEM_HEADS)]
        if i + 1 < MIXB_TILES:
            proj = in_proj(tiles[i + 1])
        for h in range(MEM_HEADS):
            _memory_output(mem_scores[h], mkv_ref, cat_ref.at[rows], MLA_HEADS * MLA_V, h)
        if i + 1 < MIXB_TILES:
            queries(tiles[i + 1], proj)
        _out_proj_norm(x_ref.at[rows], cat_ref.at[rows], wout_ref, g_ref, b_ref, o_ref.at[rows])


def _mixer_b(x2d, win, gq, wuq, wuqs, ctab, stab, k_all, vt_all, memkv, wout, g, b, batch, seq):
    assert TQ_B == TK_B
    assert MIXB_TILES % 2 == 0
    ts = MIXB_TILES * TQ_B
    ns = seq // ts
    width = MLA_HEADS * MLA_V + MEM_HEADS * MEM_DIM
    tile = lambda bi, si: (bi * ns + si, 0)
    return pl.pallas_call(
        _mixer_b_kernel,
        out_shape=jax.ShapeDtypeStruct(x2d.shape, F32),
        grid=(batch, ns),
        in_specs=[pl.BlockSpec((ts, D_MODEL), tile),
                  _const_spec(win.shape), _const_spec(gq.shape),
                  _const_spec(wuq.shape), _const_spec(wuqs.shape),
                  pl.BlockSpec((ts, LANES), tile), pl.BlockSpec((ts, LANES), tile),
                  pl.BlockSpec((seq, k_all.shape[1]), lambda bi, si: (bi, 0)),
                  pl.BlockSpec((seq // TK_B,) + vt_all.shape[1:], lambda bi, si: (bi, 0, 0)),
                  pl.BlockSpec((N_MEM, 2 * MEM_HEADS * MEM_DIM), lambda bi, si: (bi, 1)),
                  _const_spec(wout.shape), _const_spec(g.shape), _const_spec(b.shape)],
        out_specs=pl.BlockSpec((ts, D_MODEL), tile),
        scratch_shapes=[pltpu.VMEM((ts, MLA_HEADS * HEAD_PAD), BF16),
                        pltpu.VMEM((MLA_HEADS, TK_B, TQ_B), F32),
                        pltpu.VMEM((MLA_HEADS, TK_B, TQ_B), F32),
                        pltpu.VMEM((MLA_HEADS, 1, TQ_B), F32),
                        pltpu.VMEM((MLA_HEADS, 1, TQ_B), F32),
                        pltpu.VMEM((MLA_HEADS, 1, TQ_B), F32),
                        pltpu.VMEM((MLA_HEADS, VT_ROWS, TQ_B), F32),
                        pltpu.VMEM((MLA_HEADS * MLA_V, TQ_B), F32),
                        pltpu.VMEM((ts, width), BF16)],
        compiler_params=pltpu.CompilerParams(dimension_semantics=("parallel", "arbitrary"),
                                             vmem_limit_bytes=VMEM_LIMIT),
        name="mixer_b",
    )(x2d, win, gq, wuq, wuqs, ctab, stab, k_all, vt_all, memkv, wout, g, b)


def _pad_heads(w, heads, dim):
    r = w.shape[0]
    w = w.reshape(r, heads, dim)
    w = jnp.pad(w, ((0, 0), (0, 0), (0, HEAD_PAD - dim)))
    return w.reshape(r, heads * HEAD_PAD)


def _swap_rope_halves(w, heads):
    r = w.shape[0]
    w = w.reshape(r, heads, MLA_NOPE + MLA_ROPE)
    x1 = w[..., MLA_NOPE:MLA_NOPE + ROPE_HALF]
    x2 = w[..., MLA_NOPE + ROPE_HALF:]
    return jnp.concatenate([jnp.zeros_like(w[..., :MLA_NOPE]), x2, x1], axis=-1).reshape(r, -1)


def kernel(x, mem, positions, a_w_in, a_b_igate, a_b_fgate, a_w_mem_kv, a_w_out, kv_w_down, kv_norm_g, kv_w_uk, kv_w_uv, b_w_in, b_q_norm_g, b_w_uq, b_w_mem_kv, b_w_out, ln1_g, ln1_b, ffn_w_up, ffn_w_down, ln2_g, ln2_b):
    batch, seq, _ = x.shape
    t = batch * seq
    x2d = x.reshape(t, D_MODEL)
    row = lambda v: v.reshape(1, -1).astype(F32)

    memkv = _mem_kv(mem.reshape(batch * N_MEM, D_MODEL),
                    jnp.concatenate([a_w_mem_kv[0], b_w_mem_kv[0]], axis=1).astype(BF16))

    hq = MLSTM_HEADS * MLSTM_QK
    hv = MLSTM_HEADS * MLSTM_V
    g0 = 2 * hq + 2 * hv
    w_in = a_w_in[0]
    wmain = jnp.concatenate([w_in[:, :g0], w_in[:, g0 + GATE_ROWS:]], axis=1).astype(BF16)
    wgr = w_in[:, g0:g0 + GATE_ROWS].T.astype(BF16)
    bgr = jnp.concatenate([a_b_igate[0], a_b_fgate[0]]).astype(F32).reshape(GATE_ROWS, 1)
    x2d = _mixer_a(x2d, wmain, wgr, bgr, memkv, a_w_out[0].astype(BF16),
                   row(ln1_g[0]), row(ln1_b[0]), batch, seq)

    inv_freq = ROPE_THETA ** (-jnp.arange(0, MLA_ROPE, 2, dtype=F32) / MLA_ROPE)
    wd = jnp.zeros((D_MODEL, KV_LORA + 2 * LANES), F32)
    wd = wd.at[:, :KV_LORA].set(kv_w_down[:, :KV_LORA])
    r0 = KV_LORA + ROPE_LO
    wd = wd.at[:, r0:r0 + MLA_ROPE].set(kv_w_down[:, KV_LORA:])
    r1 = KV_LORA + LANES + ROPE_LO
    wd = wd.at[:, r1:r1 + ROPE_HALF].set(kv_w_down[:, KV_LORA + ROPE_HALF:])
    wd = wd.at[:, r1 + ROPE_HALF:r1 + MLA_ROPE].set(kv_w_down[:, KV_LORA:KV_LORA + ROPE_HALF])
    kv_args = (positions.reshape(1, t).astype(F32), inv_freq.reshape(ROPE_HALF, 1),
               wd.astype(BF16), row(kv_norm_g),
               _pad_heads(kv_w_uk, MLA_HEADS, MLA_NOPE).astype(BF16), kv_w_uv.T.astype(BF16))
    ffn_up, ffn_down = ffn_w_up.astype(BF16), ffn_w_down.astype(BF16)
    x2d, k_all, vt_all, ctab, stab = _ffn(x2d, ffn_up, ffn_down, 0,
                                          row(ln2_g[0]), row(ln2_b[0]), kv_args)

    wuq = _pad_heads(b_w_uq[0], MLA_HEADS, MLA_NOPE + MLA_ROPE).astype(BF16)
    wuqs = _pad_heads(_swap_rope_halves(b_w_uq[0], MLA_HEADS), MLA_HEADS,
                      MLA_NOPE + MLA_ROPE).astype(BF16)
    x2d = _mixer_b(x2d, b_w_in[0].astype(BF16), row(b_q_norm_g[0]), wuq, wuqs, ctab, stab,
                   k_all, vt_all, memkv, b_w_out[0].astype(BF16),
                   row(ln1_g[1]), row(ln1_b[1]), batch, seq)
    x2d = _ffn(x2d, ffn_up, ffn_down, 1, row(ln2_g[1]), row(ln2_b[1]))
    return x2d.reshape(batch, seq, D_MODEL)
```
